```python
import math
import jax
import jax.numpy as jnp
from jax import lax
import numpy as np

D_MODEL = 1024
BATCH = 2
SEQ = 16384
DEPTH = 2

GRID_W = 64
CTX_LEN = 256
EPS = 1e-6
N_MOD = 6

POOL_GROUPS = 4
POOL_GDIM = D_MODEL // 16
POOL_DIM = POOL_GROUPS * POOL_GDIM
POOL_WINDOWS = (2, 4, 8, 16)

SSD_HEADS = 4
SSD_HEAD_DIM = 64
SSD_INNER = SSD_HEADS * SSD_HEAD_DIM
SSD_GROUPS = 2
SSD_STATE = 64
SSD_CONV = 5
SSD_CHUNK = 128
SSD_DIRS = 2
SSD_XBC = SSD_INNER + 2 * SSD_GROUPS * SSD_STATE
SSD_PROJ = SSD_INNER + SSD_XBC + SSD_DIRS * SSD_HEADS

ATTN_HEADS = 4
ATTN_KV_HEADS = 2
HEAD_DIM = 128
ATTN_REP = ATTN_HEADS // ATTN_KV_HEADS
ATTN_DIM = ATTN_HEADS * HEAD_DIM
KV_DIM = ATTN_KV_HEADS * HEAD_DIM
ATTN_PROJ = ATTN_DIM + 2 * KV_DIM
Q_BLOCK = 128
ROPE_THETA = 10000.0

D_MIX = POOL_DIM + SSD_INNER + ATTN_DIM
D_IN = POOL_DIM + SSD_PROJ + ATTN_PROJ
D_FF = -(-(8 * D_MODEL) // (3 * 256)) * 256

kernel_name = 'hybrid_pool_ssd_gqa_dit_block'


def rmsnorm(x, w):
    xf = x.astype(jnp.float32)
    xf = xf * lax.rsqrt(jnp.mean(xf * xf, axis=-1, keepdims=True) + EPS)
    return xf.astype(x.dtype) * w


def modulation(cond, w_mod, b_mod):
    m = jax.nn.silu(cond) @ w_mod + b_mod
    return [t[:, None, :] for t in jnp.split(m, N_MOD, axis=-1)]


def adaln(h, w, shift, scale):
    return rmsnorm(h, w) * (1 + scale) + shift


def centred_mean_minus_self(u, w):
    L = u.shape[1]
    cs = jnp.cumsum(u.astype(jnp.float32), axis=1)
    cs = jnp.concatenate([jnp.zeros_like(cs[:, :1]), cs], axis=1)
    t = jnp.arange(L)
    lo = jnp.clip(t - w // 2, 0, L)
    hi = jnp.clip(t - w // 2 + w, 0, L)
    cnt = (hi - lo).astype(jnp.float32)[None, :, None]
    mean = (jnp.take(cs, hi, axis=1) - jnp.take(cs, lo, axis=1)) / cnt
    return mean.astype(u.dtype) - u


def pool_mixer(u, pool_w, pool_scale):
    Bsz, L, _ = u.shape
    ug = u.reshape(Bsz, L, POOL_GROUPS, POOL_GDIM)
    pooled = jnp.stack([centred_mean_minus_self(ug[:, :, g], w) for g, w in enumerate(POOL_WINDOWS)], axis=2)
    out = jnp.einsum('blgc,gcd->blgd', pooled, pool_w).reshape(Bsz, L, POOL_DIM)
    return out * pool_scale


def dwconv_centred(u, w, b):
    pad = w.shape[0] // 2
    out = lax.conv_general_dilated(u, w[:, None, :].astype(u.dtype), window_strides=(1,),
                                   padding=[(pad, pad)], dimension_numbers=('NWC', 'WIO', 'NWC'),
                                   feature_group_count=u.shape[-1])
    return out + b


def ssd_prep(u, conv_w, conv_b):
    Bsz, L, _ = u.shape
    z = u[..., :SSD_INNER]
    xbc = jax.nn.silu(dwconv_centred(u[..., SSD_INNER:SSD_INNER + SSD_XBC], conv_w, conv_b))
    xh = xbc[..., :SSD_INNER].reshape(Bsz, L, SSD_HEADS, SSD_HEAD_DIM)
    bc = xbc[..., SSD_INNER:].reshape(Bsz, L, 2, SSD_GROUPS, SSD_STATE)
    rep = SSD_HEADS // SSD_GROUPS
    Bm = jnp.repeat(bc[:, :, 0], rep, axis=2)
    Cm = jnp.repeat(bc[:, :, 1], rep, axis=2)
    dt_raw = u[..., SSD_INNER + SSD_XBC:].reshape(Bsz, L, SSD_DIRS, SSD_HEADS)
    return z, xh, Bm, Cm, dt_raw


def segsum(a):
    T = a.shape[-1]
    cs = jnp.cumsum(a, axis=-1)
    diff = cs[..., :, None] - cs[..., None, :]
    mask = jnp.tril(jnp.ones((T, T), dtype=bool))
    return jnp.where(mask, diff, -jnp.inf)


def ssd_scan(xh, dt, A, Bm, Cm, h0, with_output):
    Bsz, L, H, P = xh.shape
    N = Bm.shape[-1]
    nc = L // SSD_CHUNK
    x = (xh * dt[..., None]).reshape(Bsz, nc, SSD_CHUNK, H, P)
    a = (dt * A).reshape(Bsz, nc, SSD_CHUNK, H).transpose(0, 3, 1, 2)
    Bc = Bm.reshape(Bsz, nc, SSD_CHUNK, H, N)
    Cc = Cm.reshape(Bsz, nc, SSD_CHUNK, H, N)
    a_cs = jnp.cumsum(a, axis=-1)
    decay_states = jnp.exp(a_cs[..., -1:] - a_cs)
    states = jnp.einsum('bclhn,bhcl,bclhp->bchpn', Bc, decay_states, x)
    states = jnp.concatenate([h0[:, None].astype(states.dtype), states], axis=1)
    chunk_decay = jnp.exp(segsum(jnp.pad(a_cs[..., -1], ((0, 0), (0, 0), (1, 0)))))
    new_states = jnp.einsum('bhzc,bchpn->bzhpn', chunk_decay, states)
    prev_states, final = new_states[:, :-1], new_states[:, -1]
    if not with_output:
        return None, final
    Lmat = jnp.exp(segsum(a))
    scores = jnp.einsum('bclhn,bcshn->bhcls', Cc, Bc) * Lmat
    y_diag = jnp.einsum('bhcls,bcshp->bclhp', scores, x)
    y_off = jnp.einsum('bclhn,bchpn,bhcl->bclhp', Cc, prev_states, jnp.exp(a_cs))
    return (y_diag + y_off).reshape(Bsz, L, H, P), final


def gated_rmsnorm(y, z, w):
    Bsz, L = y.shape[:2]
    g = (y.reshape(Bsz, L, SSD_INNER) * jax.nn.silu(z.astype(jnp.float32))).reshape(Bsz, L, SSD_GROUPS, -1)
    g = g * lax.rsqrt(jnp.mean(g * g, axis=-1, keepdims=True) + EPS)
    return g.reshape(Bsz, L, SSD_INNER).astype(z.dtype) * w


def ssd_bidirectional(pc, px, dt_bias, a_log, d_skip, ssd_norm_w, need_ctx):
    zc, xc, Bc, Cc, dtc = pc
    zx, xx, Bx, Cx, dtx = px
    Bsz = xx.shape[0]
    y_x = jnp.zeros(xx.shape, jnp.float32)
    y_c = jnp.zeros(xc.shape, jnp.float32)
    for d in range(SSD_DIRS):
        flip = (lambda t: jnp.flip(t, axis=1)) if d == 1 else (lambda t: t)
        A = -jnp.exp(a_log[d].astype(jnp.float32))
        dt_c = jax.nn.softplus(dtc[:, :, d].astype(jnp.float32) + dt_bias[d])
        dt_x = jax.nn.softplus(dtx[:, :, d].astype(jnp.float32) + dt_bias[d])
        h0 = jnp.zeros((Bsz, SSD_HEADS, SSD_HEAD_DIM, SSD_STATE), jnp.float32)
        yc_d, hc = ssd_scan(flip(xc), flip(dt_c), A, flip(Bc), flip(Cc), h0, need_ctx)
        yx_d, _ = ssd_scan(flip(xx), flip(dt_x), A, flip(Bx), flip(Cx), hc, True)
        y_x = y_x + flip(yx_d) + d_skip[d][:, None] * xx
        if need_ctx:
            y_c = y_c + flip(yc_d) + d_skip[d][:, None] * xc
    out_x = gated_rmsnorm(y_x, zx, ssd_norm_w)
    out_c = gated_rmsnorm(y_c, zc, ssd_norm_w) if need_ctx else None
    return out_c, out_x


def axial_rope_tables(L):
    rows = L // GRID_W
    row_ids = jnp.repeat(jnp.arange(rows, dtype=jnp.float32), GRID_W)
    col_ids = jnp.tile(jnp.arange(GRID_W, dtype=jnp.float32), rows)
    axis_dim = HEAD_DIM // 2
    inv_freq = ROPE_THETA ** (-jnp.arange(0, axis_dim, 2, dtype=jnp.float32) / axis_dim)
    ang_r = row_ids[:, None] * inv_freq[None, :]
    ang_c = col_ids[:, None] * inv_freq[None, :]
    return jnp.cos(ang_r), jnp.sin(ang_r), jnp.cos(ang_c), jnp.sin(ang_c)


def rope_half(x, cos, sin):
    m = x.shape[-1] // 2
    x1, x2 = x[..., :m], x[..., m:]
    cos = cos[:, None, :].astype(x.dtype)
    sin = sin[:, None, :].astype(x.dtype)
    return jnp.concatenate([x1 * cos - x2 * sin, x1 * sin + x2 * cos], axis=-1)


def apply_axial_rope(x, tables):
    cr, sr, cc, sc = tables
    h = HEAD_DIM // 2
    return jnp.concatenate([rope_half(x[..., :h], cr, sr), rope_half(x[..., h:], cc, sc)], axis=-1)


def attn_split(u, q_norm_w, k_norm_w):
    Bsz, L, _ = u.shape
    q = rmsnorm(u[..., :ATTN_DIM].reshape(Bsz, L, ATTN_HEADS, HEAD_DIM), q_norm_w)
    k = rmsnorm(u[..., ATTN_DIM:ATTN_DIM + KV_DIM].reshape(Bsz, L, ATTN_KV_HEADS, HEAD_DIM), k_norm_w)
    v = u[..., ATTN_DIM + KV_DIM:].reshape(Bsz, L, ATTN_KV_HEADS, HEAD_DIM)
    return q, k, v


def gqa_attend(qblk, k, v):
    s = jnp.einsum('bqgrd,bkgd->bgrqk', qblk, k).astype(jnp.float32) * (HEAD_DIM ** -0.5)
    p = jax.nn.softmax(s, axis=-1).astype(v.dtype)
    return jnp.einsum('bgrqk,bkgd->bqgrd', p, v)


def latent_attention(q, k_all, v_all):
    Bsz, L = q.shape[:2]
    nb = L // Q_BLOCK
    qb = q.reshape(Bsz, nb, Q_BLOCK, ATTN_KV_HEADS, ATTN_REP, HEAD_DIM).transpose(1, 0, 2, 3, 4, 5)
    out = lax.map(lambda qblk: gqa_attend(qblk, k_all, v_all), qb)
    return out.transpose(1, 0, 2, 3, 4, 5).reshape(Bsz, L, ATTN_DIM)


def context_attention(q, k, v):
    Bsz, L = q.shape[:2]
    qg = q.reshape(Bsz, L, ATTN_KV_HEADS, ATTN_REP, HEAD_DIM)
    return gqa_attend(qg, k, v).reshape(Bsz, L, ATTN_DIM)


def swiglu(h, w1, w3, w2):
    return (jax.nn.silu(h @ w1) * (h @ w3)) @ w2


def hybrid_layer(xs, cs, c, c_ctx, norm1_w, norm2_w, w_mod, b_mod, w_in, pool_w, pool_scale,
                 conv_w, conv_b, dt_bias, a_log, d_skip, ssd_norm_w, q_norm_w, k_norm_w,
                 w_out, w1, w3, w2, rope, need_ctx):
    sh1x, sc1x, g1x, sh2x, sc2x, g2x = modulation(c, w_mod, b_mod)
    sh1c, sc1c, g1c, sh2c, sc2c, g2c = modulation(c_ctx[None, :], w_mod, b_mod)
    ux = adaln(xs, norm1_w, sh1x, sc1x) @ w_in
    uc = adaln(cs, norm1_w, sh1c, sc1c) @ w_in
    o_ssd, o_att = POOL_DIM, POOL_DIM + SSD_PROJ

    pool_x = pool_mixer(ux[..., :o_ssd], pool_w, pool_scale)
    ssd_c, ssd_x = ssd_bidirectional(ssd_prep(uc[..., o_ssd:o_att], conv_w, conv_b),
                                     ssd_prep(ux[..., o_ssd:o_att], conv_w, conv_b),
                                     dt_bias, a_log, d_skip, ssd_norm_w, need_ctx)
    qx, kx, vx = attn_split(ux[..., o_att:], q_norm_w, k_norm_w)
    qc, kc, vc = attn_split(uc[..., o_att:], q_norm_w, k_norm_w)
    qx = apply_axial_rope(qx, rope)
    kx = apply_axial_rope(kx, rope)
    attn_x = latent_attention(qx, jnp.concatenate([kc, kx], axis=1), jnp.concatenate([vc, vx], axis=1))

    mix_x = jnp.concatenate([pool_x, ssd_x.astype(xs.dtype), attn_x], axis=-1) @ w_out
    xs = xs + g1x * mix_x
    xs = xs + g2x * swiglu(adaln(xs, norm2_w, sh2x, sc2x), w1, w3, w2)

    if need_ctx:
        pool_c = pool_mixer(uc[..., :o_ssd], pool_w, pool_scale)
        attn_c = context_attention(qc, kc, vc)
        mix_c = jnp.concatenate([pool_c, ssd_c.astype(cs.dtype), attn_c], axis=-1) @ w_out
        cs = cs + g1c * mix_c
        cs = cs + g2c * swiglu(adaln(cs, norm2_w, sh2c, sc2c), w1, w3, w2)
    return xs, cs


def setup_inputs(seed: int = 0) -> dict:
    key = jax.random.key(seed)
    ks = jax.random.split(key, 24)
    f32 = jnp.float32

    def nrm(k, shape, scale):
        return jax.random.normal(k, shape, f32) * scale

    dt0 = jnp.exp(jax.random.uniform(ks[13], (DEPTH, SSD_DIRS, SSD_HEADS), f32, math.log(1e-3), math.log(1e-1)))
    return {
        'x': nrm(ks[0], (BATCH, SEQ, D_MODEL), 1.0),
        'c': nrm(ks[1], (BATCH, D_MODEL), 1.0),
        'ctx': nrm(ks[2], (BATCH, CTX_LEN, D_MODEL), 1.0),
        'c_ctx': nrm(ks[3], (D_MODEL,), 1.0),
        'norm1_w': 1.0 + nrm(ks[4], (DEPTH, D_MODEL), 0.05),
        'norm2_w': 1.0 + nrm(ks[5], (DEPTH, D_MODEL), 0.05),
        'w_mod': nrm(ks[6], (DEPTH, D_MODEL, N_MOD * D_MODEL), 0.5 * D_MODEL ** -0.5),
        'b_mod': nrm(ks[7], (DEPTH, N_MOD * D_MODEL), 0.02),
        'w_in': nrm(ks[8], (DEPTH, D_MODEL, D_IN), D_MODEL ** -0.5),
        'pool_w': nrm(ks[9], (DEPTH, POOL_GROUPS, POOL_GDIM, POOL_GDIM), POOL_GDIM ** -0.5),
        'pool_scale': 1.0 + nrm(ks[10], (DEPTH, POOL_DIM), 0.1),
        'conv_w': nrm(ks[11], (DEPTH, SSD_CONV, SSD_XBC), SSD_CONV ** -0.5),
        'conv_b': nrm(ks[12], (DEPTH, SSD_XBC), 0.02),
        'dt_bias': dt0 + jnp.log(-jnp.expm1(-dt0)),
        'a_log': jnp.log(jax.random.uniform(ks[14], (DEPTH, SSD_DIRS, SSD_HEADS), f32, 1.0, 16.0)),
        'd_skip': 1.0 + nrm(ks[15], (DEPTH, SSD_DIRS, SSD_HEADS), 0.1),
        'ssd_norm_w': 1.0 + nrm(ks[16], (DEPTH, SSD_INNER), 0.05),
        'q_norm_w': 1.0 + nrm(ks[17], (DEPTH, HEAD_DIM), 0.05),
        'k_norm_w': 1.0 + nrm(ks[18], (DEPTH, HEAD_DIM), 0.05),
        'w_out': nrm(ks[19], (DEPTH, D_MIX, D_MODEL), D_MIX ** -0.5),
        'w1': nrm(ks[20], (DEPTH, D_MODEL, D_FF), D_MODEL ** -0.5),
        'w3': nrm(ks[21], (DEPTH, D_MODEL, D_FF), D_MODEL ** -0.5),
        'w2': nrm(ks[22], (DEPTH, D_FF, D_MODEL), D_FF ** -0.5),
    }


def reference(x, c, ctx, c_ctx, norm1_w, norm2_w, w_mod, b_mod, w_in, pool_w, pool_scale,
              conv_w, conv_b, dt_bias, a_log, d_skip, ssd_norm_w, q_norm_w, k_norm_w,
              w_out, w1, w3, w2):
    rope = axial_rope_tables(x.shape[1])
    xs, cs = x, ctx
    for layer in range(DEPTH):
        xs, cs = hybrid_layer(xs, cs, c, c_ctx, norm1_w[layer], norm2_w[layer], w_mod[layer], b_mod[layer],
                              w_in[layer], pool_w[layer], pool_scale[layer], conv_w[layer], conv_b[layer],
                              dt_bias[layer], a_log[layer], d_skip[layer], ssd_norm_w[layer],
                              q_norm_w[layer], k_norm_w[layer], w_out[layer], w1[layer], w3[layer], w2[layer],
                              rope, need_ctx=(layer < DEPTH - 1))
    return xs
```

```python
import functools

import jax
import jax.numpy as jnp
from jax import lax
from jax.experimental import pallas as pl
from jax.experimental.pallas import tpu as pltpu

F32 = jnp.float32
BF16 = jnp.bfloat16

GRID_W = 64
EPS = 1e-6
N_MOD = 6
POOL_DIM = 256
POOL_GDIM = 64
POOL_GROUPS = 4
SSD_HEADS = 4
SSD_HEAD_DIM = 64
SSD_INNER = 256
SSD_GROUPS = 2
SSD_STATE = 64
SSD_CONV = 5
SSD_CHUNK = 128
SSD_DIRS = 2
SSD_XBC = 512
ATTN_HEADS = 4
ATTN_KV_HEADS = 2
HEAD_DIM = 128
ATTN_DIM = 512
KV_DIM = 256
ROPE_THETA = 10000.0

LANES = 128
SUBLANES = 8
TOKEN_TILE = 512
ATTN_Q_TILE = 256
ATTN_K_TILE = 512
VMEM_LIMIT = 56 * 1024 * 1024

C_POOL, C_Z, C_XBC, C_DT, C_Q, C_K, C_V, C_END = 0, 256, 512, 1024, 1152, 1664, 1920, 2176


def _silu(x):
    return x * (1.0 / (1.0 + jnp.exp(-x)))


def _softplus(x):
    return jnp.maximum(x, 0.0) + jnp.log1p(jnp.exp(-jnp.abs(x)))


def _const_spec(shape):
    nd = len(shape)
    return pl.BlockSpec(shape, lambda *_: (0,) * nd, pipeline_mode=pl.Buffered(1))


def _mod_kernel(cond_ref, w_ref, b_ref, o_ref):
    s = _silu(cond_ref[...]).astype(BF16)
    o_ref[...] = jnp.dot(s, w_ref[...].astype(BF16), preferred_element_type=F32) + b_ref[...]


def _modulation(cond, w_mod, b_mod):
    depth, d, n = w_mod.shape
    bn = d
    return pl.pallas_call(
        _mod_kernel,
        out_shape=jax.ShapeDtypeStruct((depth, SUBLANES, n), F32),
        grid=(depth, n // bn),
        in_specs=[
            pl.BlockSpec((SUBLANES, d), lambda l, j: (0, 0)),
            pl.BlockSpec((None, d, bn), lambda l, j: (l, 0, j)),
            pl.BlockSpec((None, 1, bn), lambda l, j: (l, 0, j)),
        ],
        out_specs=pl.BlockSpec((None, SUBLANES, bn), lambda l, j: (l, 0, j)),
        compiler_params=pltpu.CompilerParams(
            dimension_semantics=("arbitrary", "arbitrary"), vmem_limit_bytes=VMEM_LIMIT),
        name="modulation",
    )(cond, w_mod, b_mod.reshape(depth, 1, n))


def _rope_partner(x):
    lane = lax.broadcasted_iota(jnp.int32, x.shape, 1)
    fwd = pltpu.roll(x, LANES - 32, 1)
    bwd = pltpu.roll(x, 32, 1)
    return jnp.where((lane % 64) < 32, fwd, bwd)


def _inproj_kernel(xl_ref, xc_ref, mod_ref, n1_ref, w_ref, qw_ref, kw_ref, cos_ref, sin_ref,
                   pool_ref, z_ref, xbc_ref, dt_ref, q_ref, k_ref, v_ref, *, n_lat_tiles):
    i = pl.program_id(0)
    x = jnp.where(i < n_lat_tiles, xl_ref[...], xc_ref[...])
    ms = jnp.mean(x * x, axis=-1, keepdims=True)
    h = x * lax.rsqrt(ms + EPS) * n1_ref[...]
    h = h * (1.0 + mod_ref[1:2, :]) + mod_ref[0:1, :]
    u = jnp.dot(h.astype(BF16), w_ref[...], preferred_element_type=F32)
    pool_ref[...] = u[:, C_POOL:C_Z]
    z_ref[...] = u[:, C_Z:C_XBC]
    xbc_ref[...] = u[:, C_XBC:C_DT]
    dt_ref[...] = u[:, C_DT:C_Q]
    v_ref[...] = u[:, C_V:C_END].astype(BF16)
    cos = cos_ref[...]
    sin = sin_ref[...]

    def norm_rope(t, w):
        t = t * lax.rsqrt(jnp.mean(t * t, axis=-1, keepdims=True) + EPS) * w
        return t * cos + _rope_partner(t) * sin

    for hd in range(ATTN_HEADS):
        t = norm_rope(u[:, C_Q + hd * HEAD_DIM:C_Q + (hd + 1) * HEAD_DIM], qw_ref[...])
        q_ref[:, hd * HEAD_DIM:(hd + 1) * HEAD_DIM] = (t * (HEAD_DIM ** -0.5)).astype(BF16)
    for hd in range(ATTN_KV_HEADS):
        t = norm_rope(u[:, C_K + hd * HEAD_DIM:C_K + (hd + 1) * HEAD_DIM], kw_ref[...])
        k_ref[:, hd * HEAD_DIM:(hd + 1) * HEAD_DIM] = t.astype(BF16)


def _inproj(xs_lat, xs_ctx, mod, n1, w_cat, qw, kw, cos_tab, sin_tab, batch):
    n_lat, d = xs_lat.shape
    n_ctx = xs_ctx.shape[0]
    tm = TOKEN_TILE
    assert n_ctx == tm and n_lat % (batch * tm) == 0
    n_lat_tiles = n_lat // tm
    tiles_per_seq = n_lat_tiles // batch
    n_tok = n_lat + n_ctx
    grid = (n_lat_tiles + 1,)

    def lat_idx(i):
        return jnp.minimum(i, n_lat_tiles - 1)

    def rope_idx(i):
        return jnp.where(i < n_lat_tiles, i % tiles_per_seq, tiles_per_seq)

    def mod_idx(i):
        return jnp.where(i < n_lat_tiles, i // tiles_per_seq, batch)

    row = lambda w: pl.BlockSpec((tm, w), lambda i: (i, 0))
    outs = [(POOL_DIM, F32), (SSD_INNER, F32), (SSD_XBC, F32), (LANES, F32),
            (ATTN_DIM, BF16), (KV_DIM, BF16), (KV_DIM, BF16)]
    return pl.pallas_call(
        functools.partial(_inproj_kernel, n_lat_tiles=n_lat_tiles),
        out_shape=[jax.ShapeDtypeStruct((n_tok, w), t) for w, t in outs],
        grid=grid,
        in_specs=[
            pl.BlockSpec((tm, d), lambda i: (lat_idx(i), 0)),
            pl.BlockSpec((tm, d), lambda i: (0, 0)),
            pl.BlockSpec((None, N_MOD, d), lambda i: (mod_idx(i), 0, 0)),
            _const_spec((1, d)),
            _const_spec(w_cat.shape),
            _const_spec((1, HEAD_DIM)),
            _const_spec((1, HEAD_DIM)),
            pl.BlockSpec((tm, HEAD_DIM), lambda i: (rope_idx(i), 0)),
            pl.BlockSpec((tm, HEAD_DIM), lambda i: (rope_idx(i), 0)),
        ],
        out_specs=[row(w) for w, _ in outs],
        compiler_params=pltpu.CompilerParams(
            dimension_semantics=("arbitrary",), vmem_limit_bytes=VMEM_LIMIT),
        name="inproj",
    )(xs_lat, xs_ctx, mod, n1, w_cat, qw, kw, cos_tab, sin_tab)


def _ssd_kernel(xf_ref, xfp_ref, xfn_ref, dtf_ref, xb_ref, xbp_ref, xbn_ref, dtb_ref,
                cw_ref, cb_ref, bias_ref, alog_ref, dsk_ref,
                yf_ref, yb_ref, buf_ref, h_ref, *, n_ctx_ch, n_ch):
    i = pl.program_id(1)
    q = SSD_CHUNK

    @pl.when(i == 0)
    def _():
        h_ref[...] = jnp.zeros_like(h_ref)

    starts = jnp.logical_or(i == 0, i == n_ctx_ch)
    ends = jnp.logical_or(i == n_ctx_ch - 1, i == n_ch - 1)

    row = lax.broadcasted_iota(jnp.int32, (q, q), 0)
    col = lax.broadcasted_iota(jnp.int32, (q, q), 1)
    tri = (row >= col).astype(F32)
    a_vec = -jnp.exp(alog_ref[...])

    def conv_silu(x_ref, prev_ref, next_ref, zero_prev, zero_next):
        buf_ref[0:SUBLANES, :] = jnp.where(zero_prev, 0.0, prev_ref[...])
        buf_ref[SUBLANES:SUBLANES + q, :] = x_ref[...]
        buf_ref[SUBLANES + q:2 * SUBLANES + q, :] = jnp.where(zero_next, 0.0, next_ref[...])
        acc = cb_ref[...] + cw_ref[0:1, :] * buf_ref[SUBLANES - 2:SUBLANES - 2 + q, :]
        for k in range(1, SSD_CONV):
            acc = acc + cw_ref[k:k + 1, :] * buf_ref[SUBLANES - 2 + k:SUBLANES - 2 + k + q, :]
        return _silu(acc)

    def direction(d, xc, dt_raw, y_ref):
        dt = _softplus(dt_raw + bias_ref[...])
        a = dt * a_vec
        cs = jnp.dot(tri, a, preferred_element_type=F32, precision=lax.Precision.HIGHEST)
        if d == 0:
            e = cs
        else:
            e = cs - a
        e_t = e.T
        tot = cs[q - 1:q, :]
        for g in range(SSD_GROUPS):
            b_g = xc[:, SSD_INNER + g * SSD_STATE:SSD_INNER + (g + 1) * SSD_STATE]
            c_g = xc[:, SSD_INNER + (SSD_GROUPS + g) * SSD_STATE:
                     SSD_INNER + (SSD_GROUPS + g + 1) * SSD_STATE]
            c_bf = c_g.astype(BF16)
            cb = lax.dot_general(c_bf, b_g.astype(BF16), (((1,), (1,)), ((), ())),
                                 preferred_element_type=F32)
            for hh in range(SSD_HEADS // SSD_GROUPS):
                hd = g * (SSD_HEADS // SSD_GROUPS) + hh
                c = d * SSD_HEADS + hd
                e_col = e[:, c:c + 1]
                e_row = e_t[c:c + 1, :]
                tot_c = tot[:, c:c + 1]
                x_h = xc[:, hd * SSD_HEAD_DIM:(hd + 1) * SSD_HEAD_DIM]
                xdt = (x_h * dt[:, c:c + 1]).astype(BF16)
                if d == 0:
                    lmat = jnp.where(row >= col, jnp.exp(e_col - e_row), 0.0)
                    w_in = jnp.exp(e_col)
                    w_out = jnp.exp(tot_c - e_col)
                else:
                    lmat = jnp.where(col >= row, jnp.exp(e_row - e_col), 0.0)
                    w_in = jnp.exp(tot_c - e_col)
                    w_out = jnp.exp(e_col)
                hs = h_ref[c]
                y = jnp.dot((cb * lmat).astype(BF16), xdt, preferred_element_type=F32)
                y = y + jnp.dot(c_bf, hs.astype(BF16), preferred_element_type=F32) * w_in
                y = y + dsk_ref[d:d + 1, hd * SSD_HEAD_DIM:(hd + 1) * SSD_HEAD_DIM] * x_h
                y_ref[:, hd * SSD_HEAD_DIM:(hd + 1) * SSD_HEAD_DIM] = y
                bd = (b_g * w_out).astype(BF16)
                upd = lax.dot_general(bd, xdt, (((0,), (0,)), ((), ())),
                                      preferred_element_type=F32)
                h_ref[c] = jnp.exp(tot_c) * hs + upd

    xc_f = conv_silu(xf_ref, xfp_ref, xfn_ref, starts, ends)
    direction(0, xc_f, dtf_ref[...], yf_ref)
    xc_b = conv_silu(xb_ref, xbp_ref, xbn_ref, ends, starts)
    direction(1, xc_b, dtb_ref[...], yb_ref)


def _ssd(xbc, dt_raw, conv_w, conv_b, dt_bias, a_log, d_skip, batch, n_lat, n_ctx):
    n_tok = xbc.shape[0]
    q = SSD_CHUNK
    lat_ch = n_lat // batch // q
    ctx_ch = n_ctx // batch // q
    n_ch = lat_ch + ctx_ch
    blk8 = q // SUBLANES
    last8 = n_tok // SUBLANES - 1

    def fwd(b, i):
        return jnp.where(i < ctx_ch, batch * lat_ch + b * ctx_ch + i, b * lat_ch + i - ctx_ch)

    def bwd(b, i):
        return jnp.where(i < ctx_ch, batch * lat_ch + b * ctx_ch + ctx_ch - 1 - i,
                         b * lat_ch + n_ch - 1 - i)

    def specs(ch):
        return [
            pl.BlockSpec((q, SSD_XBC), lambda b, i: (ch(b, i), 0)),
            pl.BlockSpec((SUBLANES, SSD_XBC), lambda b, i: (jnp.maximum(ch(b, i) * blk8 - 1, 0), 0)),
            pl.BlockSpec((SUBLANES, SSD_XBC),
                         lambda b, i: (jnp.minimum(ch(b, i) * blk8 + blk8, last8), 0)),
            pl.BlockSpec((q, LANES), lambda b, i: (ch(b, i), 0)),
        ]

    cw = jnp.zeros((SUBLANES, SSD_XBC), F32).at[:SSD_CONV].set(conv_w)
    pad = lambda v: jnp.zeros((1, LANES), F32).at[0, :SSD_DIRS * SSD_HEADS].set(v.reshape(-1))
    dsk = jnp.repeat(d_skip, SSD_HEAD_DIM, axis=1)
    return pl.pallas_call(
        functools.partial(_ssd_kernel, n_ctx_ch=ctx_ch, n_ch=n_ch),
        out_shape=[jax.ShapeDtypeStruct((n_tok, SSD_INNER), F32)] * 2,
        grid=(batch, n_ch),
        in_specs=specs(fwd) + specs(bwd) + [
            _const_spec((SUBLANES, SSD_XBC)), _const_spec((1, SSD_XBC)),
            _const_spec((1, LANES)), _const_spec((1, LANES)), _const_spec((SSD_DIRS, SSD_INNER)),
        ],
        out_specs=[pl.BlockSpec((q, SSD_INNER), lambda b, i: (fwd(b, i), 0)),
                   pl.BlockSpec((q, SSD_INNER), lambda b, i: (bwd(b, i), 0))],
        scratch_shapes=[pltpu.VMEM((q + 2 * SUBLANES, SSD_XBC), F32),
                        pltpu.VMEM((SSD_DIRS * SSD_HEADS, SSD_STATE, SSD_HEAD_DIM), F32)],
        compiler_params=pltpu.CompilerParams(
            dimension_semantics=("arbitrary", "arbitrary"), vmem_limit_bytes=VMEM_LIMIT),
        name="ssd_scan",
    )(xbc, xbc, xbc, dt_raw, xbc, xbc, xbc, dt_raw,
      cw, conv_b.reshape(1, -1), pad(dt_bias), pad(a_log), dsk)


def _softmax_step(q, k, v, carry):
    s = lax.dot_general(q, k, (((1,), (1,)), ((), ())), preferred_element_type=F32)
    m_new = jnp.max(s, axis=1, keepdims=True)
    if carry is None:
        p = jnp.exp(s - m_new)
        l_new = jnp.sum(p, axis=1, keepdims=True)
        acc = jnp.dot(p.astype(BF16), v, preferred_element_type=F32)
        return m_new, l_new, acc
    m, l, acc = carry
    m_new = jnp.maximum(m, m_new)
    alpha = jnp.exp(m - m_new)
    p = jnp.exp(s - m_new)
    l_new = alpha * l + jnp.sum(p, axis=1, keepdims=True)
    acc = alpha * acc + jnp.dot(p.astype(BF16), v, preferred_element_type=F32)
    return m_new, l_new, acc


def _attn_kernel(q_ref, kc_ref, vc_ref, kx_ref, vx_ref, o_ref, *, n_chunks):
    tq = q_ref.shape[0]
    q = jnp.concatenate([q_ref[:, 0:HEAD_DIM], q_ref[:, HEAD_DIM:2 * HEAD_DIM]], axis=0)
    carry = _softmax_step(q, kc_ref[...], vc_ref[...], None)

    def body(j, carry):
        start = pl.multiple_of(j * ATTN_K_TILE, ATTN_K_TILE)
        return _softmax_step(q, kx_ref[pl.ds(start, ATTN_K_TILE), :],
                             vx_ref[pl.ds(start, ATTN_K_TILE), :], carry)

    if n_chunks:
        carry = lax.fori_loop(0, n_chunks, body, carry)
    _, l, acc = carry
    o = (acc / l).astype(o_ref.dtype)
    o_ref[:, 0:HEAD_DIM] = o[0:tq]
    o_ref[:, HEAD_DIM:2 * HEAD_DIM] = o[tq:2 * tq]


def _attention_latent(q, k, v, batch, n_lat, n_ctx):
    seq = n_lat // batch
    ctx = n_ctx // batch
    tq = ATTN_Q_TILE
    nq = seq // tq
    assert seq % ATTN_K_TILE == 0 and n_lat % ctx == 0
    ctx_blk0 = n_lat // ctx
    kv_c = pl.BlockSpec((ctx, HEAD_DIM), lambda b, g, j: (ctx_blk0 + b, g))
    kv_x = pl.BlockSpec((seq, HEAD_DIM), lambda b, g, j: (b, g))
    return pl.pallas_call(
        functools.partial(_attn_kernel, n_chunks=seq // ATTN_K_TILE),
        out_shape=jax.ShapeDtypeStruct((n_lat, ATTN_DIM), BF16),
        grid=(batch, ATTN_KV_HEADS, nq),
        in_specs=[pl.BlockSpec((tq, 2 * HEAD_DIM), lambda b, g, j: (b * nq + j, g)),
                  kv_c, kv_c, kv_x, kv_x],
        out_specs=pl.BlockSpec((tq, 2 * HEAD_DIM), lambda b, g, j: (b * nq + j, g)),
        compiler_params=pltpu.CompilerParams(
            dimension_semantics=("arbitrary",) * 3, vmem_limit_bytes=VMEM_LIMIT),
        name="attn_latent",
    )(q, k, v, k, v)


def _attn_ctx_kernel(q_ref, kc_ref, vc_ref, o_ref):
    _attn_kernel(q_ref, kc_ref, vc_ref, None, None, o_ref, n_chunks=0)


def _attention_context(q, k, v, batch, n_lat, n_ctx):
    ctx = n_ctx // batch
    ctx_blk0 = n_lat // ctx
    kv_c = pl.BlockSpec((ctx, HEAD_DIM), lambda b, g: (ctx_blk0 + b, g))
    return pl.pallas_call(
        _attn_ctx_kernel,
        out_shape=jax.ShapeDtypeStruct((n_ctx, ATTN_DIM), BF16),
        grid=(batch, ATTN_KV_HEADS),
        in_specs=[pl.BlockSpec((ctx, 2 * HEAD_DIM), lambda b, g: (ctx_blk0 + b, g)), kv_c, kv_c],
        out_specs=pl.BlockSpec((ctx, 2 * HEAD_DIM), lambda b, g: (b, g)),
        compiler_params=pltpu.CompilerParams(
            dimension_semantics=("arbitrary",) * 2, vmem_limit_bytes=VMEM_LIMIT),
        name="attn_context",
    )(q, k, v)


def _post_kernel(xs_ref, up_ref, upp_ref, upn_ref, z_ref, yf_ref, yb_ref, at_ref, mod_ref,
                 pw_ref, ps_ref, sw_ref, wo_ref, n2_ref, w1_ref, w3_ref, w2_ref,
                 o_ref, buf_ref, *, tiles_per_seq, seq_len):
    i = pl.program_id(0)
    tm = xs_ref.shape[0]
    halo = SUBLANES
    j = i % tiles_per_seq

    buf_ref[0:halo, :] = jnp.where(j == 0, 0.0, upp_ref[...])
    buf_ref[halo:halo + tm, :] = up_ref[...]
    buf_ref[halo + tm:2 * halo + tm, :] = jnp.where(j == tiles_per_seq - 1, 0.0, upn_ref[...])
    ub = buf_ref[...]
    n = tm + 2 * halo
    s2 = ub + pltpu.roll(ub, 1, 0)
    s4 = pltpu.roll(s2, 1, 0) + pltpu.roll(s2, n - 1, 0)
    s8 = pltpu.roll(s4, 2, 0) + pltpu.roll(s4, n - 2, 0)
    s16 = pltpu.roll(s8, 4, 0) + pltpu.roll(s8, n - 4, 0)
    shape = (tm, POOL_DIM)
    grp = lax.broadcasted_iota(jnp.int32, shape, 1) // POOL_GDIM
    t = lax.broadcasted_iota(jnp.int32, shape, 0) + j * tm
    half = jnp.left_shift(1, grp)
    cnt = (jnp.minimum(t + half, seq_len) - jnp.maximum(t - half, 0)).astype(F32)
    sl = slice(halo, halo + tm)
    wsum = jnp.where(grp == 0, s2[sl], jnp.where(grp == 1, s4[sl],
                                                 jnp.where(grp == 2, s8[sl], s16[sl])))
    pooled = wsum / cnt - ub[sl]
    pool = jnp.dot(pooled.astype(BF16), pw_ref[...], preferred_element_type=F32) * ps_ref[...]

    gy = (yf_ref[...] + yb_ref[...]) * _silu(z_ref[...])
    gw = SSD_INNER // SSD_GROUPS
    parts = []
    for g in range(SSD_GROUPS):
        t_g = gy[:, g * gw:(g + 1) * gw]
        parts.append(t_g * lax.rsqrt(jnp.mean(t_g * t_g, axis=-1, keepdims=True) + EPS))
    ssd = jnp.concatenate(parts, axis=1) * sw_ref[...]

    mix = jnp.concatenate([pool.astype(BF16), ssd.astype(BF16), at_ref[...]], axis=1)
    x1 = xs_ref[...] + mod_ref[2:3, :] * jnp.dot(mix, wo_ref[...], preferred_element_type=F32)

    h = x1 * lax.rsqrt(jnp.mean(x1 * x1, axis=-1, keepdims=True) + EPS) * n2_ref[...]
    h = (h * (1.0 + mod_ref[4:5, :]) + mod_ref[3:4, :]).astype(BF16)
    a = jnp.dot(h, w1_ref[...], preferred_element_type=F32)
    b = jnp.dot(h, w3_ref[...], preferred_element_type=F32)
    ff = jnp.dot((_silu(a) * b).astype(BF16), w2_ref[...], preferred_element_type=F32)
    o_ref[...] = x1 + mod_ref[5:6, :] * ff


def _post(xs, upool, z, yf, yb, attn, mod, pw_bd, pscale, ssd_w, w_out, n2, w1, w3, w2,
          *, tm, row0, tiles_per_seq, mod_row):
    n_rows, d = xs.shape
    n_tiles = n_rows // tm
    off = row0 // tm
    blk8 = tm // SUBLANES
    last8 = upool.shape[0] // SUBLANES - 1
    tok = lambda w: pl.BlockSpec((tm, w), lambda i: (off + i, 0))
    loc = lambda w: pl.BlockSpec((tm, w), lambda i: (i, 0))
    return pl.pallas_call(
        functools.partial(_post_kernel, tiles_per_seq=tiles_per_seq, seq_len=tiles_per_seq * tm),
        out_shape=jax.ShapeDtypeStruct((n_rows, d), F32),
        grid=(n_tiles,),
        in_specs=[
            loc(d),
            tok(POOL_DIM),
            pl.BlockSpec((SUBLANES, POOL_DIM), lambda i: (jnp.maximum((off + i) * blk8 - 1, 0), 0)),
            pl.BlockSpec((SUBLANES, POOL_DIM),
                         lambda i: (jnp.minimum((off + i) * blk8 + blk8, last8), 0)),
            tok(SSD_INNER), tok(SSD_INNER), tok(SSD_INNER),
            loc(ATTN_DIM),
            pl.BlockSpec((None, N_MOD, d), lambda i: (mod_row(i), 0, 0)),
            _const_spec(pw_bd.shape), _const_spec((1, POOL_DIM)), _const_spec((1, SSD_INNER)),
            _const_spec(w_out.shape), _const_spec((1, d)),
            _const_spec(w1.shape), _const_spec(w3.shape), _const_spec(w2.shape),
        ],
        out_specs=loc(d),
        scratch_shapes=[pltpu.VMEM((tm + 2 * SUBLANES, POOL_DIM), F32)],
        compiler_params=pltpu.CompilerParams(
            dimension_semantics=("arbitrary",), vmem_limit_bytes=VMEM_LIMIT),
        name="post_mix_ffn",
    )(xs, upool, upool, upool, z, yf, yb, attn, mod, pw_bd, pscale, ssd_w, w_out, n2, w1, w3, w2)


def _rope_tables(seq, extra_rows):
    rows = seq // GRID_W
    row_ids = jnp.repeat(jnp.arange(rows, dtype=F32), GRID_W)
    col_ids = jnp.tile(jnp.arange(GRID_W, dtype=F32), rows)
    axis_dim = HEAD_DIM // 2
    inv_freq = ROPE_THETA ** (-jnp.arange(0, axis_dim, 2, dtype=F32) / axis_dim)
    ang_r = row_ids[:, None] * inv_freq[None, :]
    ang_c = col_ids[:, None] * inv_freq[None, :]
    cr, sr, cc, sc = jnp.cos(ang_r), jnp.sin(ang_r), jnp.cos(ang_c), jnp.sin(ang_c)
    cos = jnp.concatenate([cr, cr, cc, cc], axis=1)
    sin = jnp.concatenate([-sr, sr, -sc, sc], axis=1)
    cos = jnp.concatenate([cos, jnp.ones((extra_rows, HEAD_DIM), F32)], axis=0)
    sin = jnp.concatenate([sin, jnp.zeros((extra_rows, HEAD_DIM), F32)], axis=0)
    return cos, sin


def _fuse_w_in(w_in):
    d = w_in.shape[0]
    o_ssd, o_dt = POOL_DIM, POOL_DIM + SSD_INNER + SSD_XBC
    o_att = o_dt + SSD_DIRS * SSD_HEADS
    dt_pad = jnp.zeros((d, C_Q - C_DT - SSD_DIRS * SSD_HEADS), w_in.dtype)
    return jnp.concatenate([w_in[:, :o_dt], w_in[:, o_dt:o_att], dt_pad, w_in[:, o_att:]],
                           axis=1).astype(BF16)


def kernel(x, c, ctx, c_ctx, norm1_w, norm2_w, w_mod, b_mod, w_in, pool_w, pool_scale, conv_w, conv_b,
           dt_bias, a_log, d_skip, ssd_norm_w, q_norm_w, k_norm_w, w_out, w1, w3, w2):
    batch, seq, d = x.shape
    ctx_len = ctx.shape[1]
    depth = w_mod.shape[0]
    n_lat, n_ctx = batch * seq, batch * ctx_len
    assert batch + 1 <= SUBLANES and seq % TOKEN_TILE == 0 and seq % GRID_W == 0

    cond = jnp.zeros((SUBLANES, d), F32).at[:batch].set(c).at[batch].set(c_ctx)
    mod_all = _modulation(cond, w_mod, b_mod).reshape(depth, SUBLANES, N_MOD, d)
    cos_tab, sin_tab = _rope_tables(seq, TOKEN_TILE)

    xs_lat = x.reshape(n_lat, d)
    xs_ctx = ctx.reshape(n_ctx, d)
    tiles_per_seq = seq // TOKEN_TILE
    for layer in range(depth):
        need_ctx = layer < depth - 1
        mod = mod_all[layer]
        w_cat = _fuse_w_in(w_in[layer])
        upool, z, xbc, dt_raw, q, k, v = _inproj(
            xs_lat, xs_ctx, mod, norm1_w[layer].reshape(1, d), w_cat,
            q_norm_w[layer].reshape(1, -1), k_norm_w[layer].reshape(1, -1), cos_tab, sin_tab, batch)
        yf, yb = _ssd(xbc, dt_raw, conv_w[layer], conv_b[layer], dt_bias[layer], a_log[layer],
                      d_skip[layer], batch, n_lat, n_ctx)
        attn_x = _attention_latent(q, k, v, batch, n_lat, n_ctx)
        pw_bd = jax.scipy.linalg.block_diag(*[pool_w[layer, g] for g in range(POOL_GROUPS)]).astype(BF16)
        post = functools.partial(
            _post, upool=upool, z=z, yf=yf, yb=yb, mod=mod, pw_bd=pw_bd,
            pscale=pool_scale[layer].reshape(1, -1), ssd_w=ssd_norm_w[layer].reshape(1, -1),
            w_out=w_out[layer].astype(BF16), n2=norm2_w[layer].reshape(1, d),
            w1=w1[layer].astype(BF16), w3=w3[layer].astype(BF16), w2=w2[layer].astype(BF16))
        new_lat = post(xs_lat, attn=attn_x, tm=TOKEN_TILE, row0=0, tiles_per_seq=tiles_per_seq,
                       mod_row=lambda i: i // tiles_per_seq)
        if need_ctx:
            attn_c = _attention_context(q, k, v, batch, n_lat, n_ctx)
            xs_ctx = post(xs_ctx, attn=attn_c, tm=ctx_len, row0=n_lat, tiles_per_seq=1,
                          mod_row=lambda i: batch)
        xs_lat = new_lat
    return xs_lat.reshape(batch, seq, d)
```

```python
import functools

import jax
import jax.numpy as jnp
from jax import lax
from jax.experimental import pallas as pl
from jax.experimental.pallas import tpu as pltpu

F32 = jnp.float32
BF16 = jnp.bfloat16

GRID_W = 64
EPS = 1e-6
N_MOD = 6
POOL_DIM = 256
POOL_GDIM = 64
POOL_GROUPS = 4
SSD_HEADS = 4
SSD_HEAD_DIM = 64
SSD_INNER = 256
SSD_GROUPS = 2
SSD_STATE = 64
SSD_CONV = 5
SSD_CHUNK = 128
SSD_DIRS = 2
SSD_XBC = 512
ATTN_HEADS = 4
ATTN_KV_HEADS = 2
HEAD_DIM = 128
ATTN_DIM = 512
KV_DIM = 256
ROPE_THETA = 10000.0
Q_SCALE = HEAD_DIM ** -0.5 * 1.4426950408889634

LANES = 128
SUBLANES = 8
TOKEN_TILE = 512
ATTN_Q_TILE = 256
ATTN_K_TILE = 512
ATTN_UNROLL = 6
BF16_SUBLANES = 16
VT_ROWS = HEAD_DIM + BF16_SUBLANES
VMEM_LIMIT = 56 * 1024 * 1024

C_POOL, C_Z, C_XBC, C_DT, C_Q, C_K, C_V, C_END = 0, 256, 512, 1024, 1152, 1664, 1920, 2176


def _silu(x):
    return x * (1.0 / (1.0 + jnp.exp(-x)))


def _softplus(x):
    return jnp.maximum(x, 0.0) + jnp.log1p(jnp.exp(-jnp.abs(x)))


def _const_spec(shape):
    nd = len(shape)
    return pl.BlockSpec(shape, lambda *_: (0,) * nd, pipeline_mode=pl.Buffered(1))


def _mod_kernel(cond_ref, w_ref, b_ref, o_ref):
    s = _silu(cond_ref[...]).astype(BF16)
    o_ref[...] = jnp.dot(s, w_ref[...].astype(BF16), preferred_element_type=F32) + b_ref[...]


def _modulation(cond, w_mod, b_mod):
    depth, d, n = w_mod.shape
    bn = d
    return pl.pallas_call(
        _mod_kernel,
        out_shape=jax.ShapeDtypeStruct((depth, SUBLANES, n), F32),
        grid=(depth, n // bn),
        in_specs=[
            pl.BlockSpec((SUBLANES, d), lambda l, j: (0, 0)),
            pl.BlockSpec((None, d, bn), lambda l, j: (l, 0, j)),
            pl.BlockSpec((None, 1, bn), lambda l, j: (l, 0, j)),
        ],
        out_specs=pl.BlockSpec((None, SUBLANES, bn), lambda l, j: (l, 0, j)),
        compiler_params=pltpu.CompilerParams(
            dimension_semantics=("arbitrary", "arbitrary"), vmem_limit_bytes=VMEM_LIMIT),
        name="modulation",
    )(cond, w_mod, b_mod.reshape(depth, 1, n))


def _rope_partner(x):
    lane = lax.broadcasted_iota(jnp.int32, x.shape, 1)
    fwd = pltpu.roll(x, LANES - 32, 1)
    bwd = pltpu.roll(x, 32, 1)
    return jnp.where((lane % 64) < 32, fwd, bwd)


def _inproj_kernel(xl_ref, xc_ref, mod_ref, n1_ref, w_ref, qw_ref, kw_ref, cos_ref, sin_ref,
                   pool_ref, z_ref, xbc_ref, dt_ref, q_ref, k_ref, vt_ref, *, n_lat_tiles):
    i = pl.program_id(0)
    x = jnp.where(i < n_lat_tiles, xl_ref[...], xc_ref[...])
    ms = jnp.mean(x * x, axis=-1, keepdims=True)
    h = x * lax.rsqrt(ms + EPS) * n1_ref[...]
    h = h * (1.0 + mod_ref[1:2, :]) + mod_ref[0:1, :]
    u = jnp.dot(h.astype(BF16), w_ref[...], preferred_element_type=F32)
    pool_ref[...] = u[:, C_POOL:C_Z]
    z_ref[...] = u[:, C_Z:C_XBC]
    xbc_ref[...] = u[:, C_XBC:C_DT]
    dt_ref[...] = u[:, C_DT:C_Q]
    vt = u[:, C_V:C_END].T.astype(BF16)
    for hd in range(ATTN_KV_HEADS):
        vt_ref[hd * VT_ROWS:hd * VT_ROWS + HEAD_DIM, :] = vt[hd * HEAD_DIM:(hd + 1) * HEAD_DIM]
        vt_ref[hd * VT_ROWS + HEAD_DIM:(hd + 1) * VT_ROWS, :] = jnp.ones(
            (VT_ROWS - HEAD_DIM, vt.shape[1]), BF16)
    cos = cos_ref[...]
    sin = sin_ref[...]

    def norm_rope(t, w):
        t = t * lax.rsqrt(jnp.mean(t * t, axis=-1, keepdims=True) + EPS) * w
        return t * cos + _rope_partner(t) * sin

    for hd in range(ATTN_HEADS):
        t = norm_rope(u[:, C_Q + hd * HEAD_DIM:C_Q + (hd + 1) * HEAD_DIM], qw_ref[...])
        q_ref[:, hd * HEAD_DIM:(hd + 1) * HEAD_DIM] = (t * Q_SCALE).astype(BF16)
    for hd in range(ATTN_KV_HEADS):
        t = norm_rope(u[:, C_K + hd * HEAD_DIM:C_K + (hd + 1) * HEAD_DIM], kw_ref[...])
        k_ref[:, hd * HEAD_DIM:(hd + 1) * HEAD_DIM] = t.astype(BF16)


def _inproj(xs_lat, xs_ctx, mod, n1, w_cat, qw, kw, cos_tab, sin_tab, batch):
    n_lat, d = xs_lat.shape
    n_ctx = xs_ctx.shape[0]
    tm = TOKEN_TILE
    assert n_ctx == tm and n_lat % (batch * tm) == 0
    n_lat_tiles = n_lat // tm
    tiles_per_seq = n_lat_tiles // batch
    n_tok = n_lat + n_ctx
    grid = (n_lat_tiles + 1,)

    def lat_idx(i):
        return jnp.minimum(i, n_lat_tiles - 1)

    def rope_idx(i):
        return jnp.where(i < n_lat_tiles, i % tiles_per_seq, tiles_per_seq)

    def mod_idx(i):
        return jnp.where(i < n_lat_tiles, i // tiles_per_seq, batch)

    row = lambda w: pl.BlockSpec((tm, w), lambda i: (i, 0))
    outs = [(POOL_DIM, F32), (SSD_INNER, F32), (SSD_XBC, F32), (LANES, F32),
            (ATTN_DIM, BF16), (KV_DIM, BF16)]
    return pl.pallas_call(
        functools.partial(_inproj_kernel, n_lat_tiles=n_lat_tiles),
        out_shape=[jax.ShapeDtypeStruct((n_tok, w), t) for w, t in outs]
        + [jax.ShapeDtypeStruct((n_tok // tm, ATTN_KV_HEADS * VT_ROWS, tm), BF16)],
        grid=grid,
        in_specs=[
            pl.BlockSpec((tm, d), lambda i: (lat_idx(i), 0)),
            pl.BlockSpec((tm, d), lambda i: (0, 0)),
            pl.BlockSpec((None, N_MOD, d), lambda i: (mod_idx(i), 0, 0)),
            _const_spec((1, d)),
            _const_spec(w_cat.shape),
            _const_spec((1, HEAD_DIM)),
            _const_spec((1, HEAD_DIM)),
            pl.BlockSpec((tm, HEAD_DIM), lambda i: (rope_idx(i), 0)),
            pl.BlockSpec((tm, HEAD_DIM), lambda i: (rope_idx(i), 0)),
        ],
        out_specs=[row(w) for w, _ in outs]
        + [pl.BlockSpec((None, ATTN_KV_HEADS * VT_ROWS, tm), lambda i: (i, 0, 0))],
        compiler_params=pltpu.CompilerParams(
            dimension_semantics=("arbitrary",), vmem_limit_bytes=VMEM_LIMIT),
        name="inproj",
    )(xs_lat, xs_ctx, mod, n1, w_cat, qw, kw, cos_tab, sin_tab)


def _ssd_kernel(xf_ref, xfp_ref, xfn_ref, dtf_ref, xb_ref, xbp_ref, xbn_ref, dtb_ref,
                cw_ref, cb_ref, bias_ref, alog_ref, dsk_ref,
                yf_ref, yb_ref, buf_ref, h_ref, *, n_ctx_ch, n_ch):
    i = pl.program_id(1)
    q = SSD_CHUNK

    @pl.when(i == 0)
    def _():
        h_ref[...] = jnp.zeros_like(h_ref)

    starts = jnp.logical_or(i == 0, i == n_ctx_ch)
    ends = jnp.logical_or(i == n_ctx_ch - 1, i == n_ch - 1)

    row = lax.broadcasted_iota(jnp.int32, (q, q), 0)
    col = lax.broadcasted_iota(jnp.int32, (q, q), 1)
    tri = (row >= col).astype(F32)
    a_vec = -jnp.exp(alog_ref[...])

    def conv_silu(x_ref, prev_ref, next_ref, zero_prev, zero_next):
        buf_ref[0:SUBLANES, :] = jnp.where(zero_prev, 0.0, prev_ref[...])
        buf_ref[SUBLANES:SUBLANES + q, :] = x_ref[...]
        buf_ref[SUBLANES + q:2 * SUBLANES + q, :] = jnp.where(zero_next, 0.0, next_ref[...])
        acc = cb_ref[...] + cw_ref[0:1, :] * buf_ref[SUBLANES - 2:SUBLANES - 2 + q, :]
        for k in range(1, SSD_CONV):
            acc = acc + cw_ref[k:k + 1, :] * buf_ref[SUBLANES - 2 + k:SUBLANES - 2 + k + q, :]
        return _silu(acc)

    def direction(d, xc, dt_raw, y_ref):
        dt = _softplus(dt_raw + bias_ref[...])
        a = dt * a_vec
        cs = jnp.dot(tri, a, preferred_element_type=F32, precision=lax.Precision.HIGHEST)
        if d == 0:
            e = cs
        else:
            e = cs - a
        e_t = e.T
        tot = cs[q - 1:q, :]
        for g in range(SSD_GROUPS):
            b_g = xc[:, SSD_INNER + g * SSD_STATE:SSD_INNER + (g + 1) * SSD_STATE]
            c_g = xc[:, SSD_INNER + (SSD_GROUPS + g) * SSD_STATE:
                     SSD_INNER + (SSD_GROUPS + g + 1) * SSD_STATE]
            c_bf = c_g.astype(BF16)
            cb = lax.dot_general(c_bf, b_g.astype(BF16), (((1,), (1,)), ((), ())),
                                 preferred_element_type=F32)
            for hh in range(SSD_HEADS // SSD_GROUPS):
                hd = g * (SSD_HEADS // SSD_GROUPS) + hh
                c = d * SSD_HEADS + hd
                e_col = e[:, c:c + 1]
                e_row = e_t[c:c + 1, :]
                tot_c = tot[:, c:c + 1]
                x_h = xc[:, hd * SSD_HEAD_DIM:(hd + 1) * SSD_HEAD_DIM]
                xdt = (x_h * dt[:, c:c + 1]).astype(BF16)
                if d == 0:
                    lmat = jnp.where(row >= col, jnp.exp(e_col - e_row), 0.0)
                    w_in = jnp.exp(e_col)
                    w_out = jnp.exp(tot_c - e_col)
                else:
                    lmat = jnp.where(col >= row, jnp.exp(e_row - e_col), 0.0)
                    w_in = jnp.exp(tot_c - e_col)
                    w_out = jnp.exp(e_col)
                hs = h_ref[c]
                y = jnp.dot((cb * lmat).astype(BF16), xdt, preferred_element_type=F32)
                y = y + jnp.dot(c_bf, hs.astype(BF16), preferred_element_type=F32) * w_in
                y = y + dsk_ref[d:d + 1, hd * SSD_HEAD_DIM:(hd + 1) * SSD_HEAD_DIM] * x_h
                y_ref[:, hd * SSD_HEAD_DIM:(hd + 1) * SSD_HEAD_DIM] = y
                bd = (b_g * w_out).astype(BF16)
                upd = lax.dot_general(bd, xdt, (((0,), (0,)), ((), ())),
                                      preferred_element_type=F32)
                h_ref[c] = jnp.exp(tot_c) * hs + upd

    xc_f = conv_silu(xf_ref, xfp_ref, xfn_ref, starts, ends)
    direction(0, xc_f, dtf_ref[...], yf_ref)
    xc_b = conv_silu(xb_ref, xbp_ref, xbn_ref, ends, starts)
    direction(1, xc_b, dtb_ref[...], yb_ref)


def _ssd(xbc, dt_raw, conv_w, conv_b, dt_bias, a_log, d_skip, batch, n_lat, n_ctx):
    n_tok = xbc.shape[0]
    q = SSD_CHUNK
    lat_ch = n_lat // batch // q
    ctx_ch = n_ctx // batch // q
    n_ch = lat_ch + ctx_ch
    blk8 = q // SUBLANES
    last8 = n_tok // SUBLANES - 1

    def fwd(b, i):
        return jnp.where(i < ctx_ch, batch * lat_ch + b * ctx_ch + i, b * lat_ch + i - ctx_ch)

    def bwd(b, i):
        return jnp.where(i < ctx_ch, batch * lat_ch + b * ctx_ch + ctx_ch - 1 - i,
                         b * lat_ch + n_ch - 1 - i)

    def specs(ch):
        return [
            pl.BlockSpec((q, SSD_XBC), lambda b, i: (ch(b, i), 0)),
            pl.BlockSpec((SUBLANES, SSD_XBC), lambda b, i: (jnp.maximum(ch(b, i) * blk8 - 1, 0), 0)),
            pl.BlockSpec((SUBLANES, SSD_XBC),
                         lambda b, i: (jnp.minimum(ch(b, i) * blk8 + blk8, last8), 0)),
            pl.BlockSpec((q, LANES), lambda b, i: (ch(b, i), 0)),
        ]

    cw = jnp.zeros((SUBLANES, SSD_XBC), F32).at[:SSD_CONV].set(conv_w)
    pad = lambda v: jnp.zeros((1, LANES), F32).at[0, :SSD_DIRS * SSD_HEADS].set(v.reshape(-1))
    dsk = jnp.repeat(d_skip, SSD_HEAD_DIM, axis=1)
    return pl.pallas_call(
        functools.partial(_ssd_kernel, n_ctx_ch=ctx_ch, n_ch=n_ch),
        out_shape=[jax.ShapeDtypeStruct((n_tok, SSD_INNER), F32)] * 2,
        grid=(batch, n_ch),
        in_specs=specs(fwd) + specs(bwd) + [
            _const_spec((SUBLANES, SSD_XBC)), _const_spec((1, SSD_XBC)),
            _const_spec((1, LANES)), _const_spec((1, LANES)), _const_spec((SSD_DIRS, SSD_INNER)),
        ],
        out_specs=[pl.BlockSpec((q, SSD_INNER), lambda b, i: (fwd(b, i), 0)),
                   pl.BlockSpec((q, SSD_INNER), lambda b, i: (bwd(b, i), 0))],
        scratch_shapes=[pltpu.VMEM((q + 2 * SUBLANES, SSD_XBC), F32),
                        pltpu.VMEM((SSD_DIRS * SSD_HEADS, SSD_STATE, SSD_HEAD_DIM), F32)],
        compiler_params=pltpu.CompilerParams(
            dimension_semantics=("arbitrary", "arbitrary"), vmem_limit_bytes=VMEM_LIMIT),
        name="ssd_scan",
    )(xbc, xbc, xbc, dt_raw, xbc, xbc, xbc, dt_raw,
      cw, conv_b.reshape(1, -1), pad(dt_bias), pad(a_log), dsk)


def _scores_t(k, q2):
    return lax.dot_general(k, q2, (((1,), (1,)), ((), ())), preferred_element_type=F32)


def _softmax_stage(s, m_blk, m):
    if m is None:
        return m_blk, None, jnp.exp2((s - m_blk).astype(BF16))
    m_new = jnp.maximum(m, m_blk)
    return m_new, jnp.exp2(m - m_new), jnp.exp2((s - m_new).astype(BF16))


def _attn_kernel(q_ref, kc_ref, vtc_ref, kx_ref, vtx_ref, o_ref, s0_ref, s1_ref, p0_ref, p1_ref,
                 *, n_chunks):
    tq = q_ref.shape[0]
    tk = ATTN_K_TILE
    q2 = jnp.concatenate([q_ref[:, 0:HEAD_DIM], q_ref[:, HEAD_DIM:2 * HEAD_DIM]], axis=0)
    s_refs, p_refs = (s0_ref, s1_ref), (p0_ref, p1_ref)

    def scores_into(slot, c):
        s = _scores_t(kx_ref[pl.ds(pl.multiple_of(c * tk, tk), tk), :], q2)
        s_refs[slot][...] = s
        return jnp.max(s, axis=0, keepdims=True)

    def softmax_into(slot, m_blk, m):
        m, alpha, p = _softmax_stage(s_refs[slot][...], m_blk, m)
        p_refs[slot][...] = p
        return m, alpha

    def step(c, par, carry, with_scores, with_softmax):
        m, acc, alpha, m_blk = carry
        m_blk_next, alpha_next = m_blk, alpha
        if with_scores:
            m_blk_next = scores_into(par, c + 2)
        if with_softmax:
            m, alpha_next = softmax_into(1 - par, m_blk, m)
        acc = alpha * acc + jnp.dot(vtx_ref[c], p_refs[par][...], preferred_element_type=F32)
        return m, acc, alpha_next, m_blk_next

    if n_chunks:
        m_blk0 = scores_into(0, 0)
        m_blk1 = scores_into(1, 1)
    s_ctx = _scores_t(kc_ref[...], q2)
    m, _, p_ctx = _softmax_stage(s_ctx, jnp.max(s_ctx, axis=0, keepdims=True), None)
    acc = jnp.dot(vtc_ref[...], p_ctx, preferred_element_type=F32)
    if n_chunks:
        m, alpha = softmax_into(0, m_blk0, m)
        carry = (m, acc, alpha, m_blk1)
        unroll = ATTN_UNROLL
        assert unroll % 2 == 0 and (n_chunks - 2) % unroll == 0

        def body(i, carry):
            for u in range(unroll):
                carry = step(unroll * i + u, u % 2, carry, True, True)
            return carry

        carry = lax.fori_loop(0, (n_chunks - 2) // unroll, body, carry)
        carry = step(n_chunks - 2, 0, carry, False, True)
        _, acc, _, _ = step(n_chunks - 1, 1, carry, False, False)
    o = (acc[0:HEAD_DIM] / acc[HEAD_DIM:HEAD_DIM + 1]).T.astype(o_ref.dtype)
    o_ref[:, 0:HEAD_DIM] = o[0:tq]
    o_ref[:, HEAD_DIM:2 * HEAD_DIM] = o[tq:2 * tq]


def _attention_latent(q, k, vt, batch, n_lat, n_ctx):
    seq = n_lat // batch
    ctx = n_ctx // batch
    tq, tk = ATTN_Q_TILE, ATTN_K_TILE
    nq = seq // tq
    n_chunks = seq // tk
    assert seq % (2 * tk) == 0 and n_lat % ctx == 0 and n_ctx == tk and vt.shape[2] == tk
    r = 2 * tq
    return pl.pallas_call(
        functools.partial(_attn_kernel, n_chunks=n_chunks),
        out_shape=jax.ShapeDtypeStruct((n_lat, ATTN_DIM), BF16),
        grid=(batch, ATTN_KV_HEADS, nq),
        in_specs=[pl.BlockSpec((tq, 2 * HEAD_DIM), lambda b, g, j: (b * nq + j, g)),
                  pl.BlockSpec((ctx, HEAD_DIM), lambda b, g, j: (n_lat // ctx + b, g)),
                  pl.BlockSpec((None, VT_ROWS, ctx), lambda b, g, j: (n_lat // tk, g, b)),
                  pl.BlockSpec((seq, HEAD_DIM), lambda b, g, j: (b, g)),
                  pl.BlockSpec((n_chunks, VT_ROWS, tk), lambda b, g, j: (b, g, 0))],
        out_specs=pl.BlockSpec((tq, 2 * HEAD_DIM), lambda b, g, j: (b * nq + j, g)),
        scratch_shapes=[pltpu.VMEM((tk, r), F32), pltpu.VMEM((tk, r), F32),
                        pltpu.VMEM((tk, r), BF16), pltpu.VMEM((tk, r), BF16)],
        compiler_params=pltpu.CompilerParams(
            dimension_semantics=("arbitrary",) * 3, vmem_limit_bytes=VMEM_LIMIT),
        name="attn_latent",
    )(q, k, vt, k, vt)


def _attn_ctx_kernel(q_ref, kc_ref, vtc_ref, o_ref):
    _attn_kernel(q_ref, kc_ref, vtc_ref, None, None, o_ref, None, None, None, None, n_chunks=0)


def _attention_context(q, k, vt, batch, n_lat, n_ctx):
    ctx = n_ctx // batch
    tk = ATTN_K_TILE
    return pl.pallas_call(
        _attn_ctx_kernel,
        out_shape=jax.ShapeDtypeStruct((n_ctx, ATTN_DIM), BF16),
        grid=(batch, ATTN_KV_HEADS),
        in_specs=[pl.BlockSpec((ctx, 2 * HEAD_DIM), lambda b, g: (n_lat // ctx + b, g)),
                  pl.BlockSpec((ctx, HEAD_DIM), lambda b, g: (n_lat // ctx + b, g)),
                  pl.BlockSpec((None, VT_ROWS, ctx), lambda b, g: (n_lat // tk, g, b))],
        out_specs=pl.BlockSpec((ctx, 2 * HEAD_DIM), lambda b, g: (b, g)),
        compiler_params=pltpu.CompilerParams(
            dimension_semantics=("arbitrary",) * 2, vmem_limit_bytes=VMEM_LIMIT),
        name="attn_context",
    )(q, k, vt)


def _post_kernel(xs_ref, up_ref, upp_ref, upn_ref, z_ref, yf_ref, yb_ref, at_ref, mod_ref,
                 pw_ref, ps_ref, sw_ref, wo_ref, n2_ref, w1_ref, w3_ref, w2_ref,
                 o_ref, buf_ref, *, tiles_per_seq, seq_len):
    i = pl.program_id(0)
    tm = xs_ref.shape[0]
    halo = SUBLANES
    j = i % tiles_per_seq

    buf_ref[0:halo, :] = jnp.where(j == 0, 0.0, upp_ref[...])
    buf_ref[halo:halo + tm, :] = up_ref[...]
    buf_ref[halo + tm:2 * halo + tm, :] = jnp.where(j == tiles_per_seq - 1, 0.0, upn_ref[...])
    ub = buf_ref[...]
    n = tm + 2 * halo
    s2 = ub + pltpu.roll(ub, 1, 0)
    s4 = pltpu.roll(s2, 1, 0) + pltpu.roll(s2, n - 1, 0)
    s8 = pltpu.roll(s4, 2, 0) + pltpu.roll(s4, n - 2, 0)
    s16 = pltpu.roll(s8, 4, 0) + pltpu.roll(s8, n - 4, 0)
    shape = (tm, POOL_DIM)
    grp = lax.broadcasted_iota(jnp.int32, shape, 1) // POOL_GDIM
    t = lax.broadcasted_iota(jnp.int32, shape, 0) + j * tm
    half = jnp.left_shift(1, grp)
    cnt = (jnp.minimum(t + half, seq_len) - jnp.maximum(t - half, 0)).astype(F32)
    sl = slice(halo, halo + tm)
    wsum = jnp.where(grp == 0, s2[sl], jnp.where(grp == 1, s4[sl],
                                                 jnp.where(grp == 2, s8[sl], s16[sl])))
    pooled = wsum / cnt - ub[sl]
    pool = jnp.dot(pooled.astype(BF16), pw_ref[...], preferred_element_type=F32) * ps_ref[...]

    gy = (yf_ref[...] + yb_ref[...]) * _silu(z_ref[...])
    gw = SSD_INNER // SSD_GROUPS
    parts = []
    for g in range(SSD_GROUPS):
        t_g = gy[:, g * gw:(g + 1) * gw]
        parts.append(t_g * lax.rsqrt(jnp.mean(t_g * t_g, axis=-1, keepdims=True) + EPS))
    ssd = jnp.concatenate(parts, axis=1) * sw_ref[...]

    mix = jnp.concatenate([pool.astype(BF16), ssd.astype(BF16), at_ref[...]], axis=1)
    x1 = xs_ref[...] + mod_ref[2:3, :] * jnp.dot(mix, wo_ref[...], preferred_element_type=F32)

    h = x1 * lax.rsqrt(jnp.mean(x1 * x1, axis=-1, keepdims=True) + EPS) * n2_ref[...]
    h = (h * (1.0 + mod_ref[4:5, :]) + mod_ref[3:4, :]).astype(BF16)
    a = jnp.dot(h, w1_ref[...], preferred_element_type=F32)
    b = jnp.dot(h, w3_ref[...], preferred_element_type=F32)
    ff = jnp.dot((_silu(a) * b).astype(BF16), w2_ref[...], preferred_element_type=F32)
    o_ref[...] = x1 + mod_ref[5:6, :] * ff


def _post(xs, upool, z, yf, yb, attn, mod, pw_bd, pscale, ssd_w, w_out, n2, w1, w3, w2,
          *, tm, row0, tiles_per_seq, mod_row):
    n_rows, d = xs.shape
    n_tiles = n_rows // tm
    off = row0 // tm
    blk8 = tm // SUBLANES
    last8 = upool.shape[0] // SUBLANES - 1
    tok = lambda w: pl.BlockSpec((tm, w), lambda i: (off + i, 0))
    loc = lambda w: pl.BlockSpec((tm, w), lambda i: (i, 0))
    return pl.pallas_call(
        functools.partial(_post_kernel, tiles_per_seq=tiles_per_seq, seq_len=tiles_per_seq * tm),
        out_shape=jax.ShapeDtypeStruct((n_rows, d), F32),
        grid=(n_tiles,),
        in_specs=[
            loc(d),
            tok(POOL_DIM),
            pl.BlockSpec((SUBLANES, POOL_DIM), lambda i: (jnp.maximum((off + i) * blk8 - 1, 0), 0)),
            pl.BlockSpec((SUBLANES, POOL_DIM),
                         lambda i: (jnp.minimum((off + i) * blk8 + blk8, last8), 0)),
            tok(SSD_INNER), tok(SSD_INNER), tok(SSD_INNER),
            loc(ATTN_DIM),
            pl.BlockSpec((None, N_MOD, d), lambda i: (mod_row(i), 0, 0)),
            _const_spec(pw_bd.shape), _const_spec((1, POOL_DIM)), _const_spec((1, SSD_INNER)),
            _const_spec(w_out.shape), _const_spec((1, d)),
            _const_spec(w1.shape), _const_spec(w3.shape), _const_spec(w2.shape),
        ],
        out_specs=loc(d),
        scratch_shapes=[pltpu.VMEM((tm + 2 * SUBLANES, POOL_DIM), F32)],
        compiler_params=pltpu.CompilerParams(
            dimension_semantics=("arbitrary",), vmem_limit_bytes=VMEM_LIMIT),
        name="post_mix_ffn",
    )(xs, upool, upool, upool, z, yf, yb, attn, mod, pw_bd, pscale, ssd_w, w_out, n2, w1, w3, w2)


def _rope_tables(seq, extra_rows):
    rows = seq // GRID_W
    row_ids = jnp.repeat(jnp.arange(rows, dtype=F32), GRID_W)
    col_ids = jnp.tile(jnp.arange(GRID_W, dtype=F32), rows)
    axis_dim = HEAD_DIM // 2
    inv_freq = ROPE_THETA ** (-jnp.arange(0, axis_dim, 2, dtype=F32) / axis_dim)
    ang_r = row_ids[:, None] * inv_freq[None, :]
    ang_c = col_ids[:, None] * inv_freq[None, :]
    cr, sr, cc, sc = jnp.cos(ang_r), jnp.sin(ang_r), jnp.cos(ang_c), jnp.sin(ang_c)
    cos = jnp.concatenate([cr, cr, cc, cc], axis=1)
    sin = jnp.concatenate([-sr, sr, -sc, sc], axis=1)
    cos = jnp.concatenate([cos, jnp.ones((extra_rows, HEAD_DIM), F32)], axis=0)
    sin = jnp.concatenate([sin, jnp.zeros((extra_rows, HEAD_DIM), F32)], axis=0)
    return cos, sin


def _fuse_w_in(w_in):
    d = w_in.shape[0]
    o_dt = POOL_DIM + SSD_INNER + SSD_XBC
    o_att = o_dt + SSD_DIRS * SSD_HEADS
    dt_pad = jnp.zeros((d, C_Q - C_DT - SSD_DIRS * SSD_HEADS), w_in.dtype)
    return jnp.concatenate([w_in[:, :o_dt], w_in[:, o_dt:o_att], dt_pad, w_in[:, o_att:]],
                           axis=1).astype(BF16)


def kernel(x, c, ctx, c_ctx, norm1_w, norm2_w, w_mod, b_mod, w_in, pool_w, pool_scale, conv_w, conv_b,
           dt_bias, a_log, d_skip, ssd_norm_w, q_norm_w, k_norm_w, w_out, w1, w3, w2):
    batch, seq, d = x.shape
    ctx_len = ctx.shape[1]
    depth = w_mod.shape[0]
    n_lat, n_ctx = batch * seq, batch * ctx_len
    assert batch + 1 <= SUBLANES and seq % TOKEN_TILE == 0 and seq % GRID_W == 0

    cond = jnp.zeros((SUBLANES, d), F32).at[:batch].set(c).at[batch].set(c_ctx)
    mod_all = _modulation(cond, w_mod, b_mod).reshape(depth, SUBLANES, N_MOD, d)
    cos_tab, sin_tab = _rope_tables(seq, TOKEN_TILE)

    xs_lat = x.reshape(n_lat, d)
    xs_ctx = ctx.reshape(n_ctx, d)
    tiles_per_seq = seq // TOKEN_TILE
    for layer in range(depth):
        need_ctx = layer < depth - 1
        mod = mod_all[layer]
        w_cat = _fuse_w_in(w_in[layer])
        upool, z, xbc, dt_raw, q, k, vt = _inproj(
            xs_lat, xs_ctx, mod, norm1_w[layer].reshape(1, d), w_cat,
            q_norm_w[layer].reshape(1, -1), k_norm_w[layer].reshape(1, -1), cos_tab, sin_tab, batch)
        yf, yb = _ssd(xbc, dt_raw, conv_w[layer], conv_b[layer], dt_bias[layer], a_log[layer],
                      d_skip[layer], batch, n_lat, n_ctx)
        attn_x = _attention_latent(q, k, vt, batch, n_lat, n_ctx)
        pw_bd = jax.scipy.linalg.block_diag(*[pool_w[layer, g] for g in range(POOL_GROUPS)]).astype(BF16)
        post = functools.partial(
            _post, upool=upool, z=z, yf=yf, yb=yb, mod=mod, pw_bd=pw_bd,
            pscale=pool_scale[layer].reshape(1, -1), ssd_w=ssd_norm_w[layer].reshape(1, -1),
            w_out=w_out[layer].astype(BF16), n2=norm2_w[layer].reshape(1, d),
            w1=w1[layer].astype(BF16), w3=w3[layer].astype(BF16), w2=w2[layer].astype(BF16))
        new_lat = post(xs_lat, attn=attn_x, tm=TOKEN_TILE, row0=0, tiles_per_seq=tiles_per_seq,
                       mod_row=lambda i: i // tiles_per_seq)
        if need_ctx:
            attn_c = _attention_context(q, k, vt, batch, n_lat, n_ctx)
            xs_ctx = post(xs_ctx, attn=attn_c, tm=ctx_len, row0=n_lat, tiles_per_seq=1,
                          mod_row=lambda i: batch)
        xs_lat = new_lat
    return xs_lat.reshape(batch, seq, d)
```

```python
import functools

import jax
import jax.numpy as jnp
from jax import lax
from jax.experimental import pallas as pl
from jax.experimental.pallas import tpu as pltpu

F32 = jnp.float32
BF16 = jnp.bfloat16

GRID_W = 64
EPS = 1e-6
N_MOD = 6
POOL_DIM = 256
POOL_GDIM = 64
POOL_GROUPS = 4
SSD_HEADS = 4
SSD_HEAD_DIM = 64
SSD_INNER = 256
SSD_GROUPS = 2
SSD_STATE = 64
SSD_CONV = 5
SSD_CHUNK = 128
SSD_DIRS = 2
SSD_XBC = 512
ATTN_HEADS = 4
ATTN_KV_HEADS = 2
HEAD_DIM = 128
ATTN_DIM = 512
KV_DIM = 256
ROPE_THETA = 10000.0
Q_SCALE = HEAD_DIM ** -0.5 * 1.4426950408889634
SCORE_BOUND_LIMIT = 60.0
SCORE_BOUND_SLACK = 1.02

LANES = 128
SUBLANES = 8
TOKEN_TILE = 512
ATTN_Q_TILE = 256
ATTN_K_TILE = 512
ATTN_UNROLL = 6
ATTN_BOUNDED_UNROLL = 10
BF16_SUBLANES = 16
VT_ROWS = HEAD_DIM + BF16_SUBLANES
VMEM_LIMIT = 56 * 1024 * 1024

C_POOL, C_Z, C_XBC, C_DT, C_Q, C_K, C_V, C_END = 0, 256, 512, 1024, 1152, 1664, 1920, 2176


def _silu(x):
    return x * (1.0 / (1.0 + jnp.exp(-x)))


def _softplus(x):
    return jnp.maximum(x, 0.0) + jnp.log1p(jnp.exp(-jnp.abs(x)))


def _const_spec(shape):
    nd = len(shape)
    return pl.BlockSpec(shape, lambda *_: (0,) * nd, pipeline_mode=pl.Buffered(1))


def _mod_kernel(cond_ref, w_ref, b_ref, o_ref):
    s = _silu(cond_ref[...]).astype(BF16)
    o_ref[...] = jnp.dot(s, w_ref[...].astype(BF16), preferred_element_type=F32) + b_ref[...]


def _modulation(cond, w_mod, b_mod):
    depth, d, n = w_mod.shape
    bn = d
    return pl.pallas_call(
        _mod_kernel,
        out_shape=jax.ShapeDtypeStruct((depth, SUBLANES, n), F32),
        grid=(depth, n // bn),
        in_specs=[
            pl.BlockSpec((SUBLANES, d), lambda l, j: (0, 0)),
            pl.BlockSpec((None, d, bn), lambda l, j: (l, 0, j)),
            pl.BlockSpec((None, 1, bn), lambda l, j: (l, 0, j)),
        ],
        out_specs=pl.BlockSpec((None, SUBLANES, bn), lambda l, j: (l, 0, j)),
        compiler_params=pltpu.CompilerParams(
            dimension_semantics=("arbitrary", "arbitrary"), vmem_limit_bytes=VMEM_LIMIT),
        name="modulation",
    )(cond, w_mod, b_mod.reshape(depth, 1, n))


def _rope_partner(x):
    lane = lax.broadcasted_iota(jnp.int32, x.shape, 1)
    fwd = pltpu.roll(x, LANES - 32, 1)
    bwd = pltpu.roll(x, 32, 1)
    return jnp.where((lane % 64) < 32, fwd, bwd)


def _inproj_kernel(xl_ref, xc_ref, mod_ref, n1_ref, w_ref, qw_ref, kw_ref, cos_ref, sin_ref,
                   pool_ref, z_ref, xbc_ref, dt_ref, q_ref, k_ref, vt_ref, *, n_lat_tiles):
    i = pl.program_id(0)
    x = jnp.where(i < n_lat_tiles, xl_ref[...], xc_ref[...])
    ms = jnp.mean(x * x, axis=-1, keepdims=True)
    h = x * lax.rsqrt(ms + EPS) * n1_ref[...]
    h = h * (1.0 + mod_ref[1:2, :]) + mod_ref[0:1, :]
    u = jnp.dot(h.astype(BF16), w_ref[...], preferred_element_type=F32)
    pool_ref[...] = u[:, C_POOL:C_Z]
    z_ref[...] = u[:, C_Z:C_XBC]
    xbc_ref[...] = u[:, C_XBC:C_DT]
    dt_ref[...] = u[:, C_DT:C_Q]
    vt = u[:, C_V:C_END].T.astype(BF16)
    for hd in range(ATTN_KV_HEADS):
        vt_ref[hd * VT_ROWS:hd * VT_ROWS + HEAD_DIM, :] = vt[hd * HEAD_DIM:(hd + 1) * HEAD_DIM]
        vt_ref[hd * VT_ROWS + HEAD_DIM:(hd + 1) * VT_ROWS, :] = jnp.ones(
            (VT_ROWS - HEAD_DIM, vt.shape[1]), BF16)
    cos = cos_ref[...]
    sin = sin_ref[...]

    def norm_rope(t, w):
        t = t * lax.rsqrt(jnp.mean(t * t, axis=-1, keepdims=True) + EPS) * w
        return t * cos + _rope_partner(t) * sin

    for hd in range(ATTN_HEADS):
        t = norm_rope(u[:, C_Q + hd * HEAD_DIM:C_Q + (hd + 1) * HEAD_DIM], qw_ref[...])
        q_ref[:, hd * HEAD_DIM:(hd + 1) * HEAD_DIM] = (t * Q_SCALE).astype(BF16)
    for hd in range(ATTN_KV_HEADS):
        t = norm_rope(u[:, C_K + hd * HEAD_DIM:C_K + (hd + 1) * HEAD_DIM], kw_ref[...])
        k_ref[:, hd * HEAD_DIM:(hd + 1) * HEAD_DIM] = t.astype(BF16)


def _inproj(xs_lat, xs_ctx, mod, n1, w_cat, qw, kw, cos_tab, sin_tab, batch):
    n_lat, d = xs_lat.shape
    n_ctx = xs_ctx.shape[0]
    tm = TOKEN_TILE
    assert n_ctx == tm and n_lat % (batch * tm) == 0
    n_lat_tiles = n_lat // tm
    tiles_per_seq = n_lat_tiles // batch
    n_tok = n_lat + n_ctx
    grid = (n_lat_tiles + 1,)

    def lat_idx(i):
        return jnp.minimum(i, n_lat_tiles - 1)

    def rope_idx(i):
        return jnp.where(i < n_lat_tiles, i % tiles_per_seq, tiles_per_seq)

    def mod_idx(i):
        return jnp.where(i < n_lat_tiles, i // tiles_per_seq, batch)

    row = lambda w: pl.BlockSpec((tm, w), lambda i: (i, 0))
    outs = [(POOL_DIM, F32), (SSD_INNER, F32), (SSD_XBC, F32), (LANES, F32),
            (ATTN_DIM, BF16), (KV_DIM, BF16)]
    return pl.pallas_call(
        functools.partial(_inproj_kernel, n_lat_tiles=n_lat_tiles),
        out_shape=[jax.ShapeDtypeStruct((n_tok, w), t) for w, t in outs]
        + [jax.ShapeDtypeStruct((n_tok // tm, ATTN_KV_HEADS * VT_ROWS, tm), BF16)],
        grid=grid,
        in_specs=[
            pl.BlockSpec((tm, d), lambda i: (lat_idx(i), 0)),
            pl.BlockSpec((tm, d), lambda i: (0, 0)),
            pl.BlockSpec((None, N_MOD, d), lambda i: (mod_idx(i), 0, 0)),
            _const_spec((1, d)),
            _const_spec(w_cat.shape),
            _const_spec((1, HEAD_DIM)),
            _const_spec((1, HEAD_DIM)),
            pl.BlockSpec((tm, HEAD_DIM), lambda i: (rope_idx(i), 0)),
            pl.BlockSpec((tm, HEAD_DIM), lambda i: (rope_idx(i), 0)),
        ],
        out_specs=[row(w) for w, _ in outs]
        + [pl.BlockSpec((None, ATTN_KV_HEADS * VT_ROWS, tm), lambda i: (i, 0, 0))],
        compiler_params=pltpu.CompilerParams(
            dimension_semantics=("arbitrary",), vmem_limit_bytes=VMEM_LIMIT),
        name="inproj",
    )(xs_lat, xs_ctx, mod, n1, w_cat, qw, kw, cos_tab, sin_tab)


def _ssd_kernel(xf_ref, xfp_ref, xfn_ref, dtf_ref, xb_ref, xbp_ref, xbn_ref, dtb_ref,
                cw_ref, cb_ref, bias_ref, alog_ref, dsk_ref,
                yf_ref, yb_ref, buf_ref, h_ref, *, n_ctx_ch, n_ch):
    i = pl.program_id(1)
    q = SSD_CHUNK

    @pl.when(i == 0)
    def _():
        h_ref[...] = jnp.zeros_like(h_ref)

    starts = jnp.logical_or(i == 0, i == n_ctx_ch)
    ends = jnp.logical_or(i == n_ctx_ch - 1, i == n_ch - 1)

    row = lax.broadcasted_iota(jnp.int32, (q, q), 0)
    col = lax.broadcasted_iota(jnp.int32, (q, q), 1)
    tri = (row >= col).astype(F32)
    a_vec = -jnp.exp(alog_ref[...])

    def conv_silu(x_ref, prev_ref, next_ref, zero_prev, zero_next):
        buf_ref[0:SUBLANES, :] = jnp.where(zero_prev, 0.0, prev_ref[...])
        buf_ref[SUBLANES:SUBLANES + q, :] = x_ref[...]
        buf_ref[SUBLANES + q:2 * SUBLANES + q, :] = jnp.where(zero_next, 0.0, next_ref[...])
        acc = cb_ref[...] + cw_ref[0:1, :] * buf_ref[SUBLANES - 2:SUBLANES - 2 + q, :]
        for k in range(1, SSD_CONV):
            acc = acc + cw_ref[k:k + 1, :] * buf_ref[SUBLANES - 2 + k:SUBLANES - 2 + k + q, :]
        return _silu(acc)

    def direction(d, xc, dt_raw, y_ref):
        dt = _softplus(dt_raw + bias_ref[...])
        a = dt * a_vec
        cs = jnp.dot(tri, a, preferred_element_type=F32, precision=lax.Precision.HIGHEST)
        if d == 0:
            e = cs
        else:
            e = cs - a
        e_t = e.T
        tot = cs[q - 1:q, :]
        for g in range(SSD_GROUPS):
            b_g = xc[:, SSD_INNER + g * SSD_STATE:SSD_INNER + (g + 1) * SSD_STATE]
            c_g = xc[:, SSD_INNER + (SSD_GROUPS + g) * SSD_STATE:
                     SSD_INNER + (SSD_GROUPS + g + 1) * SSD_STATE]
            c_bf = c_g.astype(BF16)
            cb = lax.dot_general(c_bf, b_g.astype(BF16), (((1,), (1,)), ((), ())),
                                 preferred_element_type=F32)
            for hh in range(SSD_HEADS // SSD_GROUPS):
                hd = g * (SSD_HEADS // SSD_GROUPS) + hh
                c = d * SSD_HEADS + hd
                e_col = e[:, c:c + 1]
                e_row = e_t[c:c + 1, :]
                tot_c = tot[:, c:c + 1]
                x_h = xc[:, hd * SSD_HEAD_DIM:(hd + 1) * SSD_HEAD_DIM]
                xdt = (x_h * dt[:, c:c + 1]).astype(BF16)
                if d == 0:
                    lmat = jnp.where(row >= col, jnp.exp(e_col - e_row), 0.0)
                    w_in = jnp.exp(e_col)
                    w_out = jnp.exp(tot_c - e_col)
                else:
                    lmat = jnp.where(col >= row, jnp.exp(e_row - e_col), 0.0)
                    w_in = jnp.exp(tot_c - e_col)
                    w_out = jnp.exp(e_col)
                hs = h_ref[c]
                y = jnp.dot((cb * lmat).astype(BF16), xdt, preferred_element_type=F32)
                y = y + jnp.dot(c_bf, hs.astype(BF16), preferred_element_type=F32) * w_in
                y = y + dsk_ref[d:d + 1, hd * SSD_HEAD_DIM:(hd + 1) * SSD_HEAD_DIM] * x_h
                y_ref[:, hd * SSD_HEAD_DIM:(hd + 1) * SSD_HEAD_DIM] = y
                bd = (b_g * w_out).astype(BF16)
                upd = lax.dot_general(bd, xdt, (((0,), (0,)), ((), ())),
                                      preferred_element_type=F32)
                h_ref[c] = jnp.exp(tot_c) * hs + upd

    xc_f = conv_silu(xf_ref, xfp_ref, xfn_ref, starts, ends)
    direction(0, xc_f, dtf_ref[...], yf_ref)
    xc_b = conv_silu(xb_ref, xbp_ref, xbn_ref, ends, starts)
    direction(1, xc_b, dtb_ref[...], yb_ref)


def _ssd(xbc, dt_raw, conv_w, conv_b, dt_bias, a_log, d_skip, batch, n_lat, n_ctx):
    n_tok = xbc.shape[0]
    q = SSD_CHUNK
    lat_ch = n_lat // batch // q
    ctx_ch = n_ctx // batch // q
    n_ch = lat_ch + ctx_ch
    blk8 = q // SUBLANES
    last8 = n_tok // SUBLANES - 1

    def fwd(b, i):
        return jnp.where(i < ctx_ch, batch * lat_ch + b * ctx_ch + i, b * lat_ch + i - ctx_ch)

    def bwd(b, i):
        return jnp.where(i < ctx_ch, batch * lat_ch + b * ctx_ch + ctx_ch - 1 - i,
                         b * lat_ch + n_ch - 1 - i)

    def specs(ch):
        return [
            pl.BlockSpec((q, SSD_XBC), lambda b, i: (ch(b, i), 0)),
            pl.BlockSpec((SUBLANES, SSD_XBC), lambda b, i: (jnp.maximum(ch(b, i) * blk8 - 1, 0), 0)),
            pl.BlockSpec((SUBLANES, SSD_XBC),
                         lambda b, i: (jnp.minimum(ch(b, i) * blk8 + blk8, last8), 0)),
            pl.BlockSpec((q, LANES), lambda b, i: (ch(b, i), 0)),
        ]

    cw = jnp.zeros((SUBLANES, SSD_XBC), F32).at[:SSD_CONV].set(conv_w)
    pad = lambda v: jnp.zeros((1, LANES), F32).at[0, :SSD_DIRS * SSD_HEADS].set(v.reshape(-1))
    dsk = jnp.repeat(d_skip, SSD_HEAD_DIM, axis=1)
    return pl.pallas_call(
        functools.partial(_ssd_kernel, n_ctx_ch=ctx_ch, n_ch=n_ch),
        out_shape=[jax.ShapeDtypeStruct((n_tok, SSD_INNER), F32)] * 2,
        grid=(batch, n_ch),
        in_specs=specs(fwd) + specs(bwd) + [
            _const_spec((SUBLANES, SSD_XBC)), _const_spec((1, SSD_XBC)),
            _const_spec((1, LANES)), _const_spec((1, LANES)), _const_spec((SSD_DIRS, SSD_INNER)),
        ],
        out_specs=[pl.BlockSpec((q, SSD_INNER), lambda b, i: (fwd(b, i), 0)),
                   pl.BlockSpec((q, SSD_INNER), lambda b, i: (bwd(b, i), 0))],
        scratch_shapes=[pltpu.VMEM((q + 2 * SUBLANES, SSD_XBC), F32),
                        pltpu.VMEM((SSD_DIRS * SSD_HEADS, SSD_STATE, SSD_HEAD_DIM), F32)],
        compiler_params=pltpu.CompilerParams(
            dimension_semantics=("arbitrary", "arbitrary"), vmem_limit_bytes=VMEM_LIMIT),
        name="ssd_scan",
    )(xbc, xbc, xbc, dt_raw, xbc, xbc, xbc, dt_raw,
      cw, conv_b.reshape(1, -1), pad(dt_bias), pad(a_log), dsk)


def _steps_per_trip(n_steps, preferred):
    return max(u for u in range(2, preferred + 1, 2) if n_steps % u == 0) if n_steps else 2


def _scores_t(k, q2):
    return lax.dot_general(k, q2, (((1,), (1,)), ((), ())), preferred_element_type=F32)


def _softmax_stage(s, m_blk, m):
    if m is None:
        return m_blk, None, jnp.exp2((s - m_blk).astype(BF16))
    m_new = jnp.maximum(m, m_blk)
    return m_new, jnp.exp2(m - m_new), jnp.exp2((s - m_new).astype(BF16))


def _attn_kernel(q_ref, kc_ref, vtc_ref, kx_ref, vtx_ref, o_ref, s0_ref, s1_ref, p0_ref, p1_ref,
                 *, n_chunks):
    tq = q_ref.shape[0]
    tk = ATTN_K_TILE
    q2 = jnp.concatenate([q_ref[:, 0:HEAD_DIM], q_ref[:, HEAD_DIM:2 * HEAD_DIM]], axis=0)
    s_refs, p_refs = (s0_ref, s1_ref), (p0_ref, p1_ref)

    def scores_into(slot, c):
        s = _scores_t(kx_ref[pl.ds(pl.multiple_of(c * tk, tk), tk), :], q2)
        s_refs[slot][...] = s
        return jnp.max(s, axis=0, keepdims=True)

    def softmax_into(slot, m_blk, m):
        m, alpha, p = _softmax_stage(s_refs[slot][...], m_blk, m)
        p_refs[slot][...] = p
        return m, alpha

    def step(c, par, carry, with_scores, with_softmax):
        m, acc, alpha, m_blk = carry
        m_blk_next, alpha_next = m_blk, alpha
        if with_scores:
            m_blk_next = scores_into(par, c + 2)
        if with_softmax:
            m, alpha_next = softmax_into(1 - par, m_blk, m)
        acc = alpha * acc + jnp.dot(vtx_ref[c], p_refs[par][...], preferred_element_type=F32)
        return m, acc, alpha_next, m_blk_next

    if n_chunks:
        m_blk0 = scores_into(0, 0)
        m_blk1 = scores_into(1, 1)
    s_ctx = _scores_t(kc_ref[...], q2)
    m, _, p_ctx = _softmax_stage(s_ctx, jnp.max(s_ctx, axis=0, keepdims=True), None)
    acc = jnp.dot(vtc_ref[...], p_ctx, preferred_element_type=F32)
    if n_chunks:
        m, alpha = softmax_into(0, m_blk0, m)
        carry = (m, acc, alpha, m_blk1)
        unroll = _steps_per_trip(n_chunks - 2, ATTN_UNROLL)

        def body(i, carry):
            for u in range(unroll):
                carry = step(unroll * i + u, u % 2, carry, True, True)
            return carry

        carry = lax.fori_loop(0, (n_chunks - 2) // unroll, body, carry)
        carry = step(n_chunks - 2, 0, carry, False, True)
        _, acc, _, _ = step(n_chunks - 1, 1, carry, False, False)
    o = (acc[0:HEAD_DIM] / acc[HEAD_DIM:HEAD_DIM + 1]).T.astype(o_ref.dtype)
    o_ref[:, 0:HEAD_DIM] = o[0:tq]
    o_ref[:, HEAD_DIM:2 * HEAD_DIM] = o[tq:2 * tq]


def _attn_bounded_kernel(q_ref, kc_ref, vtc_ref, kx_ref, vtx_ref, o_ref, p0_ref, p1_ref, *, n_chunks):
    tq = q_ref.shape[0]
    tk = ATTN_K_TILE
    q2 = jnp.concatenate([q_ref[:, 0:HEAD_DIM], q_ref[:, HEAD_DIM:2 * HEAD_DIM]], axis=0)
    p_refs = (p0_ref, p1_ref)

    def colsum(p):
        return jnp.sum(p.reshape(p.shape[0] // SUBLANES, SUBLANES, p.shape[1]), axis=0)

    def weights_into(slot, c):
        p = jnp.exp2(_scores_t(kx_ref[pl.ds(pl.multiple_of(c * tk, tk), tk), :], q2))
        p_refs[slot][...] = p.astype(BF16)
        return colsum(p)

    def step(c, par, carry, with_scores):
        l8, acc = carry
        if with_scores:
            l8 = l8 + weights_into(1 - par, c + 1)
        acc = acc + jnp.dot(vtx_ref[c, 0:HEAD_DIM, :], p_refs[par][...],
                            preferred_element_type=F32)
        return l8, acc

    p_ctx = jnp.exp2(_scores_t(kc_ref[...], q2))
    l8 = colsum(p_ctx) + weights_into(0, 0)
    acc = jnp.dot(vtc_ref[0:HEAD_DIM, :], p_ctx.astype(BF16), preferred_element_type=F32)
    unroll = _steps_per_trip(n_chunks - 2, ATTN_BOUNDED_UNROLL)

    def body(i, carry):
        for u in range(unroll):
            carry = step(unroll * i + u, u % 2, carry, True)
        return carry

    carry = lax.fori_loop(0, (n_chunks - 2) // unroll, body, (l8, acc))
    carry = step(n_chunks - 2, 0, carry, True)
    l8, acc = step(n_chunks - 1, 1, carry, False)
    o = (acc / jnp.sum(l8, axis=0, keepdims=True)).T.astype(o_ref.dtype)
    o_ref[:, 0:HEAD_DIM] = o[0:tq]
    o_ref[:, HEAD_DIM:2 * HEAD_DIM] = o[tq:2 * tq]


def _attention_latent(q, k, vt, batch, n_lat, n_ctx, bounded):
    seq = n_lat // batch
    ctx = n_ctx // batch
    tq, tk = ATTN_Q_TILE, ATTN_K_TILE
    nq = seq // tq
    n_chunks = seq // tk
    assert seq % (2 * tk) == 0 and n_lat % ctx == 0 and n_ctx == tk and vt.shape[2] == tk
    r = 2 * tq
    body = _attn_bounded_kernel if bounded else _attn_kernel
    p_bufs = [pltpu.VMEM((tk, r), BF16), pltpu.VMEM((tk, r), BF16)]
    s_bufs = [] if bounded else [pltpu.VMEM((tk, r), F32), pltpu.VMEM((tk, r), F32)]
    return pl.pallas_call(
        functools.partial(body, n_chunks=n_chunks),
        out_shape=jax.ShapeDtypeStruct((n_lat, ATTN_DIM), BF16),
        grid=(batch, ATTN_KV_HEADS, nq),
        in_specs=[pl.BlockSpec((tq, 2 * HEAD_DIM), lambda b, g, j: (b * nq + j, g)),
                  pl.BlockSpec((ctx, HEAD_DIM), lambda b, g, j: (n_lat // ctx + b, g)),
                  pl.BlockSpec((None, VT_ROWS, ctx), lambda b, g, j: (n_lat // tk, g, b)),
                  pl.BlockSpec((seq, HEAD_DIM), lambda b, g, j: (b, g)),
                  pl.BlockSpec((n_chunks, VT_ROWS, tk), lambda b, g, j: (b, g, 0))],
        out_specs=pl.BlockSpec((tq, 2 * HEAD_DIM), lambda b, g, j: (b * nq + j, g)),
        scratch_shapes=s_bufs + p_bufs,
        compiler_params=pltpu.CompilerParams(
            dimension_semantics=("arbitrary",) * 3, vmem_limit_bytes=VMEM_LIMIT),
        name="attn_latent_bounded" if bounded else "attn_latent",
    )(q, k, vt, k, vt)


def _attn_ctx_kernel(q_ref, kc_ref, vtc_ref, o_ref):
    _attn_kernel(q_ref, kc_ref, vtc_ref, None, None, o_ref, None, None, None, None, n_chunks=0)


def _attention_context(q, k, vt, batch, n_lat, n_ctx):
    ctx = n_ctx // batch
    tk = ATTN_K_TILE
    return pl.pallas_call(
        _attn_ctx_kernel,
        out_shape=jax.ShapeDtypeStruct((n_ctx, ATTN_DIM), BF16),
        grid=(batch, ATTN_KV_HEADS),
        in_specs=[pl.BlockSpec((ctx, 2 * HEAD_DIM), lambda b, g: (n_lat // ctx + b, g)),
                  pl.BlockSpec((ctx, HEAD_DIM), lambda b, g: (n_lat // ctx + b, g)),
                  pl.BlockSpec((None, VT_ROWS, ctx), lambda b, g: (n_lat // tk, g, b))],
        out_specs=pl.BlockSpec((ctx, 2 * HEAD_DIM), lambda b, g: (b, g)),
        compiler_params=pltpu.CompilerParams(
            dimension_semantics=("arbitrary",) * 2, vmem_limit_bytes=VMEM_LIMIT),
        name="attn_context",
    )(q, k, vt)


def _post_kernel(xs_ref, up_ref, upp_ref, upn_ref, z_ref, yf_ref, yb_ref, at_ref, mod_ref,
                 pw_ref, ps_ref, sw_ref, wo_ref, n2_ref, w1_ref, w3_ref, w2_ref,
                 o_ref, buf_ref, *, tiles_per_seq, seq_len):
    i = pl.program_id(0)
    tm = xs_ref.shape[0]
    halo = SUBLANES
    j = i % tiles_per_seq

    buf_ref[0:halo, :] = jnp.where(j == 0, 0.0, upp_ref[...])
    buf_ref[halo:halo + tm, :] = up_ref[...]
    buf_ref[halo + tm:2 * halo + tm, :] = jnp.where(j == tiles_per_seq - 1, 0.0, upn_ref[...])
    ub = buf_ref[...]
    n = tm + 2 * halo
    s2 = ub + pltpu.roll(ub, 1, 0)
    s4 = pltpu.roll(s2, 1, 0) + pltpu.roll(s2, n - 1, 0)
    s8 = pltpu.roll(s4, 2, 0) + pltpu.roll(s4, n - 2, 0)
    s16 = pltpu.roll(s8, 4, 0) + pltpu.roll(s8, n - 4, 0)
    shape = (tm, POOL_DIM)
    grp = lax.broadcasted_iota(jnp.int32, shape, 1) // POOL_GDIM
    t = lax.broadcasted_iota(jnp.int32, shape, 0) + j * tm
    half = jnp.left_shift(1, grp)
    cnt = (jnp.minimum(t + half, seq_len) - jnp.maximum(t - half, 0)).astype(F32)
    sl = slice(halo, halo + tm)
    wsum = jnp.where(grp == 0, s2[sl], jnp.where(grp == 1, s4[sl],
                                                 jnp.where(grp == 2, s8[sl], s16[sl])))
    pooled = wsum / cnt - ub[sl]
    pool = jnp.dot(pooled.astype(BF16), pw_ref[...], preferred_element_type=F32) * ps_ref[...]

    gy = (yf_ref[...] + yb_ref[...]) * _silu(z_ref[...])
    gw = SSD_INNER // SSD_GROUPS
    parts = []
    for g in range(SSD_GROUPS):
        t_g = gy[:, g * gw:(g + 1) * gw]
        parts.append(t_g * lax.rsqrt(jnp.mean(t_g * t_g, axis=-1, keepdims=True) + EPS))
    ssd = jnp.concatenate(parts, axis=1) * sw_ref[...]

    mix = jnp.concatenate([pool.astype(BF16), ssd.astype(BF16), at_ref[...]], axis=1)
    x1 = xs_ref[...] + mod_ref[2:3, :] * jnp.dot(mix, wo_ref[...], preferred_element_type=F32)

    h = x1 * lax.rsqrt(jnp.mean(x1 * x1, axis=-1, keepdims=True) + EPS) * n2_ref[...]
    h = (h * (1.0 + mod_ref[4:5, :]) + mod_ref[3:4, :]).astype(BF16)
    a = jnp.dot(h, w1_ref[...], preferred_element_type=F32)
    b = jnp.dot(h, w3_ref[...], preferred_element_type=F32)
    ff = jnp.dot((_silu(a) * b).astype(BF16), w2_ref[...], preferred_element_type=F32)
    o_ref[...] = x1 + mod_ref[5:6, :] * ff


def _post(xs, upool, z, yf, yb, attn, mod, pw_bd, pscale, ssd_w, w_out, n2, w1, w3, w2,
          *, tm, row0, tiles_per_seq, mod_row):
    n_rows, d = xs.shape
    n_tiles = n_rows // tm
    off = row0 // tm
    blk8 = tm // SUBLANES
    last8 = upool.shape[0] // SUBLANES - 1
    tok = lambda w: pl.BlockSpec((tm, w), lambda i: (off + i, 0))
    loc = lambda w: pl.BlockSpec((tm, w), lambda i: (i, 0))
    return pl.pallas_call(
        functools.partial(_post_kernel, tiles_per_seq=tiles_per_seq, seq_len=tiles_per_seq * tm),
        out_shape=jax.ShapeDtypeStruct((n_rows, d), F32),
        grid=(n_tiles,),
        in_specs=[
            loc(d),
            tok(POOL_DIM),
            pl.BlockSpec((SUBLANES, POOL_DIM), lambda i: (jnp.maximum((off + i) * blk8 - 1, 0), 0)),
            pl.BlockSpec((SUBLANES, POOL_DIM),
                         lambda i: (jnp.minimum((off + i) * blk8 + blk8, last8), 0)),
            tok(SSD_INNER), tok(SSD_INNER), tok(SSD_INNER),
            loc(ATTN_DIM),
            pl.BlockSpec((None, N_MOD, d), lambda i: (mod_row(i), 0, 0)),
            _const_spec(pw_bd.shape), _const_spec((1, POOL_DIM)), _const_spec((1, SSD_INNER)),
            _const_spec(w_out.shape), _const_spec((1, d)),
            _const_spec(w1.shape), _const_spec(w3.shape), _const_spec(w2.shape),
        ],
        out_specs=loc(d),
        scratch_shapes=[pltpu.VMEM((tm + 2 * SUBLANES, POOL_DIM), F32)],
        compiler_params=pltpu.CompilerParams(
            dimension_semantics=("arbitrary",), vmem_limit_bytes=VMEM_LIMIT),
        name="post_mix_ffn",
    )(xs, upool, upool, upool, z, yf, yb, attn, mod, pw_bd, pscale, ssd_w, w_out, n2, w1, w3, w2)


def _rope_tables(seq, extra_rows):
    rows = seq // GRID_W
    row_ids = jnp.repeat(jnp.arange(rows, dtype=F32), GRID_W)
    col_ids = jnp.tile(jnp.arange(GRID_W, dtype=F32), rows)
    axis_dim = HEAD_DIM // 2
    inv_freq = ROPE_THETA ** (-jnp.arange(0, axis_dim, 2, dtype=F32) / axis_dim)
    ang_r = row_ids[:, None] * inv_freq[None, :]
    ang_c = col_ids[:, None] * inv_freq[None, :]
    cr, sr, cc, sc = jnp.cos(ang_r), jnp.sin(ang_r), jnp.cos(ang_c), jnp.sin(ang_c)
    cos = jnp.concatenate([cr, cr, cc, cc], axis=1)
    sin = jnp.concatenate([-sr, sr, -sc, sc], axis=1)
    cos = jnp.concatenate([cos, jnp.ones((extra_rows, HEAD_DIM), F32)], axis=0)
    sin = jnp.concatenate([sin, jnp.zeros((extra_rows, HEAD_DIM), F32)], axis=0)
    return cos, sin


def _fuse_w_in(w_in):
    d = w_in.shape[0]
    o_dt = POOL_DIM + SSD_INNER + SSD_XBC
    o_att = o_dt + SSD_DIRS * SSD_HEADS
    dt_pad = jnp.zeros((d, C_Q - C_DT - SSD_DIRS * SSD_HEADS), w_in.dtype)
    return jnp.concatenate([w_in[:, :o_dt], w_in[:, o_dt:o_att], dt_pad, w_in[:, o_att:]],
                           axis=1).astype(BF16)


def kernel(x, c, ctx, c_ctx, norm1_w, norm2_w, w_mod, b_mod, w_in, pool_w, pool_scale, conv_w, conv_b,
           dt_bias, a_log, d_skip, ssd_norm_w, q_norm_w, k_norm_w, w_out, w1, w3, w2):
    batch, seq, d = x.shape
    ctx_len = ctx.shape[1]
    depth = w_mod.shape[0]
    n_lat, n_ctx = batch * seq, batch * ctx_len
    assert batch + 1 <= SUBLANES and seq % TOKEN_TILE == 0 and seq % GRID_W == 0

    cond = jnp.zeros((SUBLANES, d), F32).at[:batch].set(c).at[batch].set(c_ctx)
    mod_all = _modulation(cond, w_mod, b_mod).reshape(depth, SUBLANES, N_MOD, d)
    cos_tab, sin_tab = _rope_tables(seq, TOKEN_TILE)

    xs_lat = x.reshape(n_lat, d)
    xs_ctx = ctx.reshape(n_ctx, d)
    tiles_per_seq = seq // TOKEN_TILE
    for layer in range(depth):
        need_ctx = layer < depth - 1
        mod = mod_all[layer]
        w_cat = _fuse_w_in(w_in[layer])
        upool, z, xbc, dt_raw, q, k, vt = _inproj(
            xs_lat, xs_ctx, mod, norm1_w[layer].reshape(1, d), w_cat,
            q_norm_w[layer].reshape(1, -1), k_norm_w[layer].reshape(1, -1), cos_tab, sin_tab, batch)
        yf, yb = _ssd(xbc, dt_raw, conv_w[layer], conv_b[layer], dt_bias[layer], a_log[layer],
                      d_skip[layer], batch, n_lat, n_ctx)
        score_bound = (HEAD_DIM * Q_SCALE * SCORE_BOUND_SLACK * jnp.max(jnp.abs(q_norm_w[layer]))
                       * jnp.max(jnp.abs(k_norm_w[layer])))
        attn_x = lax.cond(
            score_bound < SCORE_BOUND_LIMIT,
            functools.partial(_attention_latent, batch=batch, n_lat=n_lat, n_ctx=n_ctx, bounded=True),
            functools.partial(_attention_latent, batch=batch, n_lat=n_lat, n_ctx=n_ctx, bounded=False),
            q, k, vt)
        pw_bd = jax.scipy.linalg.block_diag(*[pool_w[layer, g] for g in range(POOL_GROUPS)]).astype(BF16)
        post = functools.partial(
            _post, upool=upool, z=z, yf=yf, yb=yb, mod=mod, pw_bd=pw_bd,
            pscale=pool_scale[layer].reshape(1, -1), ssd_w=ssd_norm_w[layer].reshape(1, -1),
            w_out=w_out[layer].astype(BF16), n2=norm2_w[layer].reshape(1, d),
            w1=w1[layer].astype(BF16), w3=w3[layer].astype(BF16), w2=w2[layer].astype(BF16))
        new_lat = post(xs_lat, attn=attn_x, tm=TOKEN_TILE, row0=0, tiles_per_seq=tiles_per_seq,
                       mod_row=lambda i: i // tiles_per_seq)
        if need_ctx:
            attn_c = _attention_context(q, k, vt, batch, n_lat, n_ctx)
            xs_ctx = post(xs_ctx, attn=attn_c, tm=ctx_len, row0=n_lat, tiles_per_seq=1,
                          mod_row=lambda i: batch)
        xs_lat = new_lat
    return xs_lat.reshape(batch, seq, d)
```

```python
import functools

import jax
import jax.numpy as jnp
from jax import lax
from jax.experimental import pallas as pl
from jax.experimental.pallas import tpu as pltpu

F32 = jnp.float32
BF16 = jnp.bfloat16

GRID_W = 64
EPS = 1e-6
N_MOD = 6
POOL_DIM = 256
POOL_GDIM = 64
POOL_GROUPS = 4
SSD_HEADS = 4
SSD_HEAD_DIM = 64
SSD_INNER = 256
SSD_GROUPS = 2
SSD_STATE = 64
SSD_CONV = 5
SSD_CHUNK = 128
SSD_DIRS = 2
SSD_XBC = 512
ATTN_HEADS = 4
ATTN_KV_HEADS = 2
HEAD_DIM = 128
ATTN_DIM = 512
KV_DIM = 256
ROPE_THETA = 10000.0
Q_SCALE = HEAD_DIM ** -0.5 * 1.4426950408889634
SCORE_BOUND_LIMIT = 60.0
SCORE_BOUND_SLACK = 1.02

LANES = 128
SUBLANES = 8
TOKEN_TILE = 512
INPROJ_ROWS = 256
ATTN_Q_TILE = 512
ATTN_K_TILE = 512
ATTN_UNROLL = 6
ATTN_BOUNDED_UNROLL = 10
BF16_SUBLANES = 16
VT_ROWS = HEAD_DIM + BF16_SUBLANES
VMEM_LIMIT = 56 * 1024 * 1024

C_POOL, C_Z, C_XBC, C_DT, C_Q, C_K, C_V, C_END = 0, 256, 512, 1024, 1152, 1664, 1920, 2176


def _silu(x):
    return x * (1.0 / (1.0 + jnp.exp(-x)))


def _softplus(x):
    return jnp.maximum(x, 0.0) + jnp.log1p(jnp.exp(-jnp.abs(x)))


def _const_spec(shape):
    nd = len(shape)
    return pl.BlockSpec(shape, lambda *_: (0,) * nd, pipeline_mode=pl.Buffered(1))


def _mod_kernel(cond_ref, w_ref, b_ref, o_ref):
    s = _silu(cond_ref[...]).astype(BF16)
    o_ref[...] = jnp.dot(s, w_ref[...].astype(BF16), preferred_element_type=F32) + b_ref[...]


def _modulation(cond, w_mod, b_mod):
    depth, d, n = w_mod.shape
    bn = d
    return pl.pallas_call(
        _mod_kernel,
        out_shape=jax.ShapeDtypeStruct((depth, SUBLANES, n), F32),
        grid=(depth, n // bn),
        in_specs=[
            pl.BlockSpec((SUBLANES, d), lambda l, j: (0, 0)),
            pl.BlockSpec((None, d, bn), lambda l, j: (l, 0, j)),
            pl.BlockSpec((None, 1, bn), lambda l, j: (l, 0, j)),
        ],
        out_specs=pl.BlockSpec((None, SUBLANES, bn), lambda l, j: (l, 0, j)),
        compiler_params=pltpu.CompilerParams(
            dimension_semantics=("arbitrary", "arbitrary"), vmem_limit_bytes=VMEM_LIMIT),
        name="modulation",
    )(cond, w_mod, b_mod.reshape(depth, 1, n))


def _rope_partner(x):
    lane = lax.broadcasted_iota(jnp.int32, x.shape, 1)
    fwd = pltpu.roll(x, LANES - 32, 1)
    bwd = pltpu.roll(x, 32, 1)
    return jnp.where((lane % 64) < 32, fwd, bwd)


def _inproj_kernel(xl_ref, xc_ref, mod_ref, n1_ref, w_ref, qw_ref, kw_ref, cos_ref, sin_ref,
                   pool_ref, z_ref, xbc_ref, dt_ref, q_ref, k_ref, vt_ref, *, n_lat_tiles):
    i = pl.program_id(0)
    is_lat = i < n_lat_tiles
    tm = xl_ref.shape[0]
    for r0 in range(0, tm, INPROJ_ROWS):
        rows = slice(r0, r0 + INPROJ_ROWS)
        x = jnp.where(is_lat, xl_ref[rows, :], xc_ref[rows, :])
        ms = jnp.mean(x * x, axis=-1, keepdims=True)
        h = x * lax.rsqrt(ms + EPS) * n1_ref[...]
        h = h * (1.0 + mod_ref[1:2, :]) + mod_ref[0:1, :]
        u = jnp.dot(h.astype(BF16), w_ref[...], preferred_element_type=F32)
        pool_ref[rows, :] = u[:, C_POOL:C_Z]
        z_ref[rows, :] = u[:, C_Z:C_XBC]
        xbc_ref[rows, :] = u[:, C_XBC:C_DT]
        dt_ref[rows, :] = u[:, C_DT:C_Q]
        vt = u[:, C_V:C_END].T.astype(BF16)
        for hd in range(ATTN_KV_HEADS):
            vt_ref[hd * VT_ROWS:hd * VT_ROWS + HEAD_DIM, rows] = vt[hd * HEAD_DIM:(hd + 1) * HEAD_DIM]
            vt_ref[hd * VT_ROWS + HEAD_DIM:(hd + 1) * VT_ROWS, rows] = jnp.ones(
                (VT_ROWS - HEAD_DIM, INPROJ_ROWS), BF16)
        cos = cos_ref[rows, :]
        sin = sin_ref[rows, :]

        def norm_rope(t, w):
            t = t * lax.rsqrt(jnp.mean(t * t, axis=-1, keepdims=True) + EPS) * w
            return t * cos + _rope_partner(t) * sin

        for hd in range(ATTN_HEADS):
            t = norm_rope(u[:, C_Q + hd * HEAD_DIM:C_Q + (hd + 1) * HEAD_DIM], qw_ref[...])
            q_ref[rows, hd * HEAD_DIM:(hd + 1) * HEAD_DIM] = (t * Q_SCALE).astype(BF16)
        for hd in range(ATTN_KV_HEADS):
            t = norm_rope(u[:, C_K + hd * HEAD_DIM:C_K + (hd + 1) * HEAD_DIM], kw_ref[...])
            k_ref[rows, hd * HEAD_DIM:(hd + 1) * HEAD_DIM] = t.astype(BF16)


def _inproj(xs_lat, xs_ctx, mod, n1, w_cat, qw, kw, cos_tab, sin_tab, batch):
    n_lat, d = xs_lat.shape
    n_ctx = xs_ctx.shape[0]
    tm = TOKEN_TILE
    assert n_ctx == tm and n_lat % (batch * tm) == 0
    n_lat_tiles = n_lat // tm
    tiles_per_seq = n_lat_tiles // batch
    n_tok = n_lat + n_ctx
    grid = (n_lat_tiles + 1,)

    def lat_idx(i):
        return jnp.minimum(i, n_lat_tiles - 1)

    def rope_idx(i):
        return jnp.where(i < n_lat_tiles, i % tiles_per_seq, tiles_per_seq)

    def mod_idx(i):
        return jnp.where(i < n_lat_tiles, i // tiles_per_seq, batch)

    row = lambda w: pl.BlockSpec((tm, w), lambda i: (i, 0))
    outs = [(POOL_DIM, F32), (SSD_INNER, F32), (SSD_XBC, F32), (LANES, F32),
            (ATTN_DIM, BF16), (KV_DIM, BF16)]
    return pl.pallas_call(
        functools.partial(_inproj_kernel, n_lat_tiles=n_lat_tiles),
        out_shape=[jax.ShapeDtypeStruct((n_tok, w), t) for w, t in outs]
        + [jax.ShapeDtypeStruct((n_tok // tm, ATTN_KV_HEADS * VT_ROWS, tm), BF16)],
        grid=grid,
        in_specs=[
            pl.BlockSpec((tm, d), lambda i: (lat_idx(i), 0)),
            pl.BlockSpec((tm, d), lambda i: (0, 0)),
            pl.BlockSpec((None, N_MOD, d), lambda i: (mod_idx(i), 0, 0)),
            _const_spec((1, d)),
            _const_spec(w_cat.shape),
            _const_spec((1, HEAD_DIM)),
            _const_spec((1, HEAD_DIM)),
            pl.BlockSpec((tm, HEAD_DIM), lambda i: (rope_idx(i), 0)),
            pl.BlockSpec((tm, HEAD_DIM), lambda i: (rope_idx(i), 0)),
        ],
        out_specs=[row(w) for w, _ in outs]
        + [pl.BlockSpec((None, ATTN_KV_HEADS * VT_ROWS, tm), lambda i: (i, 0, 0))],
        compiler_params=pltpu.CompilerParams(
            dimension_semantics=("arbitrary",), vmem_limit_bytes=VMEM_LIMIT),
        name="inproj",
    )(xs_lat, xs_ctx, mod, n1, w_cat, qw, kw, cos_tab, sin_tab)


def _ssd_kernel(xf_ref, xfp_ref, xfn_ref, dtf_ref, dtf_next_ref,
                xb_ref, xbp_ref, xbn_ref, dtb_ref, dtb_next_ref,
                cw_ref, cb_ref, bias_ref, alog_ref, dsk_ref, spread_ref, ecol_ref,
                yf_ref, yb_ref, buff_ref, bufb_ref, h_ref, et_sc, spread_sc, ecols_sc,
                *, n_ctx_ch, n_ch):
    i = pl.program_id(1)
    q = SSD_CHUNK

    @pl.when(i == 0)
    def _():
        h_ref[...] = jnp.zeros_like(h_ref)

    starts = jnp.logical_or(i == 0, i == n_ctx_ch)
    ends = jnp.logical_or(i == n_ctx_ch - 1, i == n_ch - 1)

    row = lax.broadcasted_iota(jnp.int32, (q, q), 0)
    col = lax.broadcasted_iota(jnp.int32, (q, q), 1)
    n_col = SSD_DIRS * SSD_HEADS
    heads_per_group = SSD_HEADS // SSD_GROUPS
    pair = heads_per_group * SSD_HEAD_DIM
    lane_half = col // SSD_HEAD_DIM
    st_row = lax.broadcasted_iota(jnp.int32, (SSD_GROUPS * SSD_STATE, SSD_INNER), 0)
    st_col = lax.broadcasted_iota(jnp.int32, (SSD_GROUPS * SSD_STATE, SSD_INNER), 1)
    same_group = (st_row // SSD_STATE) == (st_col // pair)
    contract0 = (((0,), (0,)), ((), ()))

    def conv_silu(buf_ref, x_ref, prev_ref, next_ref, zero_prev, zero_next):
        buf_ref[0:SUBLANES, :] = jnp.where(zero_prev, 0.0, prev_ref[...])
        buf_ref[SUBLANES:SUBLANES + q, :] = x_ref[...]
        buf_ref[SUBLANES + q:2 * SUBLANES + q, :] = jnp.where(zero_next, 0.0, next_ref[...])
        u = buf_ref[...]
        n = u.shape[0]
        pad = SSD_CONV // 2
        acc = cb_ref[...] + cw_ref[pad:pad + 1, :] * u[SUBLANES:SUBLANES + q]
        for k in range(SSD_CONV):
            if k != pad:
                shifted = pltpu.roll(u, (pad - k) % n, 0)
                acc = acc + cw_ref[k:k + 1, :] * shifted[SUBLANES:SUBLANES + q]
        return _silu(acc)

    def bf16_terms(v, n_terms):
        terms, rest = [], v
        for _ in range(n_terms):
            t = rest.astype(BF16).astype(F32)
            terms.append(t)
            rest = rest - t
        rows = -(-n_terms * n_col // BF16_SUBLANES) * BF16_SUBLANES
        if rows > n_terms * n_col:
            terms.append(jnp.zeros((rows - n_terms * n_col, v.shape[1]), F32))
        return jnp.concatenate(terms, axis=0).astype(BF16)

    fwd_rows = lax.broadcasted_iota(jnp.int32, (n_col, q), 0) < SSD_HEADS

    def token_decay_sums(dtf_blk_ref, dtb_blk_ref):
        dt_raw = jnp.where(fwd_rows, dtf_blk_ref[...].T[0:n_col], dtb_blk_ref[...].T[0:n_col])
        dt = _softplus(dt_raw + bias_ref[...])
        a = dt * -jnp.exp(alog_ref[...])
        tri_u = (row <= col).astype(F32)
        cs = jnp.dot(a, tri_u, preferred_element_type=F32, precision=lax.Precision.HIGHEST)
        return dt, a, cs

    def token_scalars(dt, a, cs):
        tot = cs[:, q - 1:q]
        e_t = jnp.where(fwd_rows, cs, cs - a)
        grow, shrink = jnp.exp(e_t), jnp.exp(tot - e_t)
        w_in = jnp.where(fwd_rows, grow, shrink)
        w_out = jnp.where(fwd_rows, shrink, grow)
        terms = jnp.concatenate(
            [bf16_terms(dt, 2), bf16_terms(w_in, 2), bf16_terms(dt * w_out, 2)], axis=1)
        spread = lax.dot_general(terms, spread_ref[...], contract0,
                                 preferred_element_type=F32)
        e_cols = lax.dot_general(bf16_terms(e_t, 3), ecol_ref[...], contract0,
                                 preferred_element_type=F32)
        return e_t, spread, e_cols

    def into_scratch(vals):
        et_sc[...], spread_sc[...], ecols_sc[...] = vals

    @pl.when(i == 0)
    def _():
        into_scratch(token_scalars(*token_decay_sums(dtf_ref, dtb_ref)))

    next_sums = token_decay_sums(dtf_next_ref, dtb_next_ref)
    xc_f = conv_silu(buff_ref, xf_ref, xfp_ref, xfn_ref, starts, ends)
    xc_b = conv_silu(bufb_ref, xb_ref, xbp_ref, xbn_ref, ends, starts)

    def direction(d, xc, y_ref):
        ch = slice(d * SSD_INNER, (d + 1) * SSD_INNER)
        dt_x, w_in_x, dtw_x = spread_sc[0:q, ch], spread_sc[q:2 * q, ch], spread_sc[2 * q:3 * q, ch]
        x = xc[:, 0:SSD_INNER]
        xdt = (x * dt_x).astype(BF16)
        xdtw = (x * dtw_x).astype(BF16)
        b_all = xc[:, SSD_INNER:SSD_INNER + SSD_GROUPS * SSD_STATE].astype(BF16)
        c_all = xc[:, SSD_INNER + SSD_GROUPS * SSD_STATE:].astype(BF16)
        hs = h_ref[d]
        y_off = jnp.dot(c_all, hs.astype(BF16), preferred_element_type=F32) * w_in_x
        zero = jnp.zeros((), BF16)
        for g in range(SSD_GROUPS):
            c_g = jnp.where(lane_half == g, c_all, zero)
            cb = lax.dot_general(c_g, b_all, (((1,), (1,)), ((), ())),
                                 preferred_element_type=F32)
            sl = slice(g * pair, (g + 1) * pair)
            y_g = y_off[:, sl] + dsk_ref[d:d + 1, sl] * x[:, sl]
            for hh in range(heads_per_group):
                c = d * SSD_HEADS + g * heads_per_group + hh
                e_col = ecols_sc[:, c * q:(c + 1) * q]
                e_row = et_sc[c:c + 1, :]
                if d == 0:
                    lmat = jnp.where(row >= col, jnp.exp(e_col - e_row), 0.0)
                else:
                    lmat = jnp.where(col >= row, jnp.exp(e_row - e_col), 0.0)
                x_h = jnp.where(lane_half == hh, xdt[:, sl], zero)
                y_g = y_g + jnp.dot((cb * lmat).astype(BF16), x_h, preferred_element_type=F32)
            y_ref[:, sl] = y_g
        upd = lax.dot_general(b_all, xdtw, contract0, preferred_element_type=F32)
        exp_tot = w_in_x[q - 1:q, :] if d == 0 else w_in_x[0:1, :]
        h_ref[d] = exp_tot * hs + jnp.where(same_group, upd, 0.0)

    direction(0, xc_f, yf_ref)
    next_scalars = token_scalars(*next_sums)
    direction(1, xc_b, yb_ref)
    into_scratch(next_scalars)


def _ssd(xbc, dt_raw, conv_w, conv_b, dt_bias, a_log, d_skip, batch, n_lat, n_ctx):
    n_tok = xbc.shape[0]
    q = SSD_CHUNK
    lat_ch = n_lat // batch // q
    ctx_ch = n_ctx // batch // q
    n_ch = lat_ch + ctx_ch
    blk8 = q // SUBLANES
    last8 = n_tok // SUBLANES - 1

    def fwd(b, i):
        return jnp.where(i < ctx_ch, batch * lat_ch + b * ctx_ch + i, b * lat_ch + i - ctx_ch)

    def bwd(b, i):
        return jnp.where(i < ctx_ch, batch * lat_ch + b * ctx_ch + ctx_ch - 1 - i,
                         b * lat_ch + n_ch - 1 - i)

    def specs(ch):
        return [
            pl.BlockSpec((q, SSD_XBC), lambda b, i: (ch(b, i), 0)),
            pl.BlockSpec((SUBLANES, SSD_XBC), lambda b, i: (jnp.maximum(ch(b, i) * blk8 - 1, 0), 0)),
            pl.BlockSpec((SUBLANES, SSD_XBC),
                         lambda b, i: (jnp.minimum(ch(b, i) * blk8 + blk8, last8), 0)),
            pl.BlockSpec((q, LANES), lambda b, i: (ch(b, i), 0)),
            pl.BlockSpec((q, LANES), lambda b, i: (ch(b, jnp.minimum(i + 1, n_ch - 1)), 0)),
        ]

    cw = jnp.zeros((SUBLANES, SSD_XBC), F32).at[:SSD_CONV].set(conv_w)
    n_col = SSD_DIRS * SSD_HEADS
    rows = lambda v: jnp.broadcast_to(v.reshape(n_col, 1), (n_col, q))
    dsk = jnp.repeat(d_skip, SSD_HEAD_DIM, axis=1)
    term_row = jnp.arange(2 * BF16_SUBLANES) % n_col
    live3 = jnp.arange(2 * BF16_SUBLANES) < 3 * n_col
    col_of_channel = jnp.arange(SSD_DIRS * SSD_INNER) // SSD_HEAD_DIM
    col_of_block = jnp.arange(n_col * q) // q
    spread = (term_row[:BF16_SUBLANES, None] == col_of_channel[None, :]).astype(BF16)
    ecol = ((term_row[:, None] == col_of_block[None, :]) & live3[:, None]).astype(BF16)
    return pl.pallas_call(
        functools.partial(_ssd_kernel, n_ctx_ch=ctx_ch, n_ch=n_ch),
        out_shape=[jax.ShapeDtypeStruct((n_tok, SSD_INNER), F32)] * 2,
        grid=(batch, n_ch),
        in_specs=specs(fwd) + specs(bwd) + [
            _const_spec((SUBLANES, SSD_XBC)), _const_spec((1, SSD_XBC)),
            _const_spec((n_col, q)), _const_spec((n_col, q)), _const_spec((SSD_DIRS, SSD_INNER)),
            _const_spec(spread.shape), _const_spec(ecol.shape),
        ],
        out_specs=[pl.BlockSpec((q, SSD_INNER), lambda b, i: (fwd(b, i), 0)),
                   pl.BlockSpec((q, SSD_INNER), lambda b, i: (bwd(b, i), 0))],
        scratch_shapes=[pltpu.VMEM((q + 2 * SUBLANES, SSD_XBC), F32),
                        pltpu.VMEM((q + 2 * SUBLANES, SSD_XBC), F32),
                        pltpu.VMEM((SSD_DIRS, SSD_GROUPS * SSD_STATE, SSD_INNER), F32),
                        pltpu.VMEM((n_col, q), F32),
                        pltpu.VMEM((3 * q, SSD_DIRS * SSD_INNER), F32),
                        pltpu.VMEM((q, n_col * q), F32)],
        compiler_params=pltpu.CompilerParams(
            dimension_semantics=("arbitrary", "arbitrary"), vmem_limit_bytes=VMEM_LIMIT),
        name="ssd_scan",
    )(xbc, xbc, xbc, dt_raw, dt_raw, xbc, xbc, xbc, dt_raw, dt_raw,
      cw, conv_b.reshape(1, -1), rows(dt_bias), rows(a_log), dsk, spread, ecol)


def _steps_per_trip(n_steps, preferred):
    return max(u for u in range(2, preferred + 1, 2) if n_steps % u == 0) if n_steps else 2


def _scores_t(k, q2):
    return lax.dot_general(k, q2, (((1,), (1,)), ((), ())), preferred_element_type=F32)


def _softmax_stage(s, m_blk, m):
    if m is None:
        return m_blk, None, jnp.exp2((s - m_blk).astype(BF16))
    m_new = jnp.maximum(m, m_blk)
    return m_new, jnp.exp2(m - m_new), jnp.exp2((s - m_new).astype(BF16))


def _attn_kernel(q_ref, kc_ref, vtc_ref, kx_ref, vtx_ref, o_ref, s0_ref, s1_ref, p0_ref, p1_ref,
                 *, n_chunks):
    tq = q_ref.shape[0]
    tk = ATTN_K_TILE
    q2 = jnp.concatenate([q_ref[:, 0:HEAD_DIM], q_ref[:, HEAD_DIM:2 * HEAD_DIM]], axis=0)
    s_refs, p_refs = (s0_ref, s1_ref), (p0_ref, p1_ref)

    def scores_into(slot, c):
        s = _scores_t(kx_ref[pl.ds(pl.multiple_of(c * tk, tk), tk), :], q2)
        s_refs[slot][...] = s
        return jnp.max(s, axis=0, keepdims=True)

    def softmax_into(slot, m_blk, m):
        m, alpha, p = _softmax_stage(s_refs[slot][...], m_blk, m)
        p_refs[slot][...] = p
        return m, alpha

    def step(c, par, carry, with_scores, with_softmax):
        m, acc, alpha, m_blk = carry
        m_blk_next, alpha_next = m_blk, alpha
        if with_scores:
            m_blk_next = scores_into(par, c + 2)
        if with_softmax:
            m, alpha_next = softmax_into(1 - par, m_blk, m)
        acc = alpha * acc + jnp.dot(vtx_ref[c], p_refs[par][...], preferred_element_type=F32)
        return m, acc, alpha_next, m_blk_next

    if n_chunks:
        m_blk0 = scores_into(0, 0)
        m_blk1 = scores_into(1, 1)
    s_ctx = _scores_t(kc_ref[...], q2)
    m, _, p_ctx = _softmax_stage(s_ctx, jnp.max(s_ctx, axis=0, keepdims=True), None)
    acc = jnp.dot(vtc_ref[...], p_ctx, preferred_element_type=F32)
    if n_chunks:
        m, alpha = softmax_into(0, m_blk0, m)
        carry = (m, acc, alpha, m_blk1)
        unroll = _steps_per_trip(n_chunks - 2, ATTN_UNROLL)

        def body(i, carry):
            for u in range(unroll):
                carry = step(unroll * i + u, u % 2, carry, True, True)
            return carry

        carry = lax.fori_loop(0, (n_chunks - 2) // unroll, body, carry)
        carry = step(n_chunks - 2, 0, carry, False, True)
        _, acc, _, _ = step(n_chunks - 1, 1, carry, False, False)
    o = (acc[0:HEAD_DIM] / acc[HEAD_DIM:HEAD_DIM + 1]).T.astype(o_ref.dtype)
    o_ref[:, 0:HEAD_DIM] = o[0:tq]
    o_ref[:, HEAD_DIM:2 * HEAD_DIM] = o[tq:2 * tq]


def _attn_bounded_kernel(q_ref, kc_ref, vtc_ref, kx_ref, vtx_ref, o_ref, p0_ref, p1_ref, *, n_chunks):
    tq = q_ref.shape[0]
    tk = ATTN_K_TILE
    q2 = jnp.concatenate([q_ref[:, 0:HEAD_DIM], q_ref[:, HEAD_DIM:2 * HEAD_DIM]], axis=0)
    p_refs = (p0_ref, p1_ref)

    def colsum(p):
        return jnp.sum(p.reshape(p.shape[0] // SUBLANES, SUBLANES, p.shape[1]), axis=0)

    def weights_into(slot, c):
        p = jnp.exp2(_scores_t(kx_ref[pl.ds(pl.multiple_of(c * tk, tk), tk), :], q2))
        p_refs[slot][...] = p.astype(BF16)
        return colsum(p)

    def step(c, par, carry, with_scores):
        l8, acc = carry
        if with_scores:
            l8 = l8 + weights_into(1 - par, c + 1)
        acc = acc + jnp.dot(vtx_ref[c, 0:HEAD_DIM, :], p_refs[par][...],
                            preferred_element_type=F32)
        return l8, acc

    p_ctx = jnp.exp2(_scores_t(kc_ref[...], q2))
    l8 = colsum(p_ctx) + weights_into(0, 0)
    acc = jnp.dot(vtc_ref[0:HEAD_DIM, :], p_ctx.astype(BF16), preferred_element_type=F32)
    unroll = _steps_per_trip(n_chunks - 2, ATTN_BOUNDED_UNROLL)

    def body(i, carry):
        for u in range(unroll):
            carry = step(unroll * i + u, u % 2, carry, True)
        return carry

    carry = lax.fori_loop(0, (n_chunks - 2) // unroll, body, (l8, acc))
    carry = step(n_chunks - 2, 0, carry, True)
    l8, acc = step(n_chunks - 1, 1, carry, False)
    o = (acc / jnp.sum(l8, axis=0, keepdims=True)).T.astype(o_ref.dtype)
    o_ref[:, 0:HEAD_DIM] = o[0:tq]
    o_ref[:, HEAD_DIM:2 * HEAD_DIM] = o[tq:2 * tq]


def _attention_latent(q, k, vt, batch, n_lat, n_ctx, bounded):
    seq = n_lat // batch
    ctx = n_ctx // batch
    tq, tk = ATTN_Q_TILE, ATTN_K_TILE
    nq = seq // tq
    n_chunks = seq // tk
    assert seq % (2 * tk) == 0 and n_lat % ctx == 0 and n_ctx == tk and vt.shape[2] == tk
    r = 2 * tq
    body = _attn_bounded_kernel if bounded else _attn_kernel
    p_bufs = [pltpu.VMEM((tk, r), BF16), pltpu.VMEM((tk, r), BF16)]
    s_bufs = [] if bounded else [pltpu.VMEM((tk, r), F32), pltpu.VMEM((tk, r), F32)]
    return pl.pallas_call(
        functools.partial(body, n_chunks=n_chunks),
        out_shape=jax.ShapeDtypeStruct((n_lat, ATTN_DIM), BF16),
        grid=(batch, ATTN_KV_HEADS, nq),
        in_specs=[pl.BlockSpec((tq, 2 * HEAD_DIM), lambda b, g, j: (b * nq + j, g)),
                  pl.BlockSpec((ctx, HEAD_DIM), lambda b, g, j: (n_lat // ctx + b, g)),
                  pl.BlockSpec((None, VT_ROWS, ctx), lambda b, g, j: (n_lat // tk, g, b)),
                  pl.BlockSpec((seq, HEAD_DIM), lambda b, g, j: (b, g)),
                  pl.BlockSpec((n_chunks, VT_ROWS, tk), lambda b, g, j: (b, g, 0))],
        out_specs=pl.BlockSpec((tq, 2 * HEAD_DIM), lambda b, g, j: (b * nq + j, g)),
        scratch_shapes=s_bufs + p_bufs,
        compiler_params=pltpu.CompilerParams(
            dimension_semantics=("arbitrary",) * 3, vmem_limit_bytes=VMEM_LIMIT),
        name="attn_latent_bounded" if bounded else "attn_latent",
    )(q, k, vt, k, vt)


def _attn_ctx_kernel(q_ref, kc_ref, vtc_ref, o_ref):
    _attn_kernel(q_ref, kc_ref, vtc_ref, None, None, o_ref, None, None, None, None, n_chunks=0)


def _attention_context(q, k, vt, batch, n_lat, n_ctx):
    ctx = n_ctx // batch
    tk = ATTN_K_TILE
    return pl.pallas_call(
        _attn_ctx_kernel,
        out_shape=jax.ShapeDtypeStruct((n_ctx, ATTN_DIM), BF16),
        grid=(batch, ATTN_KV_HEADS),
        in_specs=[pl.BlockSpec((ctx, 2 * HEAD_DIM), lambda b, g: (n_lat // ctx + b, g)),
                  pl.BlockSpec((ctx, HEAD_DIM), lambda b, g: (n_lat // ctx + b, g)),
                  pl.BlockSpec((None, VT_ROWS, ctx), lambda b, g: (n_lat // tk, g, b))],
        out_specs=pl.BlockSpec((ctx, 2 * HEAD_DIM), lambda b, g: (b, g)),
        compiler_params=pltpu.CompilerParams(
            dimension_semantics=("arbitrary",) * 2, vmem_limit_bytes=VMEM_LIMIT),
        name="attn_context",
    )(q, k, vt)


def _post_kernel(xs_ref, up_ref, upp_ref, upn_ref, z_ref, yf_ref, yb_ref, at_ref, mod_ref,
                 pw_ref, ps_ref, sw_ref, wo_ref, n2_ref, w1_ref, w3_ref, w2_ref,
                 o_ref, buf_ref, *, tiles_per_seq, seq_len):
    i = pl.program_id(0)
    tm = xs_ref.shape[0]
    halo = SUBLANES
    j = i % tiles_per_seq

    buf_ref[0:halo, :] = jnp.where(j == 0, 0.0, upp_ref[...])
    buf_ref[halo:halo + tm, :] = up_ref[...]
    buf_ref[halo + tm:2 * halo + tm, :] = jnp.where(j == tiles_per_seq - 1, 0.0, upn_ref[...])
    ub = buf_ref[...]
    n = tm + 2 * halo
    s2 = ub + pltpu.roll(ub, 1, 0)
    s4 = pltpu.roll(s2, 1, 0) + pltpu.roll(s2, n - 1, 0)
    s8 = pltpu.roll(s4, 2, 0) + pltpu.roll(s4, n - 2, 0)
    s16 = pltpu.roll(s8, 4, 0) + pltpu.roll(s8, n - 4, 0)
    shape = (tm, POOL_DIM)
    grp = lax.broadcasted_iota(jnp.int32, shape, 1) // POOL_GDIM
    t = lax.broadcasted_iota(jnp.int32, shape, 0) + j * tm
    half = jnp.left_shift(1, grp)
    cnt = (jnp.minimum(t + half, seq_len) - jnp.maximum(t - half, 0)).astype(F32)
    sl = slice(halo, halo + tm)
    wsum = jnp.where(grp == 0, s2[sl], jnp.where(grp == 1, s4[sl],
                                                 jnp.where(grp == 2, s8[sl], s16[sl])))
    pooled = wsum / cnt - ub[sl]
    pool = jnp.dot(pooled.astype(BF16), pw_ref[...], preferred_element_type=F32) * ps_ref[...]

    gy = (yf_ref[...] + yb_ref[...]) * _silu(z_ref[...])
    gw = SSD_INNER // SSD_GROUPS
    parts = []
    for g in range(SSD_GROUPS):
        t_g = gy[:, g * gw:(g + 1) * gw]
        parts.append(t_g * lax.rsqrt(jnp.mean(t_g * t_g, axis=-1, keepdims=True) + EPS))
    ssd = jnp.concatenate(parts, axis=1) * sw_ref[...]

    mix = jnp.concatenate([pool.astype(BF16), ssd.astype(BF16), at_ref[...]], axis=1)
    x1 = xs_ref[...] + mod_ref[2:3, :] * jnp.dot(mix, wo_ref[...], preferred_element_type=F32)

    h = x1 * lax.rsqrt(jnp.mean(x1 * x1, axis=-1, keepdims=True) + EPS) * n2_ref[...]
    h = (h * (1.0 + mod_ref[4:5, :]) + mod_ref[3:4, :]).astype(BF16)
    a = jnp.dot(h, w1_ref[...], preferred_element_type=F32)
    b = jnp.dot(h, w3_ref[...], preferred_element_type=F32)
    ff = jnp.dot((_silu(a) * b).astype(BF16), w2_ref[...], preferred_element_type=F32)
    o_ref[...] = x1 + mod_ref[5:6, :] * ff


def _post(xs, upool, z, yf, yb, attn, mod, pw_bd, pscale, ssd_w, w_out, n2, w1, w3, w2,
          *, tm, row0, tiles_per_seq, mod_row):
    n_rows, d = xs.shape
    n_tiles = n_rows // tm
    off = row0 // tm
    blk8 = tm // SUBLANES
    last8 = upool.shape[0] // SUBLANES - 1
    tok = lambda w: pl.BlockSpec((tm, w), lambda i: (off + i, 0))
    loc = lambda w: pl.BlockSpec((tm, w), lambda i: (i, 0))
    return pl.pallas_call(
        functools.partial(_post_kernel, tiles_per_seq=tiles_per_seq, seq_len=tiles_per_seq * tm),
        out_shape=jax.ShapeDtypeStruct((n_rows, d), F32),
        grid=(n_tiles,),
        in_specs=[
            loc(d),
            tok(POOL_DIM),
            pl.BlockSpec((SUBLANES, POOL_DIM), lambda i: (jnp.maximum((off + i) * blk8 - 1, 0), 0)),
            pl.BlockSpec((SUBLANES, POOL_DIM),
                         lambda i: (jnp.minimum((off + i) * blk8 + blk8, last8), 0)),
            tok(SSD_INNER), tok(SSD_INNER), tok(SSD_INNER),
            loc(ATTN_DIM),
            pl.BlockSpec((None, N_MOD, d), lambda i: (mod_row(i), 0, 0)),
            _const_spec(pw_bd.shape), _const_spec((1, POOL_DIM)), _const_spec((1, SSD_INNER)),
            _const_spec(w_out.shape), _const_spec((1, d)),
            _const_spec(w1.shape), _const_spec(w3.shape), _const_spec(w2.shape),
        ],
        out_specs=loc(d),
        scratch_shapes=[pltpu.VMEM((tm + 2 * SUBLANES, POOL_DIM), F32)],
        compiler_params=pltpu.CompilerParams(
            dimension_semantics=("arbitrary",), vmem_limit_bytes=VMEM_LIMIT),
        name="post_mix_ffn",
    )(xs, upool, upool, upool, z, yf, yb, attn, mod, pw_bd, pscale, ssd_w, w_out, n2, w1, w3, w2)


def _rope_tables(seq, extra_rows):
    rows = seq // GRID_W
    axis_dim = HEAD_DIM // 2
    inv_freq = ROPE_THETA ** (-jnp.arange(0, axis_dim, 2, dtype=F32) / axis_dim)
    ang_r = jnp.arange(rows, dtype=F32)[:, None] * inv_freq[None, :]
    ang_c = jnp.arange(GRID_W, dtype=F32)[:, None] * inv_freq[None, :]
    cr, sr = (jnp.repeat(t, GRID_W, axis=0) for t in (jnp.cos(ang_r), jnp.sin(ang_r)))
    cc, sc = (jnp.tile(t, (rows, 1)) for t in (jnp.cos(ang_c), jnp.sin(ang_c)))
    cos = jnp.concatenate([cr, cr, cc, cc], axis=1)
    sin = jnp.concatenate([-sr, sr, -sc, sc], axis=1)
    cos = jnp.concatenate([cos, jnp.ones((extra_rows, HEAD_DIM), F32)], axis=0)
    sin = jnp.concatenate([sin, jnp.zeros((extra_rows, HEAD_DIM), F32)], axis=0)
    return cos, sin


def _fuse_w_in(w_in):
    d = w_in.shape[0]
    o_dt = POOL_DIM + SSD_INNER + SSD_XBC
    o_att = o_dt + SSD_DIRS * SSD_HEADS
    dt_pad = jnp.zeros((d, C_Q - C_DT - SSD_DIRS * SSD_HEADS), w_in.dtype)
    return jnp.concatenate([w_in[:, :o_dt], w_in[:, o_dt:o_att], dt_pad, w_in[:, o_att:]],
                           axis=1).astype(BF16)


def kernel(x, c, ctx, c_ctx, norm1_w, norm2_w, w_mod, b_mod, w_in, pool_w, pool_scale, conv_w, conv_b,
           dt_bias, a_log, d_skip, ssd_norm_w, q_norm_w, k_norm_w, w_out, w1, w3, w2):
    batch, seq, d = x.shape
    ctx_len = ctx.shape[1]
    depth = w_mod.shape[0]
    n_lat, n_ctx = batch * seq, batch * ctx_len
    assert batch + 1 <= SUBLANES and seq % TOKEN_TILE == 0 and seq % GRID_W == 0

    cond = jnp.zeros((SUBLANES, d), F32).at[:batch].set(c).at[batch].set(c_ctx)
    mod_all = _modulation(cond, w_mod, b_mod).reshape(depth, SUBLANES, N_MOD, d)
    cos_tab, sin_tab = _rope_tables(seq, TOKEN_TILE)

    xs_lat = x.reshape(n_lat, d)
    xs_ctx = ctx.reshape(n_ctx, d)
    tiles_per_seq = seq // TOKEN_TILE
    for layer in range(depth):
        need_ctx = layer < depth - 1
        mod = mod_all[layer]
        w_cat = _fuse_w_in(w_in[layer])
        upool, z, xbc, dt_raw, q, k, vt = _inproj(
            xs_lat, xs_ctx, mod, norm1_w[layer].reshape(1, d), w_cat,
            q_norm_w[layer].reshape(1, -1), k_norm_w[layer].reshape(1, -1), cos_tab, sin_tab, batch)
        yf, yb = _ssd(xbc, dt_raw, conv_w[layer], conv_b[layer], dt_bias[layer], a_log[layer],
                      d_skip[layer], batch, n_lat, n_ctx)
        score_bound = (HEAD_DIM * Q_SCALE * SCORE_BOUND_SLACK * jnp.max(jnp.abs(q_norm_w[layer]))
                       * jnp.max(jnp.abs(k_norm_w[layer])))
        attn_x = lax.cond(
            score_bound < SCORE_BOUND_LIMIT,
            functools.partial(_attention_latent, batch=batch, n_lat=n_lat, n_ctx=n_ctx, bounded=True),
            functools.partial(_attention_latent, batch=batch, n_lat=n_lat, n_ctx=n_ctx, bounded=False),
            q, k, vt)
        pw_bd = jax.scipy.linalg.block_diag(*[pool_w[layer, g] for g in range(POOL_GROUPS)]).astype(BF16)
        post = functools.partial(
            _post, upool=upool, z=z, yf=yf, yb=yb, mod=mod, pw_bd=pw_bd,
            pscale=pool_scale[layer].reshape(1, -1), ssd_w=ssd_norm_w[layer].reshape(1, -1),
            w_out=w_out[layer].astype(BF16), n2=norm2_w[layer].reshape(1, d),
            w1=w1[layer].astype(BF16), w3=w3[layer].astype(BF16), w2=w2[layer].astype(BF16))
        new_lat = post(xs_lat, attn=attn_x, tm=TOKEN_TILE, row0=0, tiles_per_seq=tiles_per_seq,
                       mod_row=lambda i: i // tiles_per_seq)
        if need_ctx:
            attn_c = _attention_context(q, k, vt, batch, n_lat, n_ctx)
            xs_ctx = post(xs_ctx, attn=attn_c, tm=ctx_len, row0=n_lat, tiles_per_seq=1,
                          mod_row=lambda i: batch)
        xs_lat = new_lat
    return xs_lat.reshape(batch, seq, d)
```

```python
import functools

import jax
import jax.numpy as jnp
from jax import lax
from jax.experimental import pallas as pl
from jax.experimental.pallas import tpu as pltpu

F32 = jnp.float32
BF16 = jnp.bfloat16

GRID_W = 64
EPS = 1e-6
N_MOD = 6
POOL_DIM = 256
POOL_GDIM = 64
POOL_GROUPS = 4
SSD_HEADS = 4
SSD_HEAD_DIM = 64
SSD_INNER = 256
SSD_GROUPS = 2
SSD_STATE = 64
SSD_CONV = 5
SSD_CHUNK = 128
SSD_DIRS = 2
SSD_XBC = 512
ATTN_HEADS = 4
ATTN_KV_HEADS = 2
HEAD_DIM = 128
ATTN_DIM = 512
KV_DIM = 256
ROPE_THETA = 10000.0
Q_SCALE = HEAD_DIM ** -0.5 * 1.4426950408889634
SCORE_BOUND_LIMIT = 60.0
SCORE_BOUND_SLACK = 1.02

LANES = 128
SUBLANES = 8
TOKEN_TILE = 512
INPROJ_ROWS = 256
POST_ROWS = 256
ATTN_Q_TILE = 512
ATTN_K_TILE = 512
ATTN_UNROLL = 6
ATTN_BOUNDED_UNROLL = 30
BF16_SUBLANES = 16
VT_ROWS = HEAD_DIM + BF16_SUBLANES
VMEM_LIMIT = 56 * 1024 * 1024

C_POOL, C_Z, C_XBC, C_DT, C_Q, C_K, C_V, C_END = 0, 256, 512, 1024, 1152, 1664, 1920, 2176


def _silu(x):
    return x * (1.0 / (1.0 + jnp.exp(-x)))


def _softplus(x):
    return jnp.maximum(x, 0.0) + jnp.log1p(jnp.exp(-jnp.abs(x)))


def _const_spec(shape):
    nd = len(shape)
    return pl.BlockSpec(shape, lambda *_: (0,) * nd, pipeline_mode=pl.Buffered(1))


def _mod_kernel(cond_ref, w_ref, b_ref, o_ref):
    s = _silu(cond_ref[...]).astype(BF16)
    o_ref[...] = jnp.dot(s, w_ref[...].astype(BF16), preferred_element_type=F32) + b_ref[...]


def _modulation(cond, w_mod, b_mod):
    depth, d, n = w_mod.shape
    bn = d
    return pl.pallas_call(
        _mod_kernel,
        out_shape=jax.ShapeDtypeStruct((depth, SUBLANES, n), F32),
        grid=(depth, n // bn),
        in_specs=[
            pl.BlockSpec((SUBLANES, d), lambda l, j: (0, 0)),
            pl.BlockSpec((None, d, bn), lambda l, j: (l, 0, j)),
            pl.BlockSpec((None, 1, bn), lambda l, j: (l, 0, j)),
        ],
        out_specs=pl.BlockSpec((None, SUBLANES, bn), lambda l, j: (l, 0, j)),
        compiler_params=pltpu.CompilerParams(
            dimension_semantics=("arbitrary", "arbitrary"), vmem_limit_bytes=VMEM_LIMIT),
        name="modulation",
    )(cond, w_mod, b_mod.reshape(depth, 1, n))


def _rope_partner(x):
    lane = lax.broadcasted_iota(jnp.int32, x.shape, 1)
    fwd = pltpu.roll(x, LANES - 32, 1)
    bwd = pltpu.roll(x, 32, 1)
    return jnp.where((lane % 64) < 32, fwd, bwd)


def _inproj_kernel(xl_ref, xc_ref, mod_ref, n1_ref, w_ref, qw_ref, kw_ref, cos_ref, sin_ref,
                   pool_ref, z_ref, xbc_ref, dt_ref, q_ref, k_ref, vt_ref, *, n_lat_tiles):
    i = pl.program_id(0)
    is_lat = i < n_lat_tiles
    tm = xl_ref.shape[0]
    for r0 in range(0, tm, INPROJ_ROWS):
        rows = slice(r0, r0 + INPROJ_ROWS)
        x = jnp.where(is_lat, xl_ref[rows, :], xc_ref[rows, :])
        ms = jnp.mean(x * x, axis=-1, keepdims=True)
        h = x * lax.rsqrt(ms + EPS) * n1_ref[...]
        h = h * (1.0 + mod_ref[1:2, :]) + mod_ref[0:1, :]
        u = jnp.dot(h.astype(BF16), w_ref[...], preferred_element_type=F32)
        pool_ref[rows, :] = u[:, C_POOL:C_Z]
        z_ref[rows, :] = u[:, C_Z:C_XBC]
        xbc_ref[rows, :] = u[:, C_XBC:C_DT]
        dt_ref[rows, :] = u[:, C_DT:C_Q]
        vt = u[:, C_V:C_END].T.astype(BF16)
        for hd in range(ATTN_KV_HEADS):
            vt_ref[hd * VT_ROWS:hd * VT_ROWS + HEAD_DIM, rows] = vt[hd * HEAD_DIM:(hd + 1) * HEAD_DIM]
            vt_ref[hd * VT_ROWS + HEAD_DIM:(hd + 1) * VT_ROWS, rows] = jnp.ones(
                (VT_ROWS - HEAD_DIM, INPROJ_ROWS), BF16)
        cos = cos_ref[rows, :]
        sin = sin_ref[rows, :]

        def norm_rope(t, w):
            t = t * lax.rsqrt(jnp.mean(t * t, axis=-1, keepdims=True) + EPS) * w
            return t * cos + _rope_partner(t) * sin

        for hd in range(ATTN_HEADS):
            t = norm_rope(u[:, C_Q + hd * HEAD_DIM:C_Q + (hd + 1) * HEAD_DIM], qw_ref[...])
            q_ref[rows, hd * HEAD_DIM:(hd + 1) * HEAD_DIM] = (t * Q_SCALE).astype(BF16)
        for hd in range(ATTN_KV_HEADS):
            t = norm_rope(u[:, C_K + hd * HEAD_DIM:C_K + (hd + 1) * HEAD_DIM], kw_ref[...])
            k_ref[rows, hd * HEAD_DIM:(hd + 1) * HEAD_DIM] = t.astype(BF16)


def _inproj(xs_lat, xs_ctx, mod, n1, w_cat, qw, kw, cos_tab, sin_tab, batch):
    n_lat, d = xs_lat.shape
    n_ctx = xs_ctx.shape[0]
    tm = TOKEN_TILE
    assert n_ctx == tm and n_lat % (batch * tm) == 0
    n_lat_tiles = n_lat // tm
    tiles_per_seq = n_lat_tiles // batch
    n_tok = n_lat + n_ctx
    grid = (n_lat_tiles + 1,)

    def lat_idx(i):
        return jnp.minimum(i, n_lat_tiles - 1)

    def rope_idx(i):
        return jnp.where(i < n_lat_tiles, i % tiles_per_seq, tiles_per_seq)

    def mod_idx(i):
        return jnp.where(i < n_lat_tiles, i // tiles_per_seq, batch)

    row = lambda w: pl.BlockSpec((tm, w), lambda i: (i, 0))
    outs = [(POOL_DIM, F32), (SSD_INNER, F32), (SSD_XBC, F32), (LANES, F32),
            (ATTN_DIM, BF16), (KV_DIM, BF16)]
    return pl.pallas_call(
        functools.partial(_inproj_kernel, n_lat_tiles=n_lat_tiles),
        out_shape=[jax.ShapeDtypeStruct((n_tok, w), t) for w, t in outs]
        + [jax.ShapeDtypeStruct((n_tok // tm, ATTN_KV_HEADS * VT_ROWS, tm), BF16)],
        grid=grid,
        in_specs=[
            pl.BlockSpec((tm, d), lambda i: (lat_idx(i), 0)),
            pl.BlockSpec((tm, d), lambda i: (0, 0)),
            pl.BlockSpec((None, N_MOD, d), lambda i: (mod_idx(i), 0, 0)),
            _const_spec((1, d)),
            _const_spec(w_cat.shape),
            _const_spec((1, HEAD_DIM)),
            _const_spec((1, HEAD_DIM)),
            pl.BlockSpec((tm, HEAD_DIM), lambda i: (rope_idx(i), 0)),
            pl.BlockSpec((tm, HEAD_DIM), lambda i: (rope_idx(i), 0)),
        ],
        out_specs=[row(w) for w, _ in outs]
        + [pl.BlockSpec((None, ATTN_KV_HEADS * VT_ROWS, tm), lambda i: (i, 0, 0))],
        compiler_params=pltpu.CompilerParams(
            dimension_semantics=("arbitrary",), vmem_limit_bytes=VMEM_LIMIT),
        name="inproj",
    )(xs_lat, xs_ctx, mod, n1, w_cat, qw, kw, cos_tab, sin_tab)


def _conv_kernel(x_ref, prev_ref, next_ref, cw_ref, cb_ref, o_ref, buf_ref, *, tiles_per_seq,
                 n_lat_tiles):
    i = pl.program_id(0)
    tm = x_ref.shape[0]
    first = jnp.logical_or(i >= n_lat_tiles, i % tiles_per_seq == 0)
    last = jnp.logical_or(i >= n_lat_tiles, i % tiles_per_seq == tiles_per_seq - 1)
    buf_ref[0:SUBLANES, :] = jnp.where(first, 0.0, prev_ref[...])
    buf_ref[SUBLANES:SUBLANES + tm, :] = x_ref[...]
    buf_ref[SUBLANES + tm:2 * SUBLANES + tm, :] = jnp.where(last, 0.0, next_ref[...])
    u = buf_ref[...]
    n = u.shape[0]
    pad = SSD_CONV // 2
    acc = cb_ref[...] + cw_ref[pad:pad + 1, :] * u[SUBLANES:SUBLANES + tm]
    for k in range(SSD_CONV):
        if k != pad:
            shifted = pltpu.roll(u, (pad - k) % n, 0)
            acc = acc + cw_ref[k:k + 1, :] * shifted[SUBLANES:SUBLANES + tm]
    o_ref[...] = _silu(acc)


def _ssd_conv(xbc, conv_w, conv_b, batch, n_lat, n_ctx):
    n_tok = xbc.shape[0]
    tm = n_ctx // batch
    assert (n_lat // batch) % tm == 0
    blk8 = tm // SUBLANES
    last8 = n_tok // SUBLANES - 1
    cw = jnp.zeros((SUBLANES, SSD_XBC), F32).at[:SSD_CONV].set(conv_w)
    return pl.pallas_call(
        functools.partial(_conv_kernel, tiles_per_seq=n_lat // batch // tm, n_lat_tiles=n_lat // tm),
        out_shape=jax.ShapeDtypeStruct((n_tok, SSD_XBC), F32),
        grid=(n_tok // tm,),
        in_specs=[
            pl.BlockSpec((tm, SSD_XBC), lambda i: (i, 0)),
            pl.BlockSpec((SUBLANES, SSD_XBC), lambda i: (jnp.maximum(i * blk8 - 1, 0), 0)),
            pl.BlockSpec((SUBLANES, SSD_XBC), lambda i: (jnp.minimum(i * blk8 + blk8, last8), 0)),
            _const_spec((SUBLANES, SSD_XBC)), _const_spec((1, SSD_XBC)),
        ],
        out_specs=pl.BlockSpec((tm, SSD_XBC), lambda i: (i, 0)),
        scratch_shapes=[pltpu.VMEM((tm + 2 * SUBLANES, SSD_XBC), F32)],
        compiler_params=pltpu.CompilerParams(
            dimension_semantics=("arbitrary",), vmem_limit_bytes=VMEM_LIMIT),
        name="ssd_conv",
    )(xbc, xbc, xbc, cw, conv_b.reshape(1, -1))


def _ssd_kernel(xf_ref, dtf_ref, dtf_next_ref, xb_ref, dtb_ref, dtb_next_ref,
                bias_ref, alog_ref, dsk_ref, spread_ref, ecol_ref,
                yf_ref, yb_ref, h_ref, et_sc, spread_sc, ecols_sc):
    i = pl.program_id(1)
    q = SSD_CHUNK

    @pl.when(i == 0)
    def _():
        h_ref[...] = jnp.zeros_like(h_ref)

    row = lax.broadcasted_iota(jnp.int32, (q, q), 0)
    col = lax.broadcasted_iota(jnp.int32, (q, q), 1)
    n_col = SSD_DIRS * SSD_HEADS
    heads_per_group = SSD_HEADS // SSD_GROUPS
    pair = heads_per_group * SSD_HEAD_DIM
    lane_half = col // SSD_HEAD_DIM
    st_row = lax.broadcasted_iota(jnp.int32, (SSD_GROUPS * SSD_STATE, SSD_INNER), 0)
    st_col = lax.broadcasted_iota(jnp.int32, (SSD_GROUPS * SSD_STATE, SSD_INNER), 1)
    same_group = (st_row // SSD_STATE) == (st_col // pair)
    contract0 = (((0,), (0,)), ((), ()))

    def bf16_terms(v, n_terms):
        terms, rest = [], v
        for _ in range(n_terms):
            t = rest.astype(BF16).astype(F32)
            terms.append(t)
            rest = rest - t
        rows = -(-n_terms * n_col // BF16_SUBLANES) * BF16_SUBLANES
        if rows > n_terms * n_col:
            terms.append(jnp.zeros((rows - n_terms * n_col, v.shape[1]), F32))
        return jnp.concatenate(terms, axis=0).astype(BF16)

    fwd_rows = lax.broadcasted_iota(jnp.int32, (n_col, q), 0) < SSD_HEADS

    def token_decay_sums(dtf_blk_ref, dtb_blk_ref):
        dt_raw = jnp.where(fwd_rows, dtf_blk_ref[...].T[0:n_col], dtb_blk_ref[...].T[0:n_col])
        dt = _softplus(dt_raw + bias_ref[...])
        a = dt * -jnp.exp(alog_ref[...])
        tri_u = (row <= col).astype(F32)
        cs = jnp.dot(a, tri_u, preferred_element_type=F32, precision=lax.Precision.HIGHEST)
        return dt, a, cs

    def token_scalars(dt, a, cs):
        tot = cs[:, q - 1:q]
        e_t = jnp.where(fwd_rows, cs, cs - a)
        grow, shrink = jnp.exp(e_t), jnp.exp(tot - e_t)
        w_in = jnp.where(fwd_rows, grow, shrink)
        w_out = jnp.where(fwd_rows, shrink, grow)
        terms = jnp.concatenate(
            [bf16_terms(dt, 2), bf16_terms(w_in, 2), bf16_terms(dt * w_out, 2)], axis=1)
        spread = lax.dot_general(terms, spread_ref[...], contract0,
                                 preferred_element_type=F32)
        e_cols = lax.dot_general(bf16_terms(e_t, 3), ecol_ref[...], contract0,
                                 preferred_element_type=F32)
        return e_t, spread, e_cols

    def into_scratch(vals):
        et_sc[...], spread_sc[...], ecols_sc[...] = vals

    @pl.when(i == 0)
    def _():
        into_scratch(token_scalars(*token_decay_sums(dtf_ref, dtb_ref)))

    next_sums = token_decay_sums(dtf_next_ref, dtb_next_ref)

    zero = jnp.zeros((), BF16)

    def input_matmuls(d, xc):
        ch = slice(d * SSD_INNER, (d + 1) * SSD_INNER)
        dt_x, w_in_x, dtw_x = spread_sc[0:q, ch], spread_sc[q:2 * q, ch], spread_sc[2 * q:3 * q, ch]
        x = xc[:, 0:SSD_INNER]
        xdt = (x * dt_x).astype(BF16)
        xdtw = (x * dtw_x).astype(BF16)
        b_all = xc[:, SSD_INNER:SSD_INNER + SSD_GROUPS * SSD_STATE].astype(BF16)
        c_all = xc[:, SSD_INNER + SSD_GROUPS * SSD_STATE:].astype(BF16)
        hs = h_ref[d]
        y_off = jnp.dot(c_all, hs.astype(BF16), preferred_element_type=F32) * w_in_x
        cbs = [lax.dot_general(jnp.where(lane_half == g, c_all, zero), b_all,
                               (((1,), (1,)), ((), ())), preferred_element_type=F32)
               for g in range(SSD_GROUPS)]
        upd = lax.dot_general(b_all, xdtw, contract0, preferred_element_type=F32)
        exp_tot = w_in_x[q - 1:q, :] if d == 0 else w_in_x[0:1, :]
        h_ref[d] = exp_tot * hs + jnp.where(same_group, upd, 0.0)
        return x, xdt, y_off, cbs

    def decay_matmuls(d, x, xdt, y_off, cbs, y_ref):
        for g in range(SSD_GROUPS):
            sl = slice(g * pair, (g + 1) * pair)
            y_g = y_off[:, sl] + dsk_ref[d:d + 1, sl] * x[:, sl]
            for hh in range(heads_per_group):
                c = d * SSD_HEADS + g * heads_per_group + hh
                e_col = ecols_sc[:, c * q:(c + 1) * q]
                e_row = et_sc[c:c + 1, :]
                if d == 0:
                    lmat = jnp.where(row >= col, jnp.exp(e_col - e_row), 0.0)
                else:
                    lmat = jnp.where(col >= row, jnp.exp(e_row - e_col), 0.0)
                x_h = jnp.where(lane_half == hh, xdt[:, sl], zero)
                y_g = y_g + jnp.dot((cbs[g] * lmat).astype(BF16), x_h,
                                    preferred_element_type=F32)
            y_ref[:, sl] = y_g

    lin_f = input_matmuls(0, xf_ref)
    lin_b = input_matmuls(1, xb_ref)
    next_scalars = token_scalars(*next_sums)
    decay_matmuls(0, *lin_f, yf_ref)
    decay_matmuls(1, *lin_b, yb_ref)
    into_scratch(next_scalars)


def _ssd(xc, dt_raw, dt_bias, a_log, d_skip, batch, n_lat, n_ctx):
    n_tok = xc.shape[0]
    q = SSD_CHUNK
    lat_ch = n_lat // batch // q
    ctx_ch = n_ctx // batch // q
    n_ch = lat_ch + ctx_ch

    def fwd(b, i):
        return jnp.where(i < ctx_ch, batch * lat_ch + b * ctx_ch + i, b * lat_ch + i - ctx_ch)

    def bwd(b, i):
        return jnp.where(i < ctx_ch, batch * lat_ch + b * ctx_ch + ctx_ch - 1 - i,
                         b * lat_ch + n_ch - 1 - i)

    def specs(ch):
        return [
            pl.BlockSpec((q, SSD_XBC), lambda b, i: (ch(b, i), 0)),
            pl.BlockSpec((q, LANES), lambda b, i: (ch(b, i), 0)),
            pl.BlockSpec((q, LANES), lambda b, i: (ch(b, jnp.minimum(i + 1, n_ch - 1)), 0)),
        ]

    n_col = SSD_DIRS * SSD_HEADS
    rows = lambda v: jnp.broadcast_to(v.reshape(n_col, 1), (n_col, q))
    dsk = jnp.repeat(d_skip, SSD_HEAD_DIM, axis=1)
    term_row = jnp.arange(2 * BF16_SUBLANES) % n_col
    live3 = jnp.arange(2 * BF16_SUBLANES) < 3 * n_col
    col_of_channel = jnp.arange(SSD_DIRS * SSD_INNER) // SSD_HEAD_DIM
    col_of_block = jnp.arange(n_col * q) // q
    spread = (term_row[:BF16_SUBLANES, None] == col_of_channel[None, :]).astype(BF16)
    ecol = ((term_row[:, None] == col_of_block[None, :]) & live3[:, None]).astype(BF16)
    return pl.pallas_call(
        _ssd_kernel,
        out_shape=[jax.ShapeDtypeStruct((n_tok, SSD_INNER), F32)] * 2,
        grid=(batch, n_ch),
        in_specs=specs(fwd) + specs(bwd) + [
            _const_spec((n_col, q)), _const_spec((n_col, q)), _const_spec((SSD_DIRS, SSD_INNER)),
            _const_spec(spread.shape), _const_spec(ecol.shape),
        ],
        out_specs=[pl.BlockSpec((q, SSD_INNER), lambda b, i: (fwd(b, i), 0)),
                   pl.BlockSpec((q, SSD_INNER), lambda b, i: (bwd(b, i), 0))],
        scratch_shapes=[pltpu.VMEM((SSD_DIRS, SSD_GROUPS * SSD_STATE, SSD_INNER), F32),
                        pltpu.VMEM((n_col, q), F32),
                        pltpu.VMEM((3 * q, SSD_DIRS * SSD_INNER), F32),
                        pltpu.VMEM((q, n_col * q), F32)],
        compiler_params=pltpu.CompilerParams(
            dimension_semantics=("arbitrary", "arbitrary"), vmem_limit_bytes=VMEM_LIMIT),
        name="ssd_scan",
    )(xc, dt_raw, dt_raw, xc, dt_raw, dt_raw,
      rows(dt_bias), rows(a_log), dsk, spread, ecol)


def _steps_per_trip(n_steps, preferred):
    return max(u for u in range(2, preferred + 1, 2) if n_steps % u == 0) if n_steps else 2


def _scores_t(k, q2):
    return lax.dot_general(k, q2, (((1,), (1,)), ((), ())), preferred_element_type=F32)


def _softmax_stage(s, m_blk, m):
    if m is None:
        return m_blk, None, jnp.exp2((s - m_blk).astype(BF16))
    m_new = jnp.maximum(m, m_blk)
    return m_new, jnp.exp2(m - m_new), jnp.exp2((s - m_new).astype(BF16))


def _attn_kernel(q_ref, kc_ref, vtc_ref, kx_ref, vtx_ref, o_ref, s0_ref, s1_ref, p0_ref, p1_ref,
                 *, n_chunks):
    tq = q_ref.shape[0]
    tk = ATTN_K_TILE
    q2 = jnp.concatenate([q_ref[:, 0:HEAD_DIM], q_ref[:, HEAD_DIM:2 * HEAD_DIM]], axis=0)
    s_refs, p_refs = (s0_ref, s1_ref), (p0_ref, p1_ref)

    def scores_into(slot, c):
        s = _scores_t(kx_ref[pl.ds(pl.multiple_of(c * tk, tk), tk), :], q2)
        s_refs[slot][...] = s
        return jnp.max(s, axis=0, keepdims=True)

    def softmax_into(slot, m_blk, m):
        m, alpha, p = _softmax_stage(s_refs[slot][...], m_blk, m)
        p_refs[slot][...] = p
        return m, alpha

    def step(c, par, carry, with_scores, with_softmax):
        m, acc, alpha, m_blk = carry
        m_blk_next, alpha_next = m_blk, alpha
        if with_scores:
            m_blk_next = scores_into(par, c + 2)
        if with_softmax:
            m, alpha_next = softmax_into(1 - par, m_blk, m)
        acc = alpha * acc + jnp.dot(vtx_ref[c], p_refs[par][...], preferred_element_type=F32)
        return m, acc, alpha_next, m_blk_next

    if n_chunks:
        m_blk0 = scores_into(0, 0)
        m_blk1 = scores_into(1, 1)
    s_ctx = _scores_t(kc_ref[...], q2)
    m, _, p_ctx = _softmax_stage(s_ctx, jnp.max(s_ctx, axis=0, keepdims=True), None)
    acc = jnp.dot(vtc_ref[...], p_ctx, preferred_element_type=F32)
    if n_chunks:
        m, alpha = softmax_into(0, m_blk0, m)
        carry = (m, acc, alpha, m_blk1)
        unroll = _steps_per_trip(n_chunks - 2, ATTN_UNROLL)

        def body(i, carry):
            for u in range(unroll):
                carry = step(unroll * i + u, u % 2, carry, True, True)
            return carry

        carry = lax.fori_loop(0, (n_chunks - 2) // unroll, body, carry)
        carry = step(n_chunks - 2, 0, carry, False, True)
        _, acc, _, _ = step(n_chunks - 1, 1, carry, False, False)
    o = (acc[0:HEAD_DIM] / acc[HEAD_DIM:HEAD_DIM + 1]).T.astype(o_ref.dtype)
    o_ref[:, 0:HEAD_DIM] = o[0:tq]
    o_ref[:, HEAD_DIM:2 * HEAD_DIM] = o[tq:2 * tq]


def _attn_bounded_kernel(q_ref, kc_ref, vtc_ref, kx_ref, vtx_ref, o_ref, p0_ref, p1_ref, *, n_chunks):
    tq = q_ref.shape[0]
    tk = ATTN_K_TILE
    q2 = jnp.concatenate([q_ref[:, 0:HEAD_DIM], q_ref[:, HEAD_DIM:2 * HEAD_DIM]], axis=0)
    p_refs = (p0_ref, p1_ref)

    def colsum(p):
        return jnp.sum(p.reshape(p.shape[0] // SUBLANES, SUBLANES, p.shape[1]), axis=0)

    def weights_into(slot, c):
        p = jnp.exp2(_scores_t(kx_ref[pl.ds(pl.multiple_of(c * tk, tk), tk), :], q2))
        p_refs[slot][...] = p.astype(BF16)
        return colsum(p)

    def step(c, par, carry, with_scores):
        l8, acc = carry
        if with_scores:
            l8 = l8 + weights_into(1 - par, c + 1)
        acc = acc + jnp.dot(vtx_ref[c, 0:HEAD_DIM, :], p_refs[par][...],
                            preferred_element_type=F32)
        return l8, acc

    p_ctx = jnp.exp2(_scores_t(kc_ref[...], q2))
    l8 = colsum(p_ctx) + weights_into(0, 0)
    acc = jnp.dot(vtc_ref[0:HEAD_DIM, :], p_ctx.astype(BF16), preferred_element_type=F32)
    unroll = _steps_per_trip(n_chunks - 2, ATTN_BOUNDED_UNROLL)

    def body(i, carry):
        for u in range(unroll):
            carry = step(unroll * i + u, u % 2, carry, True)
        return carry

    carry = lax.fori_loop(0, (n_chunks - 2) // unroll, body, (l8, acc))
    carry = step(n_chunks - 2, 0, carry, True)
    l8, acc = step(n_chunks - 1, 1, carry, False)
    o = (acc / jnp.sum(l8, axis=0, keepdims=True)).T.astype(o_ref.dtype)
    o_ref[:, 0:HEAD_DIM] = o[0:tq]
    o_ref[:, HEAD_DIM:2 * HEAD_DIM] = o[tq:2 * tq]


def _attention_latent(q, k, vt, batch, n_lat, n_ctx, bounded):
    seq = n_lat // batch
    ctx = n_ctx // batch
    tq, tk = ATTN_Q_TILE, ATTN_K_TILE
    nq = seq // tq
    n_chunks = seq // tk
    assert seq % (2 * tk) == 0 and n_lat % ctx == 0 and n_ctx == tk and vt.shape[2] == tk
    r = 2 * tq
    body = _attn_bounded_kernel if bounded else _attn_kernel
    p_bufs = [pltpu.VMEM((tk, r), BF16), pltpu.VMEM((tk, r), BF16)]
    s_bufs = [] if bounded else [pltpu.VMEM((tk, r), F32), pltpu.VMEM((tk, r), F32)]
    return pl.pallas_call(
        functools.partial(body, n_chunks=n_chunks),
        out_shape=jax.ShapeDtypeStruct((n_lat, ATTN_DIM), BF16),
        grid=(batch, ATTN_KV_HEADS, nq),
        in_specs=[pl.BlockSpec((tq, 2 * HEAD_DIM), lambda b, g, j: (b * nq + j, g)),
                  pl.BlockSpec((ctx, HEAD_DIM), lambda b, g, j: (n_lat // ctx + b, g)),
                  pl.BlockSpec((None, VT_ROWS, ctx), lambda b, g, j: (n_lat // tk, g, b)),
                  pl.BlockSpec((seq, HEAD_DIM), lambda b, g, j: (b, g)),
                  pl.BlockSpec((n_chunks, VT_ROWS, tk), lambda b, g, j: (b, g, 0))],
        out_specs=pl.BlockSpec((tq, 2 * HEAD_DIM), lambda b, g, j: (b * nq + j, g)),
        scratch_shapes=s_bufs + p_bufs,
        compiler_params=pltpu.CompilerParams(
            dimension_semantics=("arbitrary",) * 3, vmem_limit_bytes=VMEM_LIMIT),
        name="attn_latent_bounded" if bounded else "attn_latent",
    )(q, k, vt, k, vt)


def _attn_ctx_kernel(q_ref, kc_ref, vtc_ref, o_ref):
    _attn_kernel(q_ref, kc_ref, vtc_ref, None, None, o_ref, None, None, None, None, n_chunks=0)


def _attention_context(q, k, vt, batch, n_lat, n_ctx):
    ctx = n_ctx // batch
    tk = ATTN_K_TILE
    return pl.pallas_call(
        _attn_ctx_kernel,
        out_shape=jax.ShapeDtypeStruct((n_ctx, ATTN_DIM), BF16),
        grid=(batch, ATTN_KV_HEADS),
        in_specs=[pl.BlockSpec((ctx, 2 * HEAD_DIM), lambda b, g: (n_lat // ctx + b, g)),
                  pl.BlockSpec((ctx, HEAD_DIM), lambda b, g: (n_lat // ctx + b, g)),
                  pl.BlockSpec((None, VT_ROWS, ctx), lambda b, g: (n_lat // tk, g, b))],
        out_specs=pl.BlockSpec((ctx, 2 * HEAD_DIM), lambda b, g: (b, g)),
        compiler_params=pltpu.CompilerParams(
            dimension_semantics=("arbitrary",) * 2, vmem_limit_bytes=VMEM_LIMIT),
        name="attn_context",
    )(q, k, vt)


def _post_kernel(xs_ref, up_ref, upp_ref, upn_ref, z_ref, yf_ref, yb_ref, at_ref, mod_ref,
                 pw_ref, ps_ref, sw_ref, wo_ref, n2_ref, w1_ref, w3_ref, w2_ref,
                 o_ref, buf_ref, *, tiles_per_seq, seq_len):
    i = pl.program_id(0)
    tm = xs_ref.shape[0]
    halo = SUBLANES
    j = i % tiles_per_seq

    buf_ref[0:halo, :] = jnp.where(j == 0, 0.0, upp_ref[...])
    buf_ref[halo:halo + tm, :] = up_ref[...]
    buf_ref[halo + tm:2 * halo + tm, :] = jnp.where(j == tiles_per_seq - 1, 0.0, upn_ref[...])
    ub = buf_ref[...]
    n = tm + 2 * halo
    s2 = ub + pltpu.roll(ub, 1, 0)
    s4 = pltpu.roll(s2, 1, 0) + pltpu.roll(s2, n - 1, 0)
    s8 = pltpu.roll(s4, 2, 0) + pltpu.roll(s4, n - 2, 0)
    s16 = pltpu.roll(s8, 4, 0) + pltpu.roll(s8, n - 4, 0)
    shape = (tm, POOL_DIM)
    grp = lax.broadcasted_iota(jnp.int32, shape, 1) // POOL_GDIM
    t = lax.broadcasted_iota(jnp.int32, shape, 0) + j * tm
    half = jnp.left_shift(1, grp)
    cnt = (jnp.minimum(t + half, seq_len) - jnp.maximum(t - half, 0)).astype(F32)
    sl = slice(halo, halo + tm)
    wsum = jnp.where(grp == 0, s2[sl], jnp.where(grp == 1, s4[sl],
                                                 jnp.where(grp == 2, s8[sl], s16[sl])))
    pooled = wsum / cnt - ub[sl]
    pool = jnp.dot(pooled.astype(BF16), pw_ref[...], preferred_element_type=F32) * ps_ref[...]

    gy = (yf_ref[...] + yb_ref[...]) * _silu(z_ref[...])
    gw = SSD_INNER // SSD_GROUPS
    parts = []
    for g in range(SSD_GROUPS):
        t_g = gy[:, g * gw:(g + 1) * gw]
        parts.append(t_g * lax.rsqrt(jnp.mean(t_g * t_g, axis=-1, keepdims=True) + EPS))
    ssd = jnp.concatenate(parts, axis=1) * sw_ref[...]

    mix = jnp.concatenate([pool.astype(BF16), ssd.astype(BF16), at_ref[...]], axis=1)
    blocks = [slice(r0, r0 + min(POST_ROWS, tm)) for r0 in range(0, tm, POST_ROWS)]
    x1 = [xs_ref[rows, :] + mod_ref[2:3, :] * jnp.dot(mix[rows], wo_ref[...],
                                                       preferred_element_type=F32)
          for rows in blocks]

    def modulated_norm(v):
        h = v * lax.rsqrt(jnp.mean(v * v, axis=-1, keepdims=True) + EPS) * n2_ref[...]
        return (h * (1.0 + mod_ref[4:5, :]) + mod_ref[3:4, :]).astype(BF16)

    gated = []
    for v in x1:
        h = modulated_norm(v)
        a = jnp.dot(h, w1_ref[...], preferred_element_type=F32)
        b = jnp.dot(h, w3_ref[...], preferred_element_type=F32)
        gated.append((_silu(a) * b).astype(BF16))
    for rows, v, gt in zip(blocks, x1, gated):
        ff = jnp.dot(gt, w2_ref[...], preferred_element_type=F32)
        o_ref[rows, :] = v + mod_ref[5:6, :] * ff


def _post(xs, upool, z, yf, yb, attn, mod, pw_bd, pscale, ssd_w, w_out, n2, w1, w3, w2,
          *, tm, row0, tiles_per_seq, mod_row):
    n_rows, d = xs.shape
    n_tiles = n_rows // tm
    off = row0 // tm
    blk8 = tm // SUBLANES
    last8 = upool.shape[0] // SUBLANES - 1
    tok = lambda w: pl.BlockSpec((tm, w), lambda i: (off + i, 0))
    loc = lambda w: pl.BlockSpec((tm, w), lambda i: (i, 0))
    return pl.pallas_call(
        functools.partial(_post_kernel, tiles_per_seq=tiles_per_seq, seq_len=tiles_per_seq * tm),
        out_shape=jax.ShapeDtypeStruct((n_rows, d), F32),
        grid=(n_tiles,),
        in_specs=[
            loc(d),
            tok(POOL_DIM),
            pl.BlockSpec((SUBLANES, POOL_DIM), lambda i: (jnp.maximum((off + i) * blk8 - 1, 0), 0)),
            pl.BlockSpec((SUBLANES, POOL_DIM),
                         lambda i: (jnp.minimum((off + i) * blk8 + blk8, last8), 0)),
            tok(SSD_INNER), tok(SSD_INNER), tok(SSD_INNER),
            loc(ATTN_DIM),
            pl.BlockSpec((None, N_MOD, d), lambda i: (mod_row(i), 0, 0)),
            _const_spec(pw_bd.shape), _const_spec((1, POOL_DIM)), _const_spec((1, SSD_INNER)),
            _const_spec(w_out.shape), _const_spec((1, d)),
            _const_spec(w1.shape), _const_spec(w3.shape), _const_spec(w2.shape),
        ],
        out_specs=loc(d),
        scratch_shapes=[pltpu.VMEM((tm + 2 * SUBLANES, POOL_DIM), F32)],
        compiler_params=pltpu.CompilerParams(
            dimension_semantics=("arbitrary",), vmem_limit_bytes=VMEM_LIMIT),
        name="post_mix_ffn",
    )(xs, upool, upool, upool, z, yf, yb, attn, mod, pw_bd, pscale, ssd_w, w_out, n2, w1, w3, w2)


def _rope_tables(seq, extra_rows):
    rows = seq // GRID_W
    axis_dim = HEAD_DIM // 2
    inv_freq = ROPE_THETA ** (-jnp.arange(0, axis_dim, 2, dtype=F32) / axis_dim)
    ang_r = jnp.arange(rows, dtype=F32)[:, None] * inv_freq[None, :]
    ang_c = jnp.arange(GRID_W, dtype=F32)[:, None] * inv_freq[None, :]
    cr, sr = (jnp.repeat(t, GRID_W, axis=0) for t in (jnp.cos(ang_r), jnp.sin(ang_r)))
    cc, sc = (jnp.tile(t, (rows, 1)) for t in (jnp.cos(ang_c), jnp.sin(ang_c)))
    cos = jnp.concatenate([cr, cr, cc, cc], axis=1)
    sin = jnp.concatenate([-sr, sr, -sc, sc], axis=1)
    cos = jnp.concatenate([cos, jnp.ones((extra_rows, HEAD_DIM), F32)], axis=0)
    sin = jnp.concatenate([sin, jnp.zeros((extra_rows, HEAD_DIM), F32)], axis=0)
    return cos, sin


def _fuse_w_in(w_in):
    d = w_in.shape[0]
    o_dt = POOL_DIM + SSD_INNER + SSD_XBC
    o_att = o_dt + SSD_DIRS * SSD_HEADS
    dt_pad = jnp.zeros((d, C_Q - C_DT - SSD_DIRS * SSD_HEADS), w_in.dtype)
    return jnp.concatenate([w_in[:, :o_dt], w_in[:, o_dt:o_att], dt_pad, w_in[:, o_att:]],
                           axis=1).astype(BF16)


def kernel(x, c, ctx, c_ctx, norm1_w, norm2_w, w_mod, b_mod, w_in, pool_w, pool_scale, conv_w, conv_b,
           dt_bias, a_log, d_skip, ssd_norm_w, q_norm_w, k_norm_w, w_out, w1, w3, w2):
    batch, seq, d = x.shape
    ctx_len = ctx.shape[1]
    depth = w_mod.shape[0]
    n_lat, n_ctx = batch * seq, batch * ctx_len
    assert batch + 1 <= SUBLANES and seq % TOKEN_TILE == 0 and seq % GRID_W == 0

    cond = jnp.zeros((SUBLANES, d), F32).at[:batch].set(c).at[batch].set(c_ctx)
    mod_all = _modulation(cond, w_mod, b_mod).reshape(depth, SUBLANES, N_MOD, d)
    cos_tab, sin_tab = _rope_tables(seq, TOKEN_TILE)

    xs_lat = x.reshape(n_lat, d)
    xs_ctx = ctx.reshape(n_ctx, d)
    tiles_per_seq = seq // TOKEN_TILE
    for layer in range(depth):
        need_ctx = layer < depth - 1
        mod = mod_all[layer]
        w_cat = _fuse_w_in(w_in[layer])
        upool, z, xbc, dt_raw, q, k, vt = _inproj(
            xs_lat, xs_ctx, mod, norm1_w[layer].reshape(1, d), w_cat,
            q_norm_w[layer].reshape(1, -1), k_norm_w[layer].reshape(1, -1), cos_tab, sin_tab, batch)
        xc = _ssd_conv(xbc, conv_w[layer], conv_b[layer], batch, n_lat, n_ctx)
        yf, yb = _ssd(xc, dt_raw, dt_bias[layer], a_log[layer], d_skip[layer], batch, n_lat, n_ctx)
        score_bound = (HEAD_DIM * Q_SCALE * SCORE_BOUND_SLACK * jnp.max(jnp.abs(q_norm_w[layer]))
                       * jnp.max(jnp.abs(k_norm_w[layer])))
        attn_x = lax.cond(
            score_bound < SCORE_BOUND_LIMIT,
            functools.partial(_attention_latent, batch=batch, n_lat=n_lat, n_ctx=n_ctx, bounded=True),
            functools.partial(_attention_latent, batch=batch, n_lat=n_lat, n_ctx=n_ctx, bounded=False),
            q, k, vt)
        pw_bd = jax.scipy.linalg.block_diag(*[pool_w[layer, g] for g in range(POOL_GROUPS)]).astype(BF16)
        post = functools.partial(
            _post, upool=upool, z=z, yf=yf, yb=yb, mod=mod, pw_bd=pw_bd,
            pscale=pool_scale[layer].reshape(1, -1), ssd_w=ssd_norm_w[layer].reshape(1, -1),
            w_out=w_out[layer].astype(BF16), n2=norm2_w[layer].reshape(1, d),
            w1=w1[layer].astype(BF16), w3=w3[layer].astype(BF16), w2=w2[layer].astype(BF16))
        new_lat = post(xs_lat, attn=attn_x, tm=TOKEN_TILE, row0=0, tiles_per_seq=tiles_per_seq,
                       mod_row=lambda i: i // tiles_per_seq)
        if need_ctx:
            attn_c = _attention_context(q, k, vt, batch, n_lat, n_ctx)
            xs_ctx = post(xs_ctx, attn=attn_c, tm=ctx_len, row0=n_lat, tiles_per_seq=1,
                          mod_row=lambda i: batch)
        xs_lat = new_lat
    return xs_lat.reshape(batch, seq, d)
```

```python
import functools

import jax
import jax.numpy as jnp
from jax import lax
from jax.experimental import pallas as pl
from jax.experimental.pallas import tpu as pltpu

F32 = jnp.float32
BF16 = jnp.bfloat16

GRID_W = 64
EPS = 1e-6
N_MOD = 6
POOL_DIM = 256
POOL_GDIM = 64
POOL_GROUPS = 4
SSD_HEADS = 4
SSD_HEAD_DIM = 64
SSD_INNER = 256
SSD_GROUPS = 2
SSD_STATE = 64
SSD_CONV = 5
SSD_CHUNK = 128
SSD_DIRS = 2
SSD_XBC = 512
ATTN_HEADS = 4
ATTN_KV_HEADS = 2
HEAD_DIM = 128
ATTN_DIM = 512
KV_DIM = 256
ROPE_THETA = 10000.0
Q_SCALE = HEAD_DIM ** -0.5 * 1.4426950408889634
SCORE_BOUND_LIMIT = 60.0
SCORE_BOUND_SLACK = 1.02

LANES = 128
SUBLANES = 8
TOKEN_TILE = 512
INPROJ_ROWS = 256
CONV_TILE = 1024
POST_ROWS = 256
ATTN_Q_TILE = 512
ATTN_K_TILE = 512
ATTN_UNROLL = 6
ATTN_BOUNDED_UNROLL = 30
BF16_SUBLANES = 16
VT_ROWS = HEAD_DIM + BF16_SUBLANES
VMEM_LIMIT = 56 * 1024 * 1024

C_POOL, C_Z, C_XBC, C_DT, C_Q, C_K, C_V, C_END = 0, 256, 512, 1024, 1152, 1664, 1920, 2176


def _silu(x):
    return x * (1.0 / (1.0 + jnp.exp(-x)))


def _softplus(x):
    return jnp.maximum(x, 0.0) + jnp.log1p(jnp.exp(-jnp.abs(x)))


def _const_spec(shape):
    nd = len(shape)
    return pl.BlockSpec(shape, lambda *_: (0,) * nd, pipeline_mode=pl.Buffered(1))


def _mod_kernel(cond_ref, w_ref, b_ref, o_ref):
    s = _silu(cond_ref[...]).astype(BF16)
    o_ref[...] = jnp.dot(s, w_ref[...].astype(BF16), preferred_element_type=F32) + b_ref[...]


def _modulation(cond, w_mod, b_mod):
    depth, d, n = w_mod.shape
    bn = d
    return pl.pallas_call(
        _mod_kernel,
        out_shape=jax.ShapeDtypeStruct((depth, SUBLANES, n), F32),
        grid=(depth, n // bn),
        in_specs=[
            pl.BlockSpec((SUBLANES, d), lambda l, j: (0, 0)),
            pl.BlockSpec((None, d, bn), lambda l, j: (l, 0, j)),
            pl.BlockSpec((None, 1, bn), lambda l, j: (l, 0, j)),
        ],
        out_specs=pl.BlockSpec((None, SUBLANES, bn), lambda l, j: (l, 0, j)),
        compiler_params=pltpu.CompilerParams(
            dimension_semantics=("arbitrary", "arbitrary"), vmem_limit_bytes=VMEM_LIMIT),
        name="modulation",
    )(cond, w_mod, b_mod.reshape(depth, 1, n))


def _rope_partner(x):
    lane = lax.broadcasted_iota(jnp.int32, x.shape, 1)
    fwd = pltpu.roll(x, LANES - 32, 1)
    bwd = pltpu.roll(x, 32, 1)
    return jnp.where((lane % 64) < 32, fwd, bwd)


def _inproj_kernel(xl_ref, xc_ref, mod_ref, n1_ref, w_ref, qw_ref, kw_ref, cos_ref, sin_ref,
                   pool_ref, z_ref, xbc_ref, dt_ref, q_ref, k_ref, vt_ref, *, n_lat_tiles):
    i = pl.program_id(0)
    is_lat = i < n_lat_tiles
    tm = xl_ref.shape[0]
    for r0 in range(0, tm, INPROJ_ROWS):
        rows = slice(r0, r0 + INPROJ_ROWS)
        x = jnp.where(is_lat, xl_ref[rows, :], xc_ref[rows, :])
        ms = jnp.mean(x * x, axis=-1, keepdims=True)
        h = x * lax.rsqrt(ms + EPS) * n1_ref[...]
        h = h * (1.0 + mod_ref[1:2, :]) + mod_ref[0:1, :]
        u = jnp.dot(h.astype(BF16), w_ref[...], preferred_element_type=F32)
        pool_ref[rows, :] = u[:, C_POOL:C_Z]
        z_ref[rows, :] = u[:, C_Z:C_XBC]
        xbc_ref[rows, :] = u[:, C_XBC:C_DT]
        dt_ref[rows, :] = u[:, C_DT:C_Q]
        vt = u[:, C_V:C_END].T.astype(BF16)
        for hd in range(ATTN_KV_HEADS):
            vt_ref[hd * VT_ROWS:hd * VT_ROWS + HEAD_DIM, rows] = vt[hd * HEAD_DIM:(hd + 1) * HEAD_DIM]
            vt_ref[hd * VT_ROWS + HEAD_DIM:(hd + 1) * VT_ROWS, rows] = jnp.ones(
                (VT_ROWS - HEAD_DIM, INPROJ_ROWS), BF16)
        cos = cos_ref[rows, :]
        sin = sin_ref[rows, :]

        def norm_rope(t, w):
            t = t * lax.rsqrt(jnp.mean(t * t, axis=-1, keepdims=True) + EPS) * w
            return t * cos + _rope_partner(t) * sin

        for hd in range(ATTN_HEADS):
            t = norm_rope(u[:, C_Q + hd * HEAD_DIM:C_Q + (hd + 1) * HEAD_DIM], qw_ref[...])
            q_ref[rows, hd * HEAD_DIM:(hd + 1) * HEAD_DIM] = (t * Q_SCALE).astype(BF16)
        for hd in range(ATTN_KV_HEADS):
            t = norm_rope(u[:, C_K + hd * HEAD_DIM:C_K + (hd + 1) * HEAD_DIM], kw_ref[...])
            k_ref[rows, hd * HEAD_DIM:(hd + 1) * HEAD_DIM] = t.astype(BF16)


def _inproj(xs_lat, xs_ctx, mod, n1, w_cat, qw, kw, cos_tab, sin_tab, batch):
    n_lat, d = xs_lat.shape
    n_ctx = xs_ctx.shape[0]
    tm = TOKEN_TILE
    assert n_ctx == tm and n_lat % (batch * tm) == 0
    n_lat_tiles = n_lat // tm
    tiles_per_seq = n_lat_tiles // batch
    n_tok = n_lat + n_ctx
    grid = (n_lat_tiles + 1,)

    def lat_idx(i):
        return jnp.minimum(i, n_lat_tiles - 1)

    def rope_idx(i):
        return jnp.where(i < n_lat_tiles, i % tiles_per_seq, tiles_per_seq)

    def mod_idx(i):
        return jnp.where(i < n_lat_tiles, i // tiles_per_seq, batch)

    row = lambda w: pl.BlockSpec((tm, w), lambda i: (i, 0))
    outs = [(POOL_DIM, F32), (SSD_INNER, F32), (SSD_XBC, F32), (LANES, F32),
            (ATTN_DIM, BF16), (KV_DIM, BF16)]
    return pl.pallas_call(
        functools.partial(_inproj_kernel, n_lat_tiles=n_lat_tiles),
        out_shape=[jax.ShapeDtypeStruct((n_tok, w), t) for w, t in outs]
        + [jax.ShapeDtypeStruct((n_tok // tm, ATTN_KV_HEADS * VT_ROWS, tm), BF16)],
        grid=grid,
        in_specs=[
            pl.BlockSpec((tm, d), lambda i: (lat_idx(i), 0)),
            pl.BlockSpec((tm, d), lambda i: (0, 0)),
            pl.BlockSpec((None, N_MOD, d), lambda i: (mod_idx(i), 0, 0)),
            _const_spec((1, d)),
            _const_spec(w_cat.shape),
            _const_spec((1, HEAD_DIM)),
            _const_spec((1, HEAD_DIM)),
            pl.BlockSpec((tm, HEAD_DIM), lambda i: (rope_idx(i), 0)),
            pl.BlockSpec((tm, HEAD_DIM), lambda i: (rope_idx(i), 0)),
        ],
        out_specs=[row(w) for w, _ in outs]
        + [pl.BlockSpec((None, ATTN_KV_HEADS * VT_ROWS, tm), lambda i: (i, 0, 0))],
        compiler_params=pltpu.CompilerParams(
            dimension_semantics=("arbitrary",), vmem_limit_bytes=VMEM_LIMIT),
        name="inproj",
    )(xs_lat, xs_ctx, mod, n1, w_cat, qw, kw, cos_tab, sin_tab)


def _conv_kernel(x_ref, prev_ref, next_ref, cw_ref, cb_ref, *rest, tiles_per_seq):
    o_ref, buf_ref = rest[-2:]
    i = pl.program_id(0)
    tm = x_ref.shape[0]
    first = i % tiles_per_seq == 0
    last = i % tiles_per_seq == tiles_per_seq - 1
    buf_ref[0:SUBLANES, :] = jnp.where(first, 0.0, prev_ref[...])
    buf_ref[SUBLANES:SUBLANES + tm, :] = x_ref[...]
    buf_ref[SUBLANES + tm:2 * SUBLANES + tm, :] = jnp.where(last, 0.0, next_ref[...])
    u = buf_ref[...]
    n = u.shape[0]
    pad = SSD_CONV // 2
    acc = cb_ref[...] + cw_ref[pad:pad + 1, :] * u[SUBLANES:SUBLANES + tm]
    for k in range(SSD_CONV):
        if k != pad:
            shifted = pltpu.roll(u, (pad - k) % n, 0)
            acc = acc + cw_ref[k:k + 1, :] * shifted[SUBLANES:SUBLANES + tm]
    o_ref[...] = _silu(acc)


def _ssd_conv_rows(xbc, cw, cb, into, *, tm, row0, n_rows, seq_len):
    n_tok = xbc.shape[0]
    off = row0 // tm
    blk8 = tm // SUBLANES
    last8 = n_tok // SUBLANES - 1
    in_specs = [
        pl.BlockSpec((tm, SSD_XBC), lambda i: (off + i, 0)),
        pl.BlockSpec((SUBLANES, SSD_XBC), lambda i: (jnp.maximum((off + i) * blk8 - 1, 0), 0)),
        pl.BlockSpec((SUBLANES, SSD_XBC), lambda i: (jnp.minimum((off + i) * blk8 + blk8, last8), 0)),
        _const_spec((SUBLANES, SSD_XBC)), _const_spec((1, SSD_XBC)),
    ]
    args = [xbc, xbc, xbc, cw, cb]
    aliases = {}
    if into is not None:
        in_specs.append(pl.BlockSpec(memory_space=pl.ANY))
        args.append(into)
        aliases = {len(args) - 1: 0}
    return pl.pallas_call(
        functools.partial(_conv_kernel, tiles_per_seq=seq_len // tm),
        out_shape=jax.ShapeDtypeStruct((n_tok, SSD_XBC), F32),
        grid=(n_rows // tm,),
        in_specs=in_specs,
        out_specs=pl.BlockSpec((tm, SSD_XBC), lambda i: (off + i, 0)),
        scratch_shapes=[pltpu.VMEM((tm + 2 * SUBLANES, SSD_XBC), F32)],
        input_output_aliases=aliases,
        compiler_params=pltpu.CompilerParams(
            dimension_semantics=("arbitrary",), vmem_limit_bytes=VMEM_LIMIT),
        name="ssd_conv",
    )(*args)


def _ssd_conv(xbc, conv_w, conv_b, batch, n_lat, n_ctx):
    seq, ctx = n_lat // batch, n_ctx // batch
    assert seq % CONV_TILE == 0 and n_lat % ctx == 0
    cw = jnp.zeros((SUBLANES, SSD_XBC), F32).at[:SSD_CONV].set(conv_w)
    cb = conv_b.reshape(1, -1)
    xc = _ssd_conv_rows(xbc, cw, cb, None, tm=CONV_TILE, row0=0, n_rows=n_lat, seq_len=seq)
    return _ssd_conv_rows(xbc, cw, cb, xc, tm=ctx, row0=n_lat, n_rows=n_ctx, seq_len=ctx)


def _ssd_kernel(xf_ref, dtf_ref, dtf_next_ref, xb_ref, dtb_ref, dtb_next_ref,
                bias_ref, alog_ref, dsk_ref, spread_ref, ecol_ref,
                yf_ref, yb_ref, h_ref, et_sc, spread_sc, ecols_sc):
    i = pl.program_id(1)
    q = SSD_CHUNK

    @pl.when(i == 0)
    def _():
        h_ref[...] = jnp.zeros_like(h_ref)

    row = lax.broadcasted_iota(jnp.int32, (q, q), 0)
    col = lax.broadcasted_iota(jnp.int32, (q, q), 1)
    n_col = SSD_DIRS * SSD_HEADS
    heads_per_group = SSD_HEADS // SSD_GROUPS
    pair = heads_per_group * SSD_HEAD_DIM
    lane_half = col // SSD_HEAD_DIM
    st_row = lax.broadcasted_iota(jnp.int32, (SSD_GROUPS * SSD_STATE, SSD_INNER), 0)
    st_col = lax.broadcasted_iota(jnp.int32, (SSD_GROUPS * SSD_STATE, SSD_INNER), 1)
    same_group = (st_row // SSD_STATE) == (st_col // pair)
    contract0 = (((0,), (0,)), ((), ()))

    def bf16_terms(v, n_terms):
        terms, rest = [], v
        for _ in range(n_terms):
            t = rest.astype(BF16).astype(F32)
            terms.append(t)
            rest = rest - t
        rows = -(-n_terms * n_col // BF16_SUBLANES) * BF16_SUBLANES
        if rows > n_terms * n_col:
            terms.append(jnp.zeros((rows - n_terms * n_col, v.shape[1]), F32))
        return jnp.concatenate(terms, axis=0).astype(BF16)

    fwd_rows = lax.broadcasted_iota(jnp.int32, (n_col, q), 0) < SSD_HEADS

    def token_decay_sums(dtf_blk_ref, dtb_blk_ref):
        dt_raw = jnp.where(fwd_rows, dtf_blk_ref[...].T[0:n_col], dtb_blk_ref[...].T[0:n_col])
        dt = _softplus(dt_raw + bias_ref[...])
        a = dt * -jnp.exp(alog_ref[...])
        tri_u = (row <= col).astype(F32)
        cs = jnp.dot(a, tri_u, preferred_element_type=F32, precision=lax.Precision.HIGHEST)
        return dt, a, cs

    def token_scalars(dt, a, cs):
        tot = cs[:, q - 1:q]
        e_t = jnp.where(fwd_rows, cs, cs - a)
        grow, shrink = jnp.exp(e_t), jnp.exp(tot - e_t)
        w_in = jnp.where(fwd_rows, grow, shrink)
        w_out = jnp.where(fwd_rows, shrink, grow)
        terms = jnp.concatenate(
            [bf16_terms(dt, 2), bf16_terms(w_in, 2), bf16_terms(dt * w_out, 2)], axis=1)
        spread = lax.dot_general(terms, spread_ref[...], contract0,
                                 preferred_element_type=F32)
        e_cols = lax.dot_general(bf16_terms(e_t, 3), ecol_ref[...], contract0,
                                 preferred_element_type=F32)
        return e_t, spread, e_cols

    def into_scratch(vals):
        et_sc[...], spread_sc[...], ecols_sc[...] = vals

    @pl.when(i == 0)
    def _():
        into_scratch(token_scalars(*token_decay_sums(dtf_ref, dtb_ref)))

    next_sums = token_decay_sums(dtf_next_ref, dtb_next_ref)

    zero = jnp.zeros((), BF16)

    def input_matmuls(d, xc):
        ch = slice(d * SSD_INNER, (d + 1) * SSD_INNER)
        dt_x, w_in_x, dtw_x = spread_sc[0:q, ch], spread_sc[q:2 * q, ch], spread_sc[2 * q:3 * q, ch]
        x = xc[:, 0:SSD_INNER]
        xdt = (x * dt_x).astype(BF16)
        xdtw = (x * dtw_x).astype(BF16)
        b_all = xc[:, SSD_INNER:SSD_INNER + SSD_GROUPS * SSD_STATE].astype(BF16)
        c_all = xc[:, SSD_INNER + SSD_GROUPS * SSD_STATE:].astype(BF16)
        hs = h_ref[d]
        y_off = jnp.dot(c_all, hs.astype(BF16), preferred_element_type=F32) * w_in_x
        cbs = [lax.dot_general(jnp.where(lane_half == g, c_all, zero), b_all,
                               (((1,), (1,)), ((), ())), preferred_element_type=F32)
               for g in range(SSD_GROUPS)]
        upd = lax.dot_general(b_all, xdtw, contract0, preferred_element_type=F32)
        exp_tot = w_in_x[q - 1:q, :] if d == 0 else w_in_x[0:1, :]
        h_ref[d] = exp_tot * hs + jnp.where(same_group, upd, 0.0)
        return x, xdt, y_off, cbs

    def decay_matmuls(d, x, xdt, y_off, cbs, y_ref):
        for g in range(SSD_GROUPS):
            sl = slice(g * pair, (g + 1) * pair)
            y_g = y_off[:, sl] + dsk_ref[d:d + 1, sl] * x[:, sl]
            for hh in range(heads_per_group):
                c = d * SSD_HEADS + g * heads_per_group + hh
                e_col = ecols_sc[:, c * q:(c + 1) * q]
                e_row = et_sc[c:c + 1, :]
                if d == 0:
                    lmat = jnp.where(row >= col, jnp.exp(e_col - e_row), 0.0)
                else:
                    lmat = jnp.where(col >= row, jnp.exp(e_row - e_col), 0.0)
                x_h = jnp.where(lane_half == hh, xdt[:, sl], zero)
                y_g = y_g + jnp.dot((cbs[g] * lmat).astype(BF16), x_h,
                                    preferred_element_type=F32)
            y_ref[:, sl] = y_g

    lin_f = input_matmuls(0, xf_ref)
    lin_b = input_matmuls(1, xb_ref)
    next_scalars = token_scalars(*next_sums)
    decay_matmuls(0, *lin_f, yf_ref)
    decay_matmuls(1, *lin_b, yb_ref)
    into_scratch(next_scalars)


def _ssd(xc, dt_raw, dt_bias, a_log, d_skip, batch, n_lat, n_ctx):
    n_tok = xc.shape[0]
    q = SSD_CHUNK
    lat_ch = n_lat // batch // q
    ctx_ch = n_ctx // batch // q
    n_ch = lat_ch + ctx_ch

    def fwd(b, i):
        return jnp.where(i < ctx_ch, batch * lat_ch + b * ctx_ch + i, b * lat_ch + i - ctx_ch)

    def bwd(b, i):
        return jnp.where(i < ctx_ch, batch * lat_ch + b * ctx_ch + ctx_ch - 1 - i,
                         b * lat_ch + n_ch - 1 - i)

    def specs(ch):
        return [
            pl.BlockSpec((q, SSD_XBC), lambda b, i: (ch(b, i), 0)),
            pl.BlockSpec((q, LANES), lambda b, i: (ch(b, i), 0)),
            pl.BlockSpec((q, LANES), lambda b, i: (ch(b, jnp.minimum(i + 1, n_ch - 1)), 0)),
        ]

    n_col = SSD_DIRS * SSD_HEADS
    rows = lambda v: jnp.broadcast_to(v.reshape(n_col, 1), (n_col, q))
    dsk = jnp.repeat(d_skip, SSD_HEAD_DIM, axis=1)
    term_row = jnp.arange(2 * BF16_SUBLANES) % n_col
    live3 = jnp.arange(2 * BF16_SUBLANES) < 3 * n_col
    col_of_channel = jnp.arange(SSD_DIRS * SSD_INNER) // SSD_HEAD_DIM
    col_of_block = jnp.arange(n_col * q) // q
    spread = (term_row[:BF16_SUBLANES, None] == col_of_channel[None, :]).astype(BF16)
    ecol = ((term_row[:, None] == col_of_block[None, :]) & live3[:, None]).astype(BF16)
    return pl.pallas_call(
        _ssd_kernel,
        out_shape=[jax.ShapeDtypeStruct((n_tok, SSD_INNER), F32)] * 2,
        grid=(batch, n_ch),
        in_specs=specs(fwd) + specs(bwd) + [
            _const_spec((n_col, q)), _const_spec((n_col, q)), _const_spec((SSD_DIRS, SSD_INNER)),
            _const_spec(spread.shape), _const_spec(ecol.shape),
        ],
        out_specs=[pl.BlockSpec((q, SSD_INNER), lambda b, i: (fwd(b, i), 0)),
                   pl.BlockSpec((q, SSD_INNER), lambda b, i: (bwd(b, i), 0))],
        scratch_shapes=[pltpu.VMEM((SSD_DIRS, SSD_GROUPS * SSD_STATE, SSD_INNER), F32),
                        pltpu.VMEM((n_col, q), F32),
                        pltpu.VMEM((3 * q, SSD_DIRS * SSD_INNER), F32),
                        pltpu.VMEM((q, n_col * q), F32)],
        compiler_params=pltpu.CompilerParams(
            dimension_semantics=("arbitrary", "arbitrary"), vmem_limit_bytes=VMEM_LIMIT),
        name="ssd_scan",
    )(xc, dt_raw, dt_raw, xc, dt_raw, dt_raw,
      rows(dt_bias), rows(a_log), dsk, spread, ecol)


def _steps_per_trip(n_steps, preferred):
    return max(u for u in range(2, preferred + 1, 2) if n_steps % u == 0) if n_steps else 2


def _scores_t(k, q2):
    return lax.dot_general(k, q2, (((1,), (1,)), ((), ())), preferred_element_type=F32)


def _softmax_stage(s, m_blk, m):
    if m is None:
        return m_blk, None, jnp.exp2((s - m_blk).astype(BF16))
    m_new = jnp.maximum(m, m_blk)
    return m_new, jnp.exp2(m - m_new), jnp.exp2((s - m_new).astype(BF16))


def _attn_kernel(q_ref, kc_ref, vtc_ref, kx_ref, vtx_ref, o_ref, s0_ref, s1_ref, p0_ref, p1_ref,
                 *, n_chunks):
    tq = q_ref.shape[0]
    tk = ATTN_K_TILE
    q2 = jnp.concatenate([q_ref[:, 0:HEAD_DIM], q_ref[:, HEAD_DIM:2 * HEAD_DIM]], axis=0)
    s_refs, p_refs = (s0_ref, s1_ref), (p0_ref, p1_ref)

    def scores_into(slot, c):
        s = _scores_t(kx_ref[pl.ds(pl.multiple_of(c * tk, tk), tk), :], q2)
        s_refs[slot][...] = s
        return jnp.max(s, axis=0, keepdims=True)

    def softmax_into(slot, m_blk, m):
        m, alpha, p = _softmax_stage(s_refs[slot][...], m_blk, m)
        p_refs[slot][...] = p
        return m, alpha

    def step(c, par, carry, with_scores, with_softmax):
        m, acc, alpha, m_blk = carry
        m_blk_next, alpha_next = m_blk, alpha
        if with_scores:
            m_blk_next = scores_into(par, c + 2)
        if with_softmax:
            m, alpha_next = softmax_into(1 - par, m_blk, m)
        acc = alpha * acc + jnp.dot(vtx_ref[c], p_refs[par][...], preferred_element_type=F32)
        return m, acc, alpha_next, m_blk_next

    if n_chunks:
        m_blk0 = scores_into(0, 0)
        m_blk1 = scores_into(1, 1)
    s_ctx = _scores_t(kc_ref[...], q2)
    m, _, p_ctx = _softmax_stage(s_ctx, jnp.max(s_ctx, axis=0, keepdims=True), None)
    acc = jnp.dot(vtc_ref[...], p_ctx, preferred_element_type=F32)
    if n_chunks:
        m, alpha = softmax_into(0, m_blk0, m)
        carry = (m, acc, alpha, m_blk1)
        unroll = _steps_per_trip(n_chunks - 2, ATTN_UNROLL)

        def body(i, carry):
            for u in range(unroll):
                carry = step(unroll * i + u, u % 2, carry, True, True)
            return carry

        carry = lax.fori_loop(0, (n_chunks - 2) // unroll, body, carry)
        carry = step(n_chunks - 2, 0, carry, False, True)
        _, acc, _, _ = step(n_chunks - 1, 1, carry, False, False)
    o = (acc[0:HEAD_DIM] / acc[HEAD_DIM:HEAD_DIM + 1]).T.astype(o_ref.dtype)
    o_ref[:, 0:HEAD_DIM] = o[0:tq]
    o_ref[:, HEAD_DIM:2 * HEAD_DIM] = o[tq:2 * tq]


def _attn_bounded_kernel(q_ref, kc_ref, vtc_ref, kx_ref, vtx_ref, o_ref, p0_ref, p1_ref, *, n_chunks):
    tq = q_ref.shape[0]
    tk = ATTN_K_TILE
    q2 = jnp.concatenate([q_ref[:, 0:HEAD_DIM], q_ref[:, HEAD_DIM:2 * HEAD_DIM]], axis=0)
    p_refs = (p0_ref, p1_ref)

    def colsum(p):
        return jnp.sum(p.reshape(p.shape[0] // SUBLANES, SUBLANES, p.shape[1]), axis=0)

    def weights_into(slot, c):
        p = jnp.exp2(_scores_t(kx_ref[pl.ds(pl.multiple_of(c * tk, tk), tk), :], q2))
        p_refs[slot][...] = p.astype(BF16)
        return colsum(p)

    def step(c, par, carry, with_scores):
        l8, acc = carry
        if with_scores:
            l8 = l8 + weights_into(1 - par, c + 1)
        acc = acc + jnp.dot(vtx_ref[c, 0:HEAD_DIM, :], p_refs[par][...],
                            preferred_element_type=F32)
        return l8, acc

    p_ctx = jnp.exp2(_scores_t(kc_ref[...], q2))
    l8 = colsum(p_ctx) + weights_into(0, 0)
    acc = jnp.dot(vtc_ref[0:HEAD_DIM, :], p_ctx.astype(BF16), preferred_element_type=F32)
    unroll = _steps_per_trip(n_chunks - 2, ATTN_BOUNDED_UNROLL)

    def body(i, carry):
        for u in range(unroll):
            carry = step(unroll * i + u, u % 2, carry, True)
        return carry

    carry = lax.fori_loop(0, (n_chunks - 2) // unroll, body, (l8, acc))
    carry = step(n_chunks - 2, 0, carry, True)
    l8, acc = step(n_chunks - 1, 1, carry, False)
    o = (acc / jnp.sum(l8, axis=0, keepdims=True)).T.astype(o_ref.dtype)
    o_ref[:, 0:HEAD_DIM] = o[0:tq]
    o_ref[:, HEAD_DIM:2 * HEAD_DIM] = o[tq:2 * tq]


def _attention_latent(q, k, vt, batch, n_lat, n_ctx, bounded):
    seq = n_lat // batch
    ctx = n_ctx // batch
    tq, tk = ATTN_Q_TILE, ATTN_K_TILE
    nq = seq // tq
    n_chunks = seq // tk
    assert seq % (2 * tk) == 0 and n_lat % ctx == 0 and n_ctx == tk and vt.shape[2] == tk
    r = 2 * tq
    body = _attn_bounded_kernel if bounded else _attn_kernel
    p_bufs = [pltpu.VMEM((tk, r), BF16), pltpu.VMEM((tk, r), BF16)]
    s_bufs = [] if bounded else [pltpu.VMEM((tk, r), F32), pltpu.VMEM((tk, r), F32)]
    return pl.pallas_call(
        functools.partial(body, n_chunks=n_chunks),
        out_shape=jax.ShapeDtypeStruct((n_lat, ATTN_DIM), BF16),
        grid=(batch, ATTN_KV_HEADS, nq),
        in_specs=[pl.BlockSpec((tq, 2 * HEAD_DIM), lambda b, g, j: (b * nq + j, g)),
                  pl.BlockSpec((ctx, HEAD_DIM), lambda b, g, j: (n_lat // ctx + b, g)),
                  pl.BlockSpec((None, VT_ROWS, ctx), lambda b, g, j: (n_lat // tk, g, b)),
                  pl.BlockSpec((seq, HEAD_DIM), lambda b, g, j: (b, g)),
                  pl.BlockSpec((n_chunks, VT_ROWS, tk), lambda b, g, j: (b, g, 0))],
        out_specs=pl.BlockSpec((tq, 2 * HEAD_DIM), lambda b, g, j: (b * nq + j, g)),
        scratch_shapes=s_bufs + p_bufs,
        compiler_params=pltpu.CompilerParams(
            dimension_semantics=("arbitrary",) * 3, vmem_limit_bytes=VMEM_LIMIT),
        name="attn_latent_bounded" if bounded else "attn_latent",
    )(q, k, vt, k, vt)


def _attn_ctx_kernel(q_ref, kc_ref, vtc_ref, o_ref):
    _attn_kernel(q_ref, kc_ref, vtc_ref, None, None, o_ref, None, None, None, None, n_chunks=0)


def _attention_context(q, k, vt, batch, n_lat, n_ctx):
    ctx = n_ctx // batch
    tk = ATTN_K_TILE
    return pl.pallas_call(
        _attn_ctx_kernel,
        out_shape=jax.ShapeDtypeStruct((n_ctx, ATTN_DIM), BF16),
        grid=(batch, ATTN_KV_HEADS),
        in_specs=[pl.BlockSpec((ctx, 2 * HEAD_DIM), lambda b, g: (n_lat // ctx + b, g)),
                  pl.BlockSpec((ctx, HEAD_DIM), lambda b, g: (n_lat // ctx + b, g)),
                  pl.BlockSpec((None, VT_ROWS, ctx), lambda b, g: (n_lat // tk, g, b))],
        out_specs=pl.BlockSpec((ctx, 2 * HEAD_DIM), lambda b, g: (b, g)),
        compiler_params=pltpu.CompilerParams(
            dimension_semantics=("arbitrary",) * 2, vmem_limit_bytes=VMEM_LIMIT),
        name="attn_context",
    )(q, k, vt)


def _post_kernel(xs_ref, up_ref, upp_ref, upn_ref, z_ref, yf_ref, yb_ref, at_ref, mod_ref,
                 pw_ref, ps_ref, sw_ref, wo_ref, n2_ref, w1_ref, w3_ref, w2_ref,
                 o_ref, buf_ref, *, tiles_per_seq, seq_len):
    i = pl.program_id(0)
    tm = xs_ref.shape[0]
    halo = SUBLANES
    j = i % tiles_per_seq

    buf_ref[0:halo, :] = jnp.where(j == 0, 0.0, upp_ref[...])
    buf_ref[halo:halo + tm, :] = up_ref[...]
    buf_ref[halo + tm:2 * halo + tm, :] = jnp.where(j == tiles_per_seq - 1, 0.0, upn_ref[...])
    ub = buf_ref[...]
    n = tm + 2 * halo
    s2 = ub + pltpu.roll(ub, 1, 0)
    s4 = pltpu.roll(s2, 1, 0) + pltpu.roll(s2, n - 1, 0)
    s8 = pltpu.roll(s4, 2, 0) + pltpu.roll(s4, n - 2, 0)
    s16 = pltpu.roll(s8, 4, 0) + pltpu.roll(s8, n - 4, 0)
    shape = (tm, POOL_DIM)
    grp = lax.broadcasted_iota(jnp.int32, shape, 1) // POOL_GDIM
    t = lax.broadcasted_iota(jnp.int32, shape, 0) + j * tm
    half = jnp.left_shift(1, grp)
    cnt = (jnp.minimum(t + half, seq_len) - jnp.maximum(t - half, 0)).astype(F32)
    sl = slice(halo, halo + tm)
    wsum = jnp.where(grp == 0, s2[sl], jnp.where(grp == 1, s4[sl],
                                                 jnp.where(grp == 2, s8[sl], s16[sl])))
    pooled = wsum / cnt - ub[sl]
    pool = jnp.dot(pooled.astype(BF16), pw_ref[...], preferred_element_type=F32) * ps_ref[...]

    gy = (yf_ref[...] + yb_ref[...]) * _silu(z_ref[...])
    gw = SSD_INNER // SSD_GROUPS
    parts = []
    for g in range(SSD_GROUPS):
        t_g = gy[:, g * gw:(g + 1) * gw]
        parts.append(t_g * lax.rsqrt(jnp.mean(t_g * t_g, axis=-1, keepdims=True) + EPS))
    ssd = jnp.concatenate(parts, axis=1) * sw_ref[...]

    mix = jnp.concatenate([pool.astype(BF16), ssd.astype(BF16), at_ref[...]], axis=1)
    blocks = [slice(r0, r0 + min(POST_ROWS, tm)) for r0 in range(0, tm, POST_ROWS)]
    x1 = [xs_ref[rows, :] + mod_ref[2:3, :] * jnp.dot(mix[rows], wo_ref[...],
                                                       preferred_element_type=F32)
          for rows in blocks]

    def modulated_norm(v):
        h = v * lax.rsqrt(jnp.mean(v * v, axis=-1, keepdims=True) + EPS) * n2_ref[...]
        return (h * (1.0 + mod_ref[4:5, :]) + mod_ref[3:4, :]).astype(BF16)

    gated = []
    for v in x1:
        h = modulated_norm(v)
        a = jnp.dot(h, w1_ref[...], preferred_element_type=F32)
        b = jnp.dot(h, w3_ref[...], preferred_element_type=F32)
        gated.append((_silu(a) * b).astype(BF16))
    for rows, v, gt in zip(blocks, x1, gated):
        ff = jnp.dot(gt, w2_ref[...], preferred_element_type=F32)
        o_ref[rows, :] = v + mod_ref[5:6, :] * ff


def _post(xs, upool, z, yf, yb, attn, mod, pw_bd, pscale, ssd_w, w_out, n2, w1, w3, w2,
          *, tm, row0, tiles_per_seq, mod_row):
    n_rows, d = xs.shape
    n_tiles = n_rows // tm
    off = row0 // tm
    blk8 = tm // SUBLANES
    last8 = upool.shape[0] // SUBLANES - 1
    tok = lambda w: pl.BlockSpec((tm, w), lambda i: (off + i, 0))
    loc = lambda w: pl.BlockSpec((tm, w), lambda i: (i, 0))
    return pl.pallas_call(
        functools.partial(_post_kernel, tiles_per_seq=tiles_per_seq, seq_len=tiles_per_seq * tm),
        out_shape=jax.ShapeDtypeStruct((n_rows, d), F32),
        grid=(n_tiles,),
        in_specs=[
            loc(d),
            tok(POOL_DIM),
            pl.BlockSpec((SUBLANES, POOL_DIM), lambda i: (jnp.maximum((off + i) * blk8 - 1, 0), 0)),
            pl.BlockSpec((SUBLANES, POOL_DIM),
                         lambda i: (jnp.minimum((off + i) * blk8 + blk8, last8), 0)),
            tok(SSD_INNER), tok(SSD_INNER), tok(SSD_INNER),
            loc(ATTN_DIM),
            pl.BlockSpec((None, N_MOD, d), lambda i: (mod_row(i), 0, 0)),
            _const_spec(pw_bd.shape), _const_spec((1, POOL_DIM)), _const_spec((1, SSD_INNER)),
            _const_spec(w_out.shape), _const_spec((1, d)),
            _const_spec(w1.shape), _const_spec(w3.shape), _const_spec(w2.shape),
        ],
        out_specs=loc(d),
        scratch_shapes=[pltpu.VMEM((tm + 2 * SUBLANES, POOL_DIM), F32)],
        compiler_params=pltpu.CompilerParams(
            dimension_semantics=("arbitrary",), vmem_limit_bytes=VMEM_LIMIT),
        name="post_mix_ffn",
    )(xs, upool, upool, upool, z, yf, yb, attn, mod, pw_bd, pscale, ssd_w, w_out, n2, w1, w3, w2)


def _rope_tables(seq, extra_rows):
    rows, pad_rows = seq // GRID_W, extra_rows // GRID_W
    axis_dim = HEAD_DIM // 2
    inv_freq = ROPE_THETA ** (-jnp.arange(0, axis_dim, 2, dtype=F32) / axis_dim)
    ang_r = jnp.arange(rows, dtype=F32)[:, None] * inv_freq[None, :]
    ang_c = jnp.arange(GRID_W, dtype=F32)[:, None] * inv_freq[None, :]
    live = (jnp.arange(rows + pad_rows) < rows)[:, None, None]
    shape = (rows + pad_rows, GRID_W, axis_dim // 2)

    def per_row(t, fill):
        t = jnp.concatenate([t, jnp.full((pad_rows, t.shape[1]), fill, F32)], axis=0)
        return jnp.broadcast_to(t[:, None, :], shape)

    def per_col(t, fill):
        return jnp.where(live, jnp.broadcast_to(t[None, :, :], shape), fill)

    cr, sr = per_row(jnp.cos(ang_r), 1.0), per_row(jnp.sin(ang_r), 0.0)
    cc, sc = per_col(jnp.cos(ang_c), 1.0), per_col(jnp.sin(ang_c), 0.0)
    cos = jnp.concatenate([cr, cr, cc, cc], axis=-1).reshape(-1, HEAD_DIM)
    sin = jnp.concatenate([-sr, sr, -sc, sc], axis=-1).reshape(-1, HEAD_DIM)
    return cos, sin


def _fuse_w_in(w_in):
    d = w_in.shape[0]
    o_dt = POOL_DIM + SSD_INNER + SSD_XBC
    o_att = o_dt + SSD_DIRS * SSD_HEADS
    dt_pad = jnp.zeros((d, C_Q - C_DT - SSD_DIRS * SSD_HEADS), w_in.dtype)
    return jnp.concatenate([w_in[:, :o_dt], w_in[:, o_dt:o_att], dt_pad, w_in[:, o_att:]],
                           axis=1).astype(BF16)


def kernel(x, c, ctx, c_ctx, norm1_w, norm2_w, w_mod, b_mod, w_in, pool_w, pool_scale, conv_w, conv_b,
           dt_bias, a_log, d_skip, ssd_norm_w, q_norm_w, k_norm_w, w_out, w1, w3, w2):
    batch, seq, d = x.shape
    ctx_len = ctx.shape[1]
    depth = w_mod.shape[0]
    n_lat, n_ctx = batch * seq, batch * ctx_len
    assert batch + 1 <= SUBLANES and seq % TOKEN_TILE == 0 and seq % GRID_W == 0

    cond = jnp.zeros((SUBLANES, d), F32).at[:batch].set(c).at[batch].set(c_ctx)
    mod_all = _modulation(cond, w_mod, b_mod).reshape(depth, SUBLANES, N_MOD, d)
    cos_tab, sin_tab = _rope_tables(seq, TOKEN_TILE)

    xs_lat = x.reshape(n_lat, d)
    xs_ctx = ctx.reshape(n_ctx, d)
    tiles_per_seq = seq // TOKEN_TILE
    for layer in range(depth):
        need_ctx = layer < depth - 1
        mod = mod_all[layer]
        w_cat = _fuse_w_in(w_in[layer])
        upool, z, xbc, dt_raw, q, k, vt = _inproj(
            xs_lat, xs_ctx, mod, norm1_w[layer].reshape(1, d), w_cat,
            q_norm_w[layer].reshape(1, -1), k_norm_w[layer].reshape(1, -1), cos_tab, sin_tab, batch)
        xc = _ssd_conv(xbc, conv_w[layer], conv_b[layer], batch, n_lat, n_ctx)
        yf, yb = _ssd(xc, dt_raw, dt_bias[layer], a_log[layer], d_skip[layer], batch, n_lat, n_ctx)
        score_bound = (HEAD_DIM * Q_SCALE * SCORE_BOUND_SLACK * jnp.max(jnp.abs(q_norm_w[layer]))
                       * jnp.max(jnp.abs(k_norm_w[layer])))
        attn_x = lax.cond(
            score_bound < SCORE_BOUND_LIMIT,
            functools.partial(_attention_latent, batch=batch, n_lat=n_lat, n_ctx=n_ctx, bounded=True),
            functools.partial(_attention_latent, batch=batch, n_lat=n_lat, n_ctx=n_ctx, bounded=False),
            q, k, vt)
        pw_bd = jax.scipy.linalg.block_diag(*[pool_w[layer, g] for g in range(POOL_GROUPS)]).astype(BF16)
        post = functools.partial(
            _post, upool=upool, z=z, yf=yf, yb=yb, mod=mod, pw_bd=pw_bd,
            pscale=pool_scale[layer].reshape(1, -1), ssd_w=ssd_norm_w[layer].reshape(1, -1),
            w_out=w_out[layer].astype(BF16), n2=norm2_w[layer].reshape(1, d),
            w1=w1[layer].astype(BF16), w3=w3[layer].astype(BF16), w2=w2[layer].astype(BF16))
        new_lat = post(xs_lat, attn=attn_x, tm=TOKEN_TILE, row0=0, tiles_per_seq=tiles_per_seq,
                       mod_row=lambda i: i // tiles_per_seq)
        if need_ctx:
            attn_c = _attention_context(q, k, vt, batch, n_lat, n_ctx)
            xs_ctx = post(xs_ctx, attn=attn_c, tm=ctx_len, row0=n_lat, tiles_per_seq=1,
                          mod_row=lambda i: batch)
        xs_lat = new_lat
    return xs_lat.reshape(batch, seq, d)
```

```python
import functools

import jax
import jax.numpy as jnp
from jax import lax
from jax.experimental import pallas as pl
from jax.experimental.pallas import tpu as pltpu

F32 = jnp.float32
BF16 = jnp.bfloat16

GRID_W = 64
EPS = 1e-6
N_MOD = 6
POOL_DIM = 256
POOL_GDIM = 64
POOL_GROUPS = 4
SSD_HEADS = 4
SSD_HEAD_DIM = 64
SSD_INNER = 256
SSD_GROUPS = 2
SSD_STATE = 64
SSD_CONV = 5
SSD_CHUNK = 128
SSD_DIRS = 2
SSD_XBC = 512
ATTN_HEADS = 4
ATTN_KV_HEADS = 2
HEAD_DIM = 128
ATTN_DIM = 512
KV_DIM = 256
ROPE_THETA = 10000.0
Q_SCALE = HEAD_DIM ** -0.5 * 1.4426950408889634
SCORE_BOUND_LIMIT = 60.0
SCORE_BOUND_SLACK = 1.02

LANES = 128
SUBLANES = 8
TOKEN_TILE = 512
INPROJ_ROWS = 256
CONV_TILE = 1024
POST_ROWS = 256
ATTN_Q_TILE = 512
ATTN_K_TILE = 512
ATTN_UNROLL = 6
ATTN_BOUNDED_UNROLL = 30
BF16_SUBLANES = 16
VT_ROWS = HEAD_DIM + BF16_SUBLANES
VMEM_LIMIT = 56 * 1024 * 1024

C_POOL, C_Z, C_XBC, C_DT, C_Q, C_K, C_V, C_END = 0, 256, 512, 1024, 1152, 1664, 1920, 2176


def _silu(x):
    return x * (1.0 / (1.0 + jnp.exp(-x)))


def _softplus(x):
    return jnp.maximum(x, 0.0) + jnp.log1p(jnp.exp(-jnp.abs(x)))


def _const_spec(shape):
    nd = len(shape)
    return pl.BlockSpec(shape, lambda *_: (0,) * nd, pipeline_mode=pl.Buffered(1))


def _mod_kernel(cond_ref, w_ref, b_ref, o_ref):
    s = _silu(cond_ref[...]).astype(BF16)
    o_ref[...] = jnp.dot(s, w_ref[...].astype(BF16), preferred_element_type=F32) + b_ref[...]


def _modulation(cond, w_mod, b_mod):
    depth, d, n = w_mod.shape
    bn = d
    return pl.pallas_call(
        _mod_kernel,
        out_shape=jax.ShapeDtypeStruct((depth, SUBLANES, n), F32),
        grid=(depth, n // bn),
        in_specs=[
            pl.BlockSpec((SUBLANES, d), lambda l, j: (0, 0)),
            pl.BlockSpec((None, d, bn), lambda l, j: (l, 0, j)),
            pl.BlockSpec((None, 1, bn), lambda l, j: (l, 0, j)),
        ],
        out_specs=pl.BlockSpec((None, SUBLANES, bn), lambda l, j: (l, 0, j)),
        compiler_params=pltpu.CompilerParams(
            dimension_semantics=("arbitrary", "arbitrary"), vmem_limit_bytes=VMEM_LIMIT),
        name="modulation",
    )(cond, w_mod, b_mod.reshape(depth, 1, n))


def _rope_partner(x):
    lane = lax.broadcasted_iota(jnp.int32, x.shape, 1)
    fwd = pltpu.roll(x, LANES - 32, 1)
    bwd = pltpu.roll(x, 32, 1)
    return jnp.where((lane % 64) < 32, fwd, bwd)


def _inproj_kernel(xl_ref, xc_ref, mod_ref, n1_ref, w_ref, qw_ref, kw_ref, cos_ref, sin_ref,
                   pool_ref, z_ref, xbc_ref, dt_ref, q_ref, k_ref, vt_ref, *, n_lat_tiles):
    i = pl.program_id(0)
    is_lat = i < n_lat_tiles
    tm = xl_ref.shape[0]
    for r0 in range(0, tm, INPROJ_ROWS):
        rows = slice(r0, r0 + INPROJ_ROWS)
        x = jnp.where(is_lat, xl_ref[rows, :], xc_ref[rows, :])
        ms = jnp.mean(x * x, axis=-1, keepdims=True)
        h = x * lax.rsqrt(ms + EPS) * n1_ref[...]
        h = h * (1.0 + mod_ref[1:2, :]) + mod_ref[0:1, :]
        u = jnp.dot(h.astype(BF16), w_ref[...], preferred_element_type=F32)
        pool_ref[rows, :] = u[:, C_POOL:C_Z]
        z_ref[rows, :] = u[:, C_Z:C_XBC]
        xbc_ref[rows, :] = u[:, C_XBC:C_DT]
        dt_ref[rows, :] = u[:, C_DT:C_Q]
        vt = u[:, C_V:C_END].T.astype(BF16)
        for hd in range(ATTN_KV_HEADS):
            vt_ref[hd * VT_ROWS:hd * VT_ROWS + HEAD_DIM, rows] = vt[hd * HEAD_DIM:(hd + 1) * HEAD_DIM]
            vt_ref[hd * VT_ROWS + HEAD_DIM:(hd + 1) * VT_ROWS, rows] = jnp.ones(
                (VT_ROWS - HEAD_DIM, INPROJ_ROWS), BF16)
        cos = cos_ref[rows, :]
        sin = sin_ref[rows, :]

        def norm_rope(t, w):
            t = t * lax.rsqrt(jnp.mean(t * t, axis=-1, keepdims=True) + EPS) * w
            return t * cos + _rope_partner(t) * sin

        for hd in range(ATTN_HEADS):
            t = norm_rope(u[:, C_Q + hd * HEAD_DIM:C_Q + (hd + 1) * HEAD_DIM], qw_ref[...])
            q_ref[rows, hd * HEAD_DIM:(hd + 1) * HEAD_DIM] = (t * Q_SCALE).astype(BF16)
        for hd in range(ATTN_KV_HEADS):
            t = norm_rope(u[:, C_K + hd * HEAD_DIM:C_K + (hd + 1) * HEAD_DIM], kw_ref[...])
            k_ref[rows, hd * HEAD_DIM:(hd + 1) * HEAD_DIM] = t.astype(BF16)


def _inproj(xs_lat, xs_ctx, mod, n1, w_cat, qw, kw, cos_tab, sin_tab, batch):
    n_lat, d = xs_lat.shape
    n_ctx = xs_ctx.shape[0]
    tm = TOKEN_TILE
    assert n_ctx == tm and n_lat % (batch * tm) == 0
    n_lat_tiles = n_lat // tm
    tiles_per_seq = n_lat_tiles // batch
    n_tok = n_lat + n_ctx
    grid = (n_lat_tiles + 1,)

    def lat_idx(i):
        return jnp.minimum(i, n_lat_tiles - 1)

    def rope_idx(i):
        return jnp.where(i < n_lat_tiles, i % tiles_per_seq, tiles_per_seq)

    def mod_idx(i):
        return jnp.where(i < n_lat_tiles, i // tiles_per_seq, batch)

    row = lambda w: pl.BlockSpec((tm, w), lambda i: (i, 0))
    outs = [(POOL_DIM, F32), (SSD_INNER, F32), (SSD_XBC, F32), (LANES, F32),
            (ATTN_DIM, BF16), (KV_DIM, BF16)]
    return pl.pallas_call(
        functools.partial(_inproj_kernel, n_lat_tiles=n_lat_tiles),
        out_shape=[jax.ShapeDtypeStruct((n_tok, w), t) for w, t in outs]
        + [jax.ShapeDtypeStruct((n_tok // tm, ATTN_KV_HEADS * VT_ROWS, tm), BF16)],
        grid=grid,
        in_specs=[
            pl.BlockSpec((tm, d), lambda i: (lat_idx(i), 0)),
            pl.BlockSpec((tm, d), lambda i: (0, 0)),
            pl.BlockSpec((None, N_MOD, d), lambda i: (mod_idx(i), 0, 0)),
            _const_spec((1, d)),
            _const_spec(w_cat.shape),
            _const_spec((1, HEAD_DIM)),
            _const_spec((1, HEAD_DIM)),
            pl.BlockSpec((tm, HEAD_DIM), lambda i: (rope_idx(i), 0)),
            pl.BlockSpec((tm, HEAD_DIM), lambda i: (rope_idx(i), 0)),
        ],
        out_specs=[row(w) for w, _ in outs]
        + [pl.BlockSpec((None, ATTN_KV_HEADS * VT_ROWS, tm), lambda i: (i, 0, 0))],
        compiler_params=pltpu.CompilerParams(
            dimension_semantics=("arbitrary",), vmem_limit_bytes=VMEM_LIMIT),
        name="inproj",
    )(xs_lat, xs_ctx, mod, n1, w_cat, qw, kw, cos_tab, sin_tab)


def _conv_kernel(x_ref, prev_ref, next_ref, ctx_ref, cw_ref, cb_ref, o_ref, buf_ref,
                 *, tiles_per_seq, n_lat_tiles, ctx_len):
    i = pl.program_id(0)
    tm = x_ref.shape[0]
    pad = SSD_CONV // 2

    def conv_silu(rows):
        u = buf_ref[0:rows + 2 * SUBLANES, :]
        n = u.shape[0]
        acc = cb_ref[...] + cw_ref[pad:pad + 1, :] * u[SUBLANES:SUBLANES + rows]
        for k in range(SSD_CONV):
            if k != pad:
                shifted = pltpu.roll(u, (pad - k) % n, 0)
                acc = acc + cw_ref[k:k + 1, :] * shifted[SUBLANES:SUBLANES + rows]
        return _silu(acc)

    @pl.when(i < n_lat_tiles)
    def _():
        first = i % tiles_per_seq == 0
        last = i % tiles_per_seq == tiles_per_seq - 1
        buf_ref[0:SUBLANES, :] = jnp.where(first, 0.0, prev_ref[...])
        buf_ref[SUBLANES:SUBLANES + tm, :] = x_ref[...]
        buf_ref[SUBLANES + tm:2 * SUBLANES + tm, :] = jnp.where(last, 0.0, next_ref[...])
        o_ref[...] = conv_silu(tm)

    @pl.when(i == n_lat_tiles)
    def _():
        n_ctx = ctx_ref.shape[0]
        halo = jnp.zeros((SUBLANES, SSD_XBC), F32)
        for r0 in range(0, n_ctx, ctx_len):
            buf_ref[0:SUBLANES, :] = halo
            buf_ref[SUBLANES:SUBLANES + ctx_len, :] = ctx_ref[r0:r0 + ctx_len, :]
            buf_ref[SUBLANES + ctx_len:2 * SUBLANES + ctx_len, :] = halo
            o_ref[r0:r0 + ctx_len, :] = conv_silu(ctx_len)
        o_ref[n_ctx:tm, :] = jnp.zeros((tm - n_ctx, SSD_XBC), F32)


def _ssd_conv(xbc, conv_w, conv_b, batch, n_lat, n_ctx):
    n_tok = xbc.shape[0]
    seq, ctx = n_lat // batch, n_ctx // batch
    tm = CONV_TILE
    assert seq % tm == 0 and n_lat % n_ctx == 0 and n_ctx < tm
    n_lat_tiles = n_lat // tm
    blk8 = tm // SUBLANES
    lat = lambda i: jnp.minimum(i, n_lat_tiles - 1)
    cw = jnp.zeros((SUBLANES, SSD_XBC), F32).at[:SSD_CONV].set(conv_w)
    return pl.pallas_call(
        functools.partial(_conv_kernel, tiles_per_seq=seq // tm, n_lat_tiles=n_lat_tiles,
                          ctx_len=ctx),
        out_shape=jax.ShapeDtypeStruct((n_tok, SSD_XBC), F32),
        grid=(n_lat_tiles + 1,),
        in_specs=[
            pl.BlockSpec((tm, SSD_XBC), lambda i: (lat(i), 0)),
            pl.BlockSpec((SUBLANES, SSD_XBC), lambda i: (jnp.maximum(lat(i) * blk8 - 1, 0), 0)),
            pl.BlockSpec((SUBLANES, SSD_XBC), lambda i: (lat(i) * blk8 + blk8, 0)),
            pl.BlockSpec((n_ctx, SSD_XBC), lambda i: (n_lat // n_ctx, 0)),
            _const_spec((SUBLANES, SSD_XBC)), _const_spec((1, SSD_XBC)),
        ],
        out_specs=pl.BlockSpec((tm, SSD_XBC), lambda i: (i, 0)),
        scratch_shapes=[pltpu.VMEM((tm + 2 * SUBLANES, SSD_XBC), F32)],
        compiler_params=pltpu.CompilerParams(
            dimension_semantics=("arbitrary",), vmem_limit_bytes=VMEM_LIMIT),
        name="ssd_conv",
    )(xbc, xbc, xbc, xbc, cw, conv_b.reshape(1, -1))


def _ssd_kernel(xf_ref, dtf_ref, dtf_next_ref, xb_ref, dtb_ref, dtb_next_ref,
                bias_ref, alog_ref, dsk_ref, spread_ref, ecol_ref,
                yf_ref, yb_ref, h_ref, et_sc, spread_sc, ecols_sc):
    i = pl.program_id(1)
    q = SSD_CHUNK

    @pl.when(i == 0)
    def _():
        h_ref[...] = jnp.zeros_like(h_ref)

    row = lax.broadcasted_iota(jnp.int32, (q, q), 0)
    col = lax.broadcasted_iota(jnp.int32, (q, q), 1)
    n_col = SSD_DIRS * SSD_HEADS
    heads_per_group = SSD_HEADS // SSD_GROUPS
    pair = heads_per_group * SSD_HEAD_DIM
    lane_half = col // SSD_HEAD_DIM
    st_row = lax.broadcasted_iota(jnp.int32, (SSD_GROUPS * SSD_STATE, SSD_INNER), 0)
    st_col = lax.broadcasted_iota(jnp.int32, (SSD_GROUPS * SSD_STATE, SSD_INNER), 1)
    same_group = (st_row // SSD_STATE) == (st_col // pair)
    contract0 = (((0,), (0,)), ((), ()))

    def bf16_terms(v, n_terms):
        terms, rest = [], v
        for _ in range(n_terms):
            t = rest.astype(BF16).astype(F32)
            terms.append(t)
            rest = rest - t
        rows = -(-n_terms * n_col // BF16_SUBLANES) * BF16_SUBLANES
        if rows > n_terms * n_col:
            terms.append(jnp.zeros((rows - n_terms * n_col, v.shape[1]), F32))
        return jnp.concatenate(terms, axis=0).astype(BF16)

    fwd_rows = lax.broadcasted_iota(jnp.int32, (n_col, q), 0) < SSD_HEADS

    def token_decay_sums(dtf_blk_ref, dtb_blk_ref):
        dt_raw = jnp.where(fwd_rows, dtf_blk_ref[...].T[0:n_col], dtb_blk_ref[...].T[0:n_col])
        dt = _softplus(dt_raw + bias_ref[...])
        a = dt * -jnp.exp(alog_ref[...])
        tri_u = (row <= col).astype(F32)
        cs = jnp.dot(a, tri_u, preferred_element_type=F32, precision=lax.Precision.HIGHEST)
        return dt, a, cs

    def token_scalars(dt, a, cs):
        tot = cs[:, q - 1:q]
        e_t = jnp.where(fwd_rows, cs, cs - a)
        grow, shrink = jnp.exp(e_t), jnp.exp(tot - e_t)
        w_in = jnp.where(fwd_rows, grow, shrink)
        w_out = jnp.where(fwd_rows, shrink, grow)
        terms = jnp.concatenate(
            [bf16_terms(dt, 2), bf16_terms(w_in, 2), bf16_terms(dt * w_out, 2)], axis=1)
        spread = lax.dot_general(terms, spread_ref[...], contract0,
                                 preferred_element_type=F32)
        e_cols = lax.dot_general(bf16_terms(e_t, 3), ecol_ref[...], contract0,
                                 preferred_element_type=F32)
        return e_t, spread, e_cols

    def into_scratch(vals):
        et_sc[...], spread_sc[...], ecols_sc[...] = vals

    @pl.when(i == 0)
    def _():
        into_scratch(token_scalars(*token_decay_sums(dtf_ref, dtb_ref)))

    next_sums = token_decay_sums(dtf_next_ref, dtb_next_ref)

    zero = jnp.zeros((), BF16)

    def input_matmuls(d, xc):
        ch = slice(d * SSD_INNER, (d + 1) * SSD_INNER)
        dt_x, w_in_x, dtw_x = spread_sc[0:q, ch], spread_sc[q:2 * q, ch], spread_sc[2 * q:3 * q, ch]
        x = xc[:, 0:SSD_INNER]
        xdt = (x * dt_x).astype(BF16)
        xdtw = (x * dtw_x).astype(BF16)
        b_all = xc[:, SSD_INNER:SSD_INNER + SSD_GROUPS * SSD_STATE].astype(BF16)
        c_all = xc[:, SSD_INNER + SSD_GROUPS * SSD_STATE:].astype(BF16)
        hs = h_ref[d]
        y_off = jnp.dot(c_all, hs.astype(BF16), preferred_element_type=F32) * w_in_x
        cbs = [lax.dot_general(jnp.where(lane_half == g, c_all, zero), b_all,
                               (((1,), (1,)), ((), ())), preferred_element_type=F32)
               for g in range(SSD_GROUPS)]
        upd = lax.dot_general(b_all, xdtw, contract0, preferred_element_type=F32)
        exp_tot = w_in_x[q - 1:q, :] if d == 0 else w_in_x[0:1, :]
        h_ref[d] = exp_tot * hs + jnp.where(same_group, upd, 0.0)
        return x, xdt, y_off, cbs

    def decay_matmuls(d, x, xdt, y_off, cbs, y_ref):
        for g in range(SSD_GROUPS):
            sl = slice(g * pair, (g + 1) * pair)
            y_g = y_off[:, sl] + dsk_ref[d:d + 1, sl] * x[:, sl]
            for hh in range(heads_per_group):
                c = d * SSD_HEADS + g * heads_per_group + hh
                e_col = ecols_sc[:, c * q:(c + 1) * q]
                e_row = et_sc[c:c + 1, :]
                if d == 0:
                    lmat = jnp.where(row >= col, jnp.exp(e_col - e_row), 0.0)
                else:
                    lmat = jnp.where(col >= row, jnp.exp(e_row - e_col), 0.0)
                x_h = jnp.where(lane_half == hh, xdt[:, sl], zero)
                y_g = y_g + jnp.dot((cbs[g] * lmat).astype(BF16), x_h,
                                    preferred_element_type=F32)
            y_ref[:, sl] = y_g

    lin_f = input_matmuls(0, xf_ref)
    lin_b = input_matmuls(1, xb_ref)
    next_scalars = token_scalars(*next_sums)
    decay_matmuls(0, *lin_f, yf_ref)
    decay_matmuls(1, *lin_b, yb_ref)
    into_scratch(next_scalars)


def _ssd(xc, dt_raw, dt_bias, a_log, d_skip, batch, n_lat, n_ctx):
    n_tok = xc.shape[0]
    q = SSD_CHUNK
    lat_ch = n_lat // batch // q
    ctx_ch = n_ctx // batch // q
    n_ch = lat_ch + ctx_ch

    def fwd(b, i):
        return jnp.where(i < ctx_ch, batch * lat_ch + b * ctx_ch + i, b * lat_ch + i - ctx_ch)

    def bwd(b, i):
        return jnp.where(i < ctx_ch, batch * lat_ch + b * ctx_ch + ctx_ch - 1 - i,
                         b * lat_ch + n_ch - 1 - i)

    def specs(ch):
        return [
            pl.BlockSpec((q, SSD_XBC), lambda b, i: (ch(b, i), 0)),
            pl.BlockSpec((q, LANES), lambda b, i: (ch(b, i), 0)),
            pl.BlockSpec((q, LANES), lambda b, i: (ch(b, jnp.minimum(i + 1, n_ch - 1)), 0)),
        ]

    n_col = SSD_DIRS * SSD_HEADS
    rows = lambda v: jnp.broadcast_to(v.reshape(n_col, 1), (n_col, q))
    dsk = jnp.repeat(d_skip, SSD_HEAD_DIM, axis=1)
    term_row = jnp.arange(2 * BF16_SUBLANES) % n_col
    live3 = jnp.arange(2 * BF16_SUBLANES) < 3 * n_col
    col_of_channel = jnp.arange(SSD_DIRS * SSD_INNER) // SSD_HEAD_DIM
    col_of_block = jnp.arange(n_col * q) // q
    spread = (term_row[:BF16_SUBLANES, None] == col_of_channel[None, :]).astype(BF16)
    ecol = ((term_row[:, None] == col_of_block[None, :]) & live3[:, None]).astype(BF16)
    return pl.pallas_call(
        _ssd_kernel,
        out_shape=[jax.ShapeDtypeStruct((n_tok, SSD_INNER), F32)] * 2,
        grid=(batch, n_ch),
        in_specs=specs(fwd) + specs(bwd) + [
            _const_spec((n_col, q)), _const_spec((n_col, q)), _const_spec((SSD_DIRS, SSD_INNER)),
            _const_spec(spread.shape), _const_spec(ecol.shape),
        ],
        out_specs=[pl.BlockSpec((q, SSD_INNER), lambda b, i: (fwd(b, i), 0)),
                   pl.BlockSpec((q, SSD_INNER), lambda b, i: (bwd(b, i), 0))],
        scratch_shapes=[pltpu.VMEM((SSD_DIRS, SSD_GROUPS * SSD_STATE, SSD_INNER), F32),
                        pltpu.VMEM((n_col, q), F32),
                        pltpu.VMEM((3 * q, SSD_DIRS * SSD_INNER), F32),
                        pltpu.VMEM((q, n_col * q), F32)],
        compiler_params=pltpu.CompilerParams(
            dimension_semantics=("arbitrary", "arbitrary"), vmem_limit_bytes=VMEM_LIMIT),
        name="ssd_scan",
    )(xc, dt_raw, dt_raw, xc, dt_raw, dt_raw,
      rows(dt_bias), rows(a_log), dsk, spread, ecol)


def _steps_per_trip(n_steps, preferred):
    return max(u for u in range(2, preferred + 1, 2) if n_steps % u == 0) if n_steps else 2


def _scores_t(k, q2):
    return lax.dot_general(k, q2, (((1,), (1,)), ((), ())), preferred_element_type=F32)


def _softmax_stage(s, m_blk, m):
    if m is None:
        return m_blk, None, jnp.exp2((s - m_blk).astype(BF16))
    m_new = jnp.maximum(m, m_blk)
    return m_new, jnp.exp2(m - m_new), jnp.exp2((s - m_new).astype(BF16))


def _attn_kernel(q_ref, kc_ref, vtc_ref, kx_ref, vtx_ref, o_ref, s0_ref, s1_ref, p0_ref, p1_ref,
                 *, n_chunks):
    tq = q_ref.shape[0]
    tk = ATTN_K_TILE
    q2 = jnp.concatenate([q_ref[:, 0:HEAD_DIM], q_ref[:, HEAD_DIM:2 * HEAD_DIM]], axis=0)
    s_refs, p_refs = (s0_ref, s1_ref), (p0_ref, p1_ref)

    def scores_into(slot, c):
        s = _scores_t(kx_ref[pl.ds(pl.multiple_of(c * tk, tk), tk), :], q2)
        s_refs[slot][...] = s
        return jnp.max(s, axis=0, keepdims=True)

    def softmax_into(slot, m_blk, m):
        m, alpha, p = _softmax_stage(s_refs[slot][...], m_blk, m)
        p_refs[slot][...] = p
        return m, alpha

    def step(c, par, carry, with_scores, with_softmax):
        m, acc, alpha, m_blk = carry
        m_blk_next, alpha_next = m_blk, alpha
        if with_scores:
            m_blk_next = scores_into(par, c + 2)
        if with_softmax:
            m, alpha_next = softmax_into(1 - par, m_blk, m)
        acc = alpha * acc + jnp.dot(vtx_ref[c], p_refs[par][...], preferred_element_type=F32)
        return m, acc, alpha_next, m_blk_next

    if n_chunks:
        m_blk0 = scores_into(0, 0)
        m_blk1 = scores_into(1, 1)
    s_ctx = _scores_t(kc_ref[...], q2)
    m, _, p_ctx = _softmax_stage(s_ctx, jnp.max(s_ctx, axis=0, keepdims=True), None)
    acc = jnp.dot(vtc_ref[...], p_ctx, preferred_element_type=F32)
    if n_chunks:
        m, alpha = softmax_into(0, m_blk0, m)
        carry = (m, acc, alpha, m_blk1)
        unroll = _steps_per_trip(n_chunks - 2, ATTN_UNROLL)

        def body(i, carry):
            for u in range(unroll):
                carry = step(unroll * i + u, u % 2, carry, True, True)
            return carry

        carry = lax.fori_loop(0, (n_chunks - 2) // unroll, body, carry)
        carry = step(n_chunks - 2, 0, carry, False, True)
        _, acc, _, _ = step(n_chunks - 1, 1, carry, False, False)
    o = (acc[0:HEAD_DIM] / acc[HEAD_DIM:HEAD_DIM + 1]).T.astype(o_ref.dtype)
    o_ref[:, 0:HEAD_DIM] = o[0:tq]
    o_ref[:, HEAD_DIM:2 * HEAD_DIM] = o[tq:2 * tq]


def _attn_bounded_kernel(q_ref, kc_ref, vtc_ref, kx_ref, vtx_ref, o_ref, p0_ref, p1_ref, *, n_chunks):
    tq = q_ref.shape[0]
    tk = ATTN_K_TILE
    q2 = jnp.concatenate([q_ref[:, 0:HEAD_DIM], q_ref[:, HEAD_DIM:2 * HEAD_DIM]], axis=0)
    p_refs = (p0_ref, p1_ref)

    def colsum(p):
        return jnp.sum(p.reshape(p.shape[0] // SUBLANES, SUBLANES, p.shape[1]), axis=0)

    def weights_into(slot, c):
        p = jnp.exp2(_scores_t(kx_ref[pl.ds(pl.multiple_of(c * tk, tk), tk), :], q2))
        p_refs[slot][...] = p.astype(BF16)
        return colsum(p)

    def step(c, par, carry, with_scores):
        l8, acc = carry
        if with_scores:
            l8 = l8 + weights_into(1 - par, c + 1)
        acc = acc + jnp.dot(vtx_ref[c, 0:HEAD_DIM, :], p_refs[par][...],
                            preferred_element_type=F32)
        return l8, acc

    p_ctx = jnp.exp2(_scores_t(kc_ref[...], q2))
    l8 = colsum(p_ctx) + weights_into(0, 0)
    acc = jnp.dot(vtc_ref[0:HEAD_DIM, :], p_ctx.astype(BF16), preferred_element_type=F32)
    unroll = _steps_per_trip(n_chunks - 2, ATTN_BOUNDED_UNROLL)

    def body(i, carry):
        for u in range(unroll):
            carry = step(unroll * i + u, u % 2, carry, True)
        return carry

    carry = lax.fori_loop(0, (n_chunks - 2) // unroll, body, (l8, acc))
    carry = step(n_chunks - 2, 0, carry, True)
    l8, acc = step(n_chunks - 1, 1, carry, False)
    o = (acc / jnp.sum(l8, axis=0, keepdims=True)).T.astype(o_ref.dtype)
    o_ref[:, 0:HEAD_DIM] = o[0:tq]
    o_ref[:, HEAD_DIM:2 * HEAD_DIM] = o[tq:2 * tq]


def _attention_latent(q, k, vt, batch, n_lat, n_ctx, bounded):
    seq = n_lat // batch
    ctx = n_ctx // batch
    tq, tk = ATTN_Q_TILE, ATTN_K_TILE
    nq = seq // tq
    n_chunks = seq // tk
    assert seq % (2 * tk) == 0 and n_lat % ctx == 0 and n_ctx == tk and vt.shape[2] == tk
    r = 2 * tq
    body = _attn_bounded_kernel if bounded else _attn_kernel
    p_bufs = [pltpu.VMEM((tk, r), BF16), pltpu.VMEM((tk, r), BF16)]
    s_bufs = [] if bounded else [pltpu.VMEM((tk, r), F32), pltpu.VMEM((tk, r), F32)]
    return pl.pallas_call(
        functools.partial(body, n_chunks=n_chunks),
        out_shape=jax.ShapeDtypeStruct((n_lat, ATTN_DIM), BF16),
        grid=(batch, ATTN_KV_HEADS, nq),
        in_specs=[pl.BlockSpec((tq, 2 * HEAD_DIM), lambda b, g, j: (b * nq + j, g)),
                  pl.BlockSpec((ctx, HEAD_DIM), lambda b, g, j: (n_lat // ctx + b, g)),
                  pl.BlockSpec((None, VT_ROWS, ctx), lambda b, g, j: (n_lat // tk, g, b)),
                  pl.BlockSpec((seq, HEAD_DIM), lambda b, g, j: (b, g)),
                  pl.BlockSpec((n_chunks, VT_ROWS, tk), lambda b, g, j: (b, g, 0))],
        out_specs=pl.BlockSpec((tq, 2 * HEAD_DIM), lambda b, g, j: (b * nq + j, g)),
        scratch_shapes=s_bufs + p_bufs,
        compiler_params=pltpu.CompilerParams(
            dimension_semantics=("arbitrary",) * 3, vmem_limit_bytes=VMEM_LIMIT),
        name="attn_latent_bounded" if bounded else "attn_latent",
    )(q, k, vt, k, vt)


def _attn_ctx_kernel(q_ref, kc_ref, vtc_ref, o_ref):
    _attn_kernel(q_ref, kc_ref, vtc_ref, None, None, o_ref, None, None, None, None, n_chunks=0)


def _attention_context(q, k, vt, batch, n_lat, n_ctx):
    ctx = n_ctx // batch
    tk = ATTN_K_TILE
    return pl.pallas_call(
        _attn_ctx_kernel,
        out_shape=jax.ShapeDtypeStruct((n_ctx, ATTN_DIM), BF16),
        grid=(batch, ATTN_KV_HEADS),
        in_specs=[pl.BlockSpec((ctx, 2 * HEAD_DIM), lambda b, g: (n_lat // ctx + b, g)),
                  pl.BlockSpec((ctx, HEAD_DIM), lambda b, g: (n_lat // ctx + b, g)),
                  pl.BlockSpec((None, VT_ROWS, ctx), lambda b, g: (n_lat // tk, g, b))],
        out_specs=pl.BlockSpec((ctx, 2 * HEAD_DIM), lambda b, g: (b, g)),
        compiler_params=pltpu.CompilerParams(
            dimension_semantics=("arbitrary",) * 2, vmem_limit_bytes=VMEM_LIMIT),
        name="attn_context",
    )(q, k, vt)


def _post_kernel(xs_ref, up_ref, upp_ref, upn_ref, z_ref, yf_ref, yb_ref, at_ref, mod_ref,
                 pw_ref, ps_ref, sw_ref, wo_ref, n2_ref, w1_ref, w3_ref, w2_ref,
                 o_ref, buf_ref, *, tiles_per_seq, seq_len):
    i = pl.program_id(0)
    tm = xs_ref.shape[0]
    halo = SUBLANES
    j = i % tiles_per_seq

    buf_ref[0:halo, :] = jnp.where(j == 0, 0.0, upp_ref[...])
    buf_ref[halo:halo + tm, :] = up_ref[...]
    buf_ref[halo + tm:2 * halo + tm, :] = jnp.where(j == tiles_per_seq - 1, 0.0, upn_ref[...])
    ub = buf_ref[...]
    n = tm + 2 * halo
    s2 = ub + pltpu.roll(ub, 1, 0)
    s4 = pltpu.roll(s2, 1, 0) + pltpu.roll(s2, n - 1, 0)
    s8 = pltpu.roll(s4, 2, 0) + pltpu.roll(s4, n - 2, 0)
    s16 = pltpu.roll(s8, 4, 0) + pltpu.roll(s8, n - 4, 0)
    shape = (tm, POOL_DIM)
    grp = lax.broadcasted_iota(jnp.int32, shape, 1) // POOL_GDIM
    t = lax.broadcasted_iota(jnp.int32, shape, 0) + j * tm
    half = jnp.left_shift(1, grp)
    cnt = (jnp.minimum(t + half, seq_len) - jnp.maximum(t - half, 0)).astype(F32)
    sl = slice(halo, halo + tm)
    wsum = jnp.where(grp == 0, s2[sl], jnp.where(grp == 1, s4[sl],
                                                 jnp.where(grp == 2, s8[sl], s16[sl])))
    pooled = wsum / cnt - ub[sl]
    pool = jnp.dot(pooled.astype(BF16), pw_ref[...], preferred_element_type=F32) * ps_ref[...]

    gy = (yf_ref[...] + yb_ref[...]) * _silu(z_ref[...])
    gw = SSD_INNER // SSD_GROUPS
    parts = []
    for g in range(SSD_GROUPS):
        t_g = gy[:, g * gw:(g + 1) * gw]
        parts.append(t_g * lax.rsqrt(jnp.mean(t_g * t_g, axis=-1, keepdims=True) + EPS))
    ssd = jnp.concatenate(parts, axis=1) * sw_ref[...]

    mix = jnp.concatenate([pool.astype(BF16), ssd.astype(BF16), at_ref[...]], axis=1)
    blocks = [slice(r0, r0 + min(POST_ROWS, tm)) for r0 in range(0, tm, POST_ROWS)]
    x1 = [xs_ref[rows, :] + mod_ref[2:3, :] * jnp.dot(mix[rows], wo_ref[...],
                                                       preferred_element_type=F32)
          for rows in blocks]

    def modulated_norm(v):
        h = v * lax.rsqrt(jnp.mean(v * v, axis=-1, keepdims=True) + EPS) * n2_ref[...]
        return (h * (1.0 + mod_ref[4:5, :]) + mod_ref[3:4, :]).astype(BF16)

    gated = []
    for v in x1:
        h = modulated_norm(v)
        a = jnp.dot(h, w1_ref[...], preferred_element_type=F32)
        b = jnp.dot(h, w3_ref[...], preferred_element_type=F32)
        gated.append((_silu(a) * b).astype(BF16))
    for rows, v, gt in zip(blocks, x1, gated):
        ff = jnp.dot(gt, w2_ref[...], preferred_element_type=F32)
        o_ref[rows, :] = v + mod_ref[5:6, :] * ff


def _post(xs, upool, z, yf, yb, attn, mod, pw_bd, pscale, ssd_w, w_out, n2, w1, w3, w2,
          *, tm, row0, tiles_per_seq, mod_row):
    n_rows, d = xs.shape
    n_tiles = n_rows // tm
    off = row0 // tm
    blk8 = tm // SUBLANES
    last8 = upool.shape[0] // SUBLANES - 1
    tok = lambda w: pl.BlockSpec((tm, w), lambda i: (off + i, 0))
    loc = lambda w: pl.BlockSpec((tm, w), lambda i: (i, 0))
    return pl.pallas_call(
        functools.partial(_post_kernel, tiles_per_seq=tiles_per_seq, seq_len=tiles_per_seq * tm),
        out_shape=jax.ShapeDtypeStruct((n_rows, d), F32),
        grid=(n_tiles,),
        in_specs=[
            loc(d),
            tok(POOL_DIM),
            pl.BlockSpec((SUBLANES, POOL_DIM), lambda i: (jnp.maximum((off + i) * blk8 - 1, 0), 0)),
            pl.BlockSpec((SUBLANES, POOL_DIM),
                         lambda i: (jnp.minimum((off + i) * blk8 + blk8, last8), 0)),
            tok(SSD_INNER), tok(SSD_INNER), tok(SSD_INNER),
            loc(ATTN_DIM),
            pl.BlockSpec((None, N_MOD, d), lambda i: (mod_row(i), 0, 0)),
            _const_spec(pw_bd.shape), _const_spec((1, POOL_DIM)), _const_spec((1, SSD_INNER)),
            _const_spec(w_out.shape), _const_spec((1, d)),
            _const_spec(w1.shape), _const_spec(w3.shape), _const_spec(w2.shape),
        ],
        out_specs=loc(d),
        scratch_shapes=[pltpu.VMEM((tm + 2 * SUBLANES, POOL_DIM), F32)],
        compiler_params=pltpu.CompilerParams(
            dimension_semantics=("arbitrary",), vmem_limit_bytes=VMEM_LIMIT),
        name="post_mix_ffn",
    )(xs, upool, upool, upool, z, yf, yb, attn, mod, pw_bd, pscale, ssd_w, w_out, n2, w1, w3, w2)


def _rope_tables(seq, extra_rows):
    rows, pad_rows = seq // GRID_W, extra_rows // GRID_W
    axis_dim = HEAD_DIM // 2
    inv_freq = ROPE_THETA ** (-jnp.arange(0, axis_dim, 2, dtype=F32) / axis_dim)
    ang_r = jnp.arange(rows, dtype=F32)[:, None] * inv_freq[None, :]
    ang_c = jnp.arange(GRID_W, dtype=F32)[:, None] * inv_freq[None, :]
    live = (jnp.arange(rows + pad_rows) < rows)[:, None, None]
    shape = (rows + pad_rows, GRID_W, axis_dim // 2)

    def per_row(t, fill):
        t = jnp.concatenate([t, jnp.full((pad_rows, t.shape[1]), fill, F32)], axis=0)
        return jnp.broadcast_to(t[:, None, :], shape)

    def per_col(t, fill):
        return jnp.where(live, jnp.broadcast_to(t[None, :, :], shape), fill)

    cr, sr = per_row(jnp.cos(ang_r), 1.0), per_row(jnp.sin(ang_r), 0.0)
    cc, sc = per_col(jnp.cos(ang_c), 1.0), per_col(jnp.sin(ang_c), 0.0)
    cos = jnp.concatenate([cr, cr, cc, cc], axis=-1).reshape(-1, HEAD_DIM)
    sin = jnp.concatenate([-sr, sr, -sc, sc], axis=-1).reshape(-1, HEAD_DIM)
    return cos, sin


def _fuse_w_in(w_in):
    d = w_in.shape[0]
    o_dt = POOL_DIM + SSD_INNER + SSD_XBC
    o_att = o_dt + SSD_DIRS * SSD_HEADS
    dt_pad = jnp.zeros((d, C_Q - C_DT - SSD_DIRS * SSD_HEADS), w_in.dtype)
    return jnp.concatenate([w_in[:, :o_dt], w_in[:, o_dt:o_att], dt_pad, w_in[:, o_att:]],
                           axis=1).astype(BF16)


def kernel(x, c, ctx, c_ctx, norm1_w, norm2_w, w_mod, b_mod, w_in, pool_w, pool_scale, conv_w, conv_b,
           dt_bias, a_log, d_skip, ssd_norm_w, q_norm_w, k_norm_w, w_out, w1, w3, w2):
    batch, seq, d = x.shape
    ctx_len = ctx.shape[1]
    depth = w_mod.shape[0]
    n_lat, n_ctx = batch * seq, batch * ctx_len
    assert batch + 1 <= SUBLANES and seq % TOKEN_TILE == 0 and seq % GRID_W == 0

    cond = jnp.zeros((SUBLANES, d), F32).at[:batch].set(c).at[batch].set(c_ctx)
    mod_all = _modulation(cond, w_mod, b_mod).reshape(depth, SUBLANES, N_MOD, d)
    cos_tab, sin_tab = _rope_tables(seq, TOKEN_TILE)

    xs_lat = x.reshape(n_lat, d)
    xs_ctx = ctx.reshape(n_ctx, d)
    tiles_per_seq = seq // TOKEN_TILE
    for layer in range(depth):
        need_ctx = layer < depth - 1
        mod = mod_all[layer]
        w_cat = _fuse_w_in(w_in[layer])
        upool, z, xbc, dt_raw, q, k, vt = _inproj(
            xs_lat, xs_ctx, mod, norm1_w[layer].reshape(1, d), w_cat,
            q_norm_w[layer].reshape(1, -1), k_norm_w[layer].reshape(1, -1), cos_tab, sin_tab, batch)
        xc = _ssd_conv(xbc, conv_w[layer], conv_b[layer], batch, n_lat, n_ctx)
        yf, yb = _ssd(xc, dt_raw, dt_bias[layer], a_log[layer], d_skip[layer], batch, n_lat, n_ctx)
        score_bound = (HEAD_DIM * Q_SCALE * SCORE_BOUND_SLACK * jnp.max(jnp.abs(q_norm_w[layer]))
                       * jnp.max(jnp.abs(k_norm_w[layer])))
        attn_x = lax.cond(
            score_bound < SCORE_BOUND_LIMIT,
            functools.partial(_attention_latent, batch=batch, n_lat=n_lat, n_ctx=n_ctx, bounded=True),
            functools.partial(_attention_latent, batch=batch, n_lat=n_lat, n_ctx=n_ctx, bounded=False),
            q, k, vt)
        pw_bd = jax.scipy.linalg.block_diag(*[pool_w[layer, g] for g in range(POOL_GROUPS)]).astype(BF16)
        post = functools.partial(
            _post, upool=upool, z=z, yf=yf, yb=yb, mod=mod, pw_bd=pw_bd,
            pscale=pool_scale[layer].reshape(1, -1), ssd_w=ssd_norm_w[layer].reshape(1, -1),
            w_out=w_out[layer].astype(BF16), n2=norm2_w[layer].reshape(1, d),
            w1=w1[layer].astype(BF16), w3=w3[layer].astype(BF16), w2=w2[layer].astype(BF16))
        new_lat = post(xs_lat, attn=attn_x, tm=TOKEN_TILE, row0=0, tiles_per_seq=tiles_per_seq,
                       mod_row=lambda i: i // tiles_per_seq)
        if need_ctx:
            attn_c = _attention_context(q, k, vt, batch, n_lat, n_ctx)
            xs_ctx = post(xs_ctx, attn=attn_c, tm=ctx_len, row0=n_lat, tiles_per_seq=1,
                          mod_row=lambda i: batch)
        xs_lat = new_lat
    return xs_lat.reshape(batch, seq, d)
```

```python
import functools

import jax
import jax.numpy as jnp
from jax import lax
from jax.experimental import pallas as pl
from jax.experimental.pallas import tpu as pltpu

F32 = jnp.float32
BF16 = jnp.bfloat16

GRID_W = 64
EPS = 1e-6
N_MOD = 6
POOL_DIM = 256
POOL_GDIM = 64
POOL_GROUPS = 4
SSD_HEADS = 4
SSD_HEAD_DIM = 64
SSD_INNER = 256
SSD_GROUPS = 2
SSD_STATE = 64
SSD_CONV = 5
SSD_CHUNK = 128
SSD_STEP_CHUNKS = 2
SSD_DIRS = 2
SSD_XBC = 512
ATTN_HEADS = 4
ATTN_KV_HEADS = 2
HEAD_DIM = 128
ATTN_DIM = 512
KV_DIM = 256
ROPE_THETA = 10000.0
Q_SCALE = HEAD_DIM ** -0.5 * 1.4426950408889634
SCORE_BOUND_LIMIT = 60.0
SCORE_BOUND_SLACK = 1.02

LANES = 128
SUBLANES = 8
TOKEN_TILE = 512
INPROJ_ROWS = 256
CONV_TILE = 1024
POST_ROWS = 256
ATTN_Q_TILE = 512
ATTN_K_TILE = 512
ATTN_UNROLL = 6
ATTN_BOUNDED_UNROLL = 30
BF16_SUBLANES = 16
VT_ROWS = HEAD_DIM + BF16_SUBLANES
VMEM_LIMIT = 56 * 1024 * 1024

C_POOL, C_Z, C_XBC, C_DT, C_Q, C_K, C_V, C_END = 0, 256, 512, 1024, 1152, 1664, 1920, 2176


def _silu(x):
    return x * (1.0 / (1.0 + jnp.exp(-x)))


def _softplus(x):
    return jnp.maximum(x, 0.0) + jnp.log1p(jnp.exp(-jnp.abs(x)))


def _const_spec(shape):
    nd = len(shape)
    return pl.BlockSpec(shape, lambda *_: (0,) * nd, pipeline_mode=pl.Buffered(1))


def _mod_kernel(cond_ref, w_ref, b_ref, o_ref):
    s = _silu(cond_ref[...]).astype(BF16)
    o_ref[...] = jnp.dot(s, w_ref[...].astype(BF16), preferred_element_type=F32) + b_ref[...]


def _modulation(cond, w_mod, b_mod):
    depth, d, n = w_mod.shape
    bn = d
    return pl.pallas_call(
        _mod_kernel,
        out_shape=jax.ShapeDtypeStruct((depth, SUBLANES, n), F32),
        grid=(depth, n // bn),
        in_specs=[
            pl.BlockSpec((SUBLANES, d), lambda l, j: (0, 0)),
            pl.BlockSpec((None, d, bn), lambda l, j: (l, 0, j)),
            pl.BlockSpec((None, 1, bn), lambda l, j: (l, 0, j)),
        ],
        out_specs=pl.BlockSpec((None, SUBLANES, bn), lambda l, j: (l, 0, j)),
        compiler_params=pltpu.CompilerParams(
            dimension_semantics=("arbitrary", "arbitrary"), vmem_limit_bytes=VMEM_LIMIT),
        name="modulation",
    )(cond, w_mod, b_mod.reshape(depth, 1, n))


def _rope_partner(x):
    lane = lax.broadcasted_iota(jnp.int32, x.shape, 1)
    fwd = pltpu.roll(x, LANES - 32, 1)
    bwd = pltpu.roll(x, 32, 1)
    return jnp.where((lane % 64) < 32, fwd, bwd)


def _inproj_kernel(xl_ref, xc_ref, mod_ref, n1_ref, w_ref, qw_ref, kw_ref, cos_ref, sin_ref,
                   pool_ref, z_ref, xbc_ref, dt_ref, q_ref, k_ref, vt_ref, *, n_lat_tiles):
    i = pl.program_id(0)
    is_lat = i < n_lat_tiles
    tm = xl_ref.shape[0]
    for r0 in range(0, tm, INPROJ_ROWS):
        rows = slice(r0, r0 + INPROJ_ROWS)
        x = jnp.where(is_lat, xl_ref[rows, :], xc_ref[rows, :])
        ms = jnp.mean(x * x, axis=-1, keepdims=True)
        h = x * lax.rsqrt(ms + EPS) * n1_ref[...]
        h = h * (1.0 + mod_ref[1:2, :]) + mod_ref[0:1, :]
        u = jnp.dot(h.astype(BF16), w_ref[...], preferred_element_type=F32)
        pool_ref[rows, :] = u[:, C_POOL:C_Z]
        z_ref[rows, :] = u[:, C_Z:C_XBC]
        xbc_ref[rows, :] = u[:, C_XBC:C_DT]
        dt_ref[rows, :] = u[:, C_DT:C_Q]
        vt = u[:, C_V:C_END].T.astype(BF16)
        for hd in range(ATTN_KV_HEADS):
            vt_ref[hd * VT_ROWS:hd * VT_ROWS + HEAD_DIM, rows] = vt[hd * HEAD_DIM:(hd + 1) * HEAD_DIM]
            vt_ref[hd * VT_ROWS + HEAD_DIM:(hd + 1) * VT_ROWS, rows] = jnp.ones(
                (VT_ROWS - HEAD_DIM, INPROJ_ROWS), BF16)
        cos = cos_ref[rows, :]
        sin = sin_ref[rows, :]

        def norm_rope(t, w):
            t = t * lax.rsqrt(jnp.mean(t * t, axis=-1, keepdims=True) + EPS) * w
            return t * cos + _rope_partner(t) * sin

        for hd in range(ATTN_HEADS):
            t = norm_rope(u[:, C_Q + hd * HEAD_DIM:C_Q + (hd + 1) * HEAD_DIM], qw_ref[...])
            q_ref[rows, hd * HEAD_DIM:(hd + 1) * HEAD_DIM] = (t * Q_SCALE).astype(BF16)
        for hd in range(ATTN_KV_HEADS):
            t = norm_rope(u[:, C_K + hd * HEAD_DIM:C_K + (hd + 1) * HEAD_DIM], kw_ref[...])
            k_ref[rows, hd * HEAD_DIM:(hd + 1) * HEAD_DIM] = t.astype(BF16)


def _inproj(xs_lat, xs_ctx, mod, n1, w_cat, qw, kw, cos_tab, sin_tab, batch):
    n_lat, d = xs_lat.shape
    n_ctx = xs_ctx.shape[0]
    tm = TOKEN_TILE
    assert n_ctx == tm and n_lat % (batch * tm) == 0
    n_lat_tiles = n_lat // tm
    tiles_per_seq = n_lat_tiles // batch
    n_tok = n_lat + n_ctx
    grid = (n_lat_tiles + 1,)

    def lat_idx(i):
        return jnp.minimum(i, n_lat_tiles - 1)

    def rope_idx(i):
        return jnp.where(i < n_lat_tiles, i % tiles_per_seq, tiles_per_seq)

    def mod_idx(i):
        return jnp.where(i < n_lat_tiles, i // tiles_per_seq, batch)

    row = lambda w: pl.BlockSpec((tm, w), lambda i: (i, 0))
    outs = [(POOL_DIM, F32), (SSD_INNER, F32), (SSD_XBC, F32), (LANES, F32),
            (ATTN_DIM, BF16), (KV_DIM, BF16)]
    return pl.pallas_call(
        functools.partial(_inproj_kernel, n_lat_tiles=n_lat_tiles),
        out_shape=[jax.ShapeDtypeStruct((n_tok, w), t) for w, t in outs]
        + [jax.ShapeDtypeStruct((n_tok // tm, ATTN_KV_HEADS * VT_ROWS, tm), BF16)],
        grid=grid,
        in_specs=[
            pl.BlockSpec((tm, d), lambda i: (lat_idx(i), 0)),
            pl.BlockSpec((tm, d), lambda i: (0, 0)),
            pl.BlockSpec((None, N_MOD, d), lambda i: (mod_idx(i), 0, 0)),
            _const_spec((1, d)),
            _const_spec(w_cat.shape),
            _const_spec((1, HEAD_DIM)),
            _const_spec((1, HEAD_DIM)),
            pl.BlockSpec((tm, HEAD_DIM), lambda i: (rope_idx(i), 0)),
            pl.BlockSpec((tm, HEAD_DIM), lambda i: (rope_idx(i), 0)),
        ],
        out_specs=[row(w) for w, _ in outs]
        + [pl.BlockSpec((None, ATTN_KV_HEADS * VT_ROWS, tm), lambda i: (i, 0, 0))],
        compiler_params=pltpu.CompilerParams(
            dimension_semantics=("arbitrary",), vmem_limit_bytes=VMEM_LIMIT),
        name="inproj",
    )(xs_lat, xs_ctx, mod, n1, w_cat, qw, kw, cos_tab, sin_tab)


def _conv_kernel(x_ref, prev_ref, next_ref, ctx_ref, cw_ref, cb_ref, o_ref, buf_ref,
                 *, tiles_per_seq, n_lat_tiles, ctx_len):
    i = pl.program_id(0)
    tm = x_ref.shape[0]
    pad = SSD_CONV // 2

    def conv_silu(rows):
        u = buf_ref[0:rows + 2 * SUBLANES, :]
        n = u.shape[0]
        acc = cb_ref[...] + cw_ref[pad:pad + 1, :] * u[SUBLANES:SUBLANES + rows]
        for k in range(SSD_CONV):
            if k != pad:
                shifted = pltpu.roll(u, (pad - k) % n, 0)
                acc = acc + cw_ref[k:k + 1, :] * shifted[SUBLANES:SUBLANES + rows]
        return _silu(acc)

    @pl.when(i < n_lat_tiles)
    def _():
        first = i % tiles_per_seq == 0
        last = i % tiles_per_seq == tiles_per_seq - 1
        buf_ref[0:SUBLANES, :] = jnp.where(first, 0.0, prev_ref[...])
        buf_ref[SUBLANES:SUBLANES + tm, :] = x_ref[...]
        buf_ref[SUBLANES + tm:2 * SUBLANES + tm, :] = jnp.where(last, 0.0, next_ref[...])
        o_ref[...] = conv_silu(tm)

    @pl.when(i == n_lat_tiles)
    def _():
        n_ctx = ctx_ref.shape[0]
        halo = jnp.zeros((SUBLANES, SSD_XBC), F32)
        for r0 in range(0, n_ctx, ctx_len):
            buf_ref[0:SUBLANES, :] = halo
            buf_ref[SUBLANES:SUBLANES + ctx_len, :] = ctx_ref[r0:r0 + ctx_len, :]
            buf_ref[SUBLANES + ctx_len:2 * SUBLANES + ctx_len, :] = halo
            o_ref[r0:r0 + ctx_len, :] = conv_silu(ctx_len)
        o_ref[n_ctx:tm, :] = jnp.zeros((tm - n_ctx, SSD_XBC), F32)


def _ssd_conv(xbc, conv_w, conv_b, batch, n_lat, n_ctx):
    n_tok = xbc.shape[0]
    seq, ctx = n_lat // batch, n_ctx // batch
    tm = CONV_TILE
    assert seq % tm == 0 and n_lat % n_ctx == 0 and n_ctx < tm
    n_lat_tiles = n_lat // tm
    blk8 = tm // SUBLANES
    lat = lambda i: jnp.minimum(i, n_lat_tiles - 1)
    cw = jnp.zeros((SUBLANES, SSD_XBC), F32).at[:SSD_CONV].set(conv_w)
    return pl.pallas_call(
        functools.partial(_conv_kernel, tiles_per_seq=seq // tm, n_lat_tiles=n_lat_tiles,
                          ctx_len=ctx),
        out_shape=jax.ShapeDtypeStruct((n_tok, SSD_XBC), F32),
        grid=(n_lat_tiles + 1,),
        in_specs=[
            pl.BlockSpec((tm, SSD_XBC), lambda i: (lat(i), 0)),
            pl.BlockSpec((SUBLANES, SSD_XBC), lambda i: (jnp.maximum(lat(i) * blk8 - 1, 0), 0)),
            pl.BlockSpec((SUBLANES, SSD_XBC), lambda i: (lat(i) * blk8 + blk8, 0)),
            pl.BlockSpec((n_ctx, SSD_XBC), lambda i: (n_lat // n_ctx, 0)),
            _const_spec((SUBLANES, SSD_XBC)), _const_spec((1, SSD_XBC)),
        ],
        out_specs=pl.BlockSpec((tm, SSD_XBC), lambda i: (i, 0)),
        scratch_shapes=[pltpu.VMEM((tm + 2 * SUBLANES, SSD_XBC), F32)],
        compiler_params=pltpu.CompilerParams(
            dimension_semantics=("arbitrary",), vmem_limit_bytes=VMEM_LIMIT),
        name="ssd_conv",
    )(xbc, xbc, xbc, xbc, cw, conv_b.reshape(1, -1))


def _ssd_kernel(xf_ref, dtf_ref, dtf_next_ref, xb_ref, dtb_ref, dtb_next_ref,
                bias_ref, alog_ref, dsk_ref, spread_ref, ecol_ref,
                yf_ref, yb_ref, h_ref, et_sc, spread_sc, ecols_sc):
    i = pl.program_id(1)
    q = SSD_CHUNK
    n_sub = SSD_STEP_CHUNKS
    blk = n_sub * q

    @pl.when(i == 0)
    def _():
        h_ref[...] = jnp.zeros_like(h_ref)

    row = lax.broadcasted_iota(jnp.int32, (q, q), 0)
    col = lax.broadcasted_iota(jnp.int32, (q, q), 1)
    n_col = SSD_DIRS * SSD_HEADS
    heads_per_group = SSD_HEADS // SSD_GROUPS
    pair = heads_per_group * SSD_HEAD_DIM
    lane_half = col // SSD_HEAD_DIM
    st_row = lax.broadcasted_iota(jnp.int32, (SSD_GROUPS * SSD_STATE, SSD_INNER), 0)
    st_col = lax.broadcasted_iota(jnp.int32, (SSD_GROUPS * SSD_STATE, SSD_INNER), 1)
    same_group = (st_row // SSD_STATE) == (st_col // pair)
    contract0 = (((0,), (0,)), ((), ()))

    def bf16_terms(v, n_terms):
        terms, rest = [], v
        for _ in range(n_terms):
            t = rest.astype(BF16).astype(F32)
            terms.append(t)
            rest = rest - t
        rows = -(-n_terms * n_col // BF16_SUBLANES) * BF16_SUBLANES
        if rows > n_terms * n_col:
            terms.append(jnp.zeros((rows - n_terms * n_col, v.shape[1]), F32))
        return jnp.concatenate(terms, axis=0).astype(BF16)

    fwd_rows = lax.broadcasted_iota(jnp.int32, (n_col, blk), 0) < SSD_HEADS

    def token_steps(dtf_blk_ref, dtb_blk_ref):
        dt_raw = jnp.where(fwd_rows, dtf_blk_ref[...].T[0:n_col], dtb_blk_ref[...].T[0:n_col])
        dt = _softplus(dt_raw + bias_ref[...])
        return dt, dt * -jnp.exp(alog_ref[...])

    def token_decay_sums(dt, a):
        r = lax.broadcasted_iota(jnp.int32, (blk, blk), 0)
        c = lax.broadcasted_iota(jnp.int32, (blk, blk), 1)
        tri_u = jnp.logical_and(r <= c, r // q == c // q).astype(F32)
        cs = jnp.dot(a, tri_u, preferred_element_type=F32, precision=lax.Precision.HIGHEST)
        return dt, a, cs

    def token_scalars(dt, a, cs):
        tot = jnp.concatenate([jnp.broadcast_to(cs[:, (u + 1) * q - 1:(u + 1) * q], (n_col, q))
                               for u in range(n_sub)], axis=1)
        e_t = jnp.where(fwd_rows, cs, cs - a)
        grow, shrink = jnp.exp(e_t), jnp.exp(tot - e_t)
        w_in = jnp.where(fwd_rows, grow, shrink)
        w_out = jnp.where(fwd_rows, shrink, grow)
        terms = jnp.concatenate(
            [bf16_terms(dt, 2), bf16_terms(w_in, 2), bf16_terms(dt * w_out, 2)], axis=1)
        spread = lax.dot_general(terms, spread_ref[...], contract0,
                                 preferred_element_type=F32)
        e_cols = lax.dot_general(bf16_terms(e_t, 3), ecol_ref[...], contract0,
                                 preferred_element_type=F32)
        return e_t, spread, e_cols

    def into_scratch(vals):
        et_sc[...], spread_sc[...], ecols_sc[...] = vals

    @pl.when(i == 0)
    def _():
        into_scratch(token_scalars(*token_decay_sums(*token_steps(dtf_ref, dtb_ref))))

    next_sums = token_decay_sums(*token_steps(dtf_next_ref, dtb_next_ref))

    zero = jnp.zeros((), BF16)

    def input_matmuls(d, u, xc):
        ch = slice(d * SSD_INNER, (d + 1) * SSD_INNER)
        tok = slice(u * q, (u + 1) * q)
        dt_x, w_in_x, dtw_x = (spread_sc[k * blk + u * q:k * blk + (u + 1) * q, ch] for k in range(3))
        x = xc[tok, 0:SSD_INNER]
        xdt = (x * dt_x).astype(BF16)
        xdtw = (x * dtw_x).astype(BF16)
        b_all = xc[tok, SSD_INNER:SSD_INNER + SSD_GROUPS * SSD_STATE].astype(BF16)
        c_all = xc[tok, SSD_INNER + SSD_GROUPS * SSD_STATE:].astype(BF16)
        hs = h_ref[d]
        y_off = jnp.dot(c_all, hs.astype(BF16), preferred_element_type=F32) * w_in_x
        cbs = [lax.dot_general(jnp.where(lane_half == g, c_all, zero), b_all,
                               (((1,), (1,)), ((), ())), preferred_element_type=F32)
               for g in range(SSD_GROUPS)]
        upd = lax.dot_general(b_all, xdtw, contract0, preferred_element_type=F32)
        exp_tot = w_in_x[q - 1:q, :] if d == 0 else w_in_x[0:1, :]
        h_ref[d] = exp_tot * hs + jnp.where(same_group, upd, 0.0)
        return x, xdt, y_off, cbs

    def decay_matmuls(d, u, x, xdt, y_off, cbs, y_ref):
        tok = slice(u * q, (u + 1) * q)
        for g in range(SSD_GROUPS):
            sl = slice(g * pair, (g + 1) * pair)
            y_g = y_off[:, sl] + dsk_ref[d:d + 1, sl] * x[:, sl]
            for hh in range(heads_per_group):
                c = d * SSD_HEADS + g * heads_per_group + hh
                e_col = ecols_sc[tok, c * q:(c + 1) * q]
                e_row = et_sc[c:c + 1, tok]
                if d == 0:
                    lmat = jnp.where(row >= col, jnp.exp(e_col - e_row), 0.0)
                else:
                    lmat = jnp.where(col >= row, jnp.exp(e_row - e_col), 0.0)
                x_h = jnp.where(lane_half == hh, xdt[:, sl], zero)
                y_g = y_g + jnp.dot((cbs[g] * lmat).astype(BF16), x_h,
                                    preferred_element_type=F32)
            y_ref[tok, sl] = y_g

    refs = ((xf_ref, yf_ref), (xb_ref, yb_ref))
    order = (tuple(range(n_sub)), tuple(reversed(range(n_sub))))
    lin = {}
    for k in range(n_sub):
        for d in range(SSD_DIRS):
            u = order[d][k]
            lin[d, u] = input_matmuls(d, u, refs[d][0])
    for k in range(n_sub):
        for d in range(SSD_DIRS):
            u = order[d][k]
            decay_matmuls(d, u, *lin[d, u], refs[d][1])
    into_scratch(token_scalars(*next_sums))


def _ssd(xc, dt_raw, dt_bias, a_log, d_skip, batch, n_lat, n_ctx):
    n_tok = xc.shape[0]
    q = SSD_CHUNK
    blk = SSD_STEP_CHUNKS * q
    lat_blk = n_lat // batch // blk
    ctx_blk = n_ctx // batch // blk
    assert lat_blk * blk * batch == n_lat and ctx_blk * blk * batch == n_ctx
    n_steps = lat_blk + ctx_blk

    def fwd(b, i):
        return jnp.where(i < ctx_blk, batch * lat_blk + b * ctx_blk + i, b * lat_blk + i - ctx_blk)

    def bwd(b, i):
        return jnp.where(i < ctx_blk, batch * lat_blk + b * ctx_blk + ctx_blk - 1 - i,
                         b * lat_blk + n_steps - 1 - i)

    def specs(ch):
        return [
            pl.BlockSpec((blk, SSD_XBC), lambda b, i: (ch(b, i), 0)),
            pl.BlockSpec((blk, LANES), lambda b, i: (ch(b, i), 0)),
            pl.BlockSpec((blk, LANES), lambda b, i: (ch(b, jnp.minimum(i + 1, n_steps - 1)), 0)),
        ]

    n_col = SSD_DIRS * SSD_HEADS
    rows = lambda v: jnp.broadcast_to(v.reshape(n_col, 1), (n_col, blk))
    dsk = jnp.repeat(d_skip, SSD_HEAD_DIM, axis=1)
    term_row = jnp.arange(2 * BF16_SUBLANES) % n_col
    live3 = jnp.arange(2 * BF16_SUBLANES) < 3 * n_col
    col_of_channel = jnp.arange(SSD_DIRS * SSD_INNER) // SSD_HEAD_DIM
    col_of_block = jnp.arange(n_col * q) // q
    spread = (term_row[:BF16_SUBLANES, None] == col_of_channel[None, :]).astype(BF16)
    ecol = ((term_row[:, None] == col_of_block[None, :]) & live3[:, None]).astype(BF16)
    return pl.pallas_call(
        _ssd_kernel,
        out_shape=[jax.ShapeDtypeStruct((n_tok, SSD_INNER), F32)] * 2,
        grid=(batch, n_steps),
        in_specs=specs(fwd) + specs(bwd) + [
            _const_spec((n_col, blk)), _const_spec((n_col, blk)), _const_spec((SSD_DIRS, SSD_INNER)),
            _const_spec(spread.shape), _const_spec(ecol.shape),
        ],
        out_specs=[pl.BlockSpec((blk, SSD_INNER), lambda b, i: (fwd(b, i), 0)),
                   pl.BlockSpec((blk, SSD_INNER), lambda b, i: (bwd(b, i), 0))],
        scratch_shapes=[pltpu.VMEM((SSD_DIRS, SSD_GROUPS * SSD_STATE, SSD_INNER), F32),
                        pltpu.VMEM((n_col, blk), F32),
                        pltpu.VMEM((3 * blk, SSD_DIRS * SSD_INNER), F32),
                        pltpu.VMEM((blk, n_col * q), F32)],
        compiler_params=pltpu.CompilerParams(
            dimension_semantics=("arbitrary", "arbitrary"), vmem_limit_bytes=VMEM_LIMIT),
        name="ssd_scan",
    )(xc, dt_raw, dt_raw, xc, dt_raw, dt_raw,
      rows(dt_bias), rows(a_log), dsk, spread, ecol)


def _steps_per_trip(n_steps, preferred):
    return max(u for u in range(2, preferred + 1, 2) if n_steps % u == 0) if n_steps else 2


def _scores_t(k, q2):
    return lax.dot_general(k, q2, (((1,), (1,)), ((), ())), preferred_element_type=F32)


def _softmax_stage(s, m_blk, m):
    if m is None:
        return m_blk, None, jnp.exp2((s - m_blk).astype(BF16))
    m_new = jnp.maximum(m, m_blk)
    return m_new, jnp.exp2(m - m_new), jnp.exp2((s - m_new).astype(BF16))


def _attn_kernel(q_ref, kc_ref, vtc_ref, kx_ref, vtx_ref, o_ref, s0_ref, s1_ref, p0_ref, p1_ref,
                 *, n_chunks):
    tq = q_ref.shape[0]
    tk = ATTN_K_TILE
    q2 = jnp.concatenate([q_ref[:, 0:HEAD_DIM], q_ref[:, HEAD_DIM:2 * HEAD_DIM]], axis=0)
    s_refs, p_refs = (s0_ref, s1_ref), (p0_ref, p1_ref)

    def scores_into(slot, c):
        s = _scores_t(kx_ref[pl.ds(pl.multiple_of(c * tk, tk), tk), :], q2)
        s_refs[slot][...] = s
        return jnp.max(s, axis=0, keepdims=True)

    def softmax_into(slot, m_blk, m):
        m, alpha, p = _softmax_stage(s_refs[slot][...], m_blk, m)
        p_refs[slot][...] = p
        return m, alpha

    def step(c, par, carry, with_scores, with_softmax):
        m, acc, alpha, m_blk = carry
        m_blk_next, alpha_next = m_blk, alpha
        if with_scores:
            m_blk_next = scores_into(par, c + 2)
        if with_softmax:
            m, alpha_next = softmax_into(1 - par, m_blk, m)
        acc = alpha * acc + jnp.dot(vtx_ref[c], p_refs[par][...], preferred_element_type=F32)
        return m, acc, alpha_next, m_blk_next

    if n_chunks:
        m_blk0 = scores_into(0, 0)
        m_blk1 = scores_into(1, 1)
    s_ctx = _scores_t(kc_ref[...], q2)
    m, _, p_ctx = _softmax_stage(s_ctx, jnp.max(s_ctx, axis=0, keepdims=True), None)
    acc = jnp.dot(vtc_ref[...], p_ctx, preferred_element_type=F32)
    if n_chunks:
        m, alpha = softmax_into(0, m_blk0, m)
        carry = (m, acc, alpha, m_blk1)
        unroll = _steps_per_trip(n_chunks - 2, ATTN_UNROLL)

        def body(i, carry):
            for u in range(unroll):
                carry = step(unroll * i + u, u % 2, carry, True, True)
            return carry

        carry = lax.fori_loop(0, (n_chunks - 2) // unroll, body, carry)
        carry = step(n_chunks - 2, 0, carry, False, True)
        _, acc, _, _ = step(n_chunks - 1, 1, carry, False, False)
    o = (acc[0:HEAD_DIM] / acc[HEAD_DIM:HEAD_DIM + 1]).T.astype(o_ref.dtype)
    o_ref[:, 0:HEAD_DIM] = o[0:tq]
    o_ref[:, HEAD_DIM:2 * HEAD_DIM] = o[tq:2 * tq]


def _attn_bounded_kernel(q_ref, kc_ref, vtc_ref, kx_ref, vtx_ref, o_ref, p0_ref, p1_ref, *, n_chunks):
    tq = q_ref.shape[0]
    tk = ATTN_K_TILE
    q2 = jnp.concatenate([q_ref[:, 0:HEAD_DIM], q_ref[:, HEAD_DIM:2 * HEAD_DIM]], axis=0)
    p_refs = (p0_ref, p1_ref)

    def colsum(p):
        return jnp.sum(p.reshape(p.shape[0] // SUBLANES, SUBLANES, p.shape[1]), axis=0)

    def weights_into(slot, c):
        p = jnp.exp2(_scores_t(kx_ref[pl.ds(pl.multiple_of(c * tk, tk), tk), :], q2))
        p_refs[slot][...] = p.astype(BF16)
        return colsum(p)

    def step(c, par, carry, with_scores):
        l8, acc = carry
        if with_scores:
            l8 = l8 + weights_into(1 - par, c + 1)
        acc = acc + jnp.dot(vtx_ref[c, 0:HEAD_DIM, :], p_refs[par][...],
                            preferred_element_type=F32)
        return l8, acc

    p_ctx = jnp.exp2(_scores_t(kc_ref[...], q2))
    l8 = colsum(p_ctx) + weights_into(0, 0)
    acc = jnp.dot(vtc_ref[0:HEAD_DIM, :], p_ctx.astype(BF16), preferred_element_type=F32)
    unroll = _steps_per_trip(n_chunks - 2, ATTN_BOUNDED_UNROLL)

    def body(i, carry):
        for u in range(unroll):
            carry = step(unroll * i + u, u % 2, carry, True)
        return carry

    carry = lax.fori_loop(0, (n_chunks - 2) // unroll, body, (l8, acc))
    carry = step(n_chunks - 2, 0, carry, True)
    l8, acc = step(n_chunks - 1, 1, carry, False)
    o = (acc / jnp.sum(l8, axis=0, keepdims=True)).T.astype(o_ref.dtype)
    o_ref[:, 0:HEAD_DIM] = o[0:tq]
    o_ref[:, HEAD_DIM:2 * HEAD_DIM] = o[tq:2 * tq]


def _attention_latent(q, k, vt, batch, n_lat, n_ctx, bounded):
    seq = n_lat // batch
    ctx = n_ctx // batch
    tq, tk = ATTN_Q_TILE, ATTN_K_TILE
    nq = seq // tq
    n_chunks = seq // tk
    assert seq % (2 * tk) == 0 and n_lat % ctx == 0 and n_ctx == tk and vt.shape[2] == tk
    r = 2 * tq
    body = _attn_bounded_kernel if bounded else _attn_kernel
    p_bufs = [pltpu.VMEM((tk, r), BF16), pltpu.VMEM((tk, r), BF16)]
    s_bufs = [] if bounded else [pltpu.VMEM((tk, r), F32), pltpu.VMEM((tk, r), F32)]
    return pl.pallas_call(
        functools.partial(body, n_chunks=n_chunks),
        out_shape=jax.ShapeDtypeStruct((n_lat, ATTN_DIM), BF16),
        grid=(batch, ATTN_KV_HEADS, nq),
        in_specs=[pl.BlockSpec((tq, 2 * HEAD_DIM), lambda b, g, j: (b * nq + j, g)),
                  pl.BlockSpec((ctx, HEAD_DIM), lambda b, g, j: (n_lat // ctx + b, g)),
                  pl.BlockSpec((None, VT_ROWS, ctx), lambda b, g, j: (n_lat // tk, g, b)),
                  pl.BlockSpec((seq, HEAD_DIM), lambda b, g, j: (b, g)),
                  pl.BlockSpec((n_chunks, VT_ROWS, tk), lambda b, g, j: (b, g, 0))],
        out_specs=pl.BlockSpec((tq, 2 * HEAD_DIM), lambda b, g, j: (b * nq + j, g)),
        scratch_shapes=s_bufs + p_bufs,
        compiler_params=pltpu.CompilerParams(
            dimension_semantics=("arbitrary",) * 3, vmem_limit_bytes=VMEM_LIMIT),
        name="attn_latent_bounded" if bounded else "attn_latent",
    )(q, k, vt, k, vt)


def _attn_ctx_kernel(q_ref, kc_ref, vtc_ref, o_ref):
    _attn_kernel(q_ref, kc_ref, vtc_ref, None, None, o_ref, None, None, None, None, n_chunks=0)


def _attention_context(q, k, vt, batch, n_lat, n_ctx):
    ctx = n_ctx // batch
    tk = ATTN_K_TILE
    return pl.pallas_call(
        _attn_ctx_kernel,
        out_shape=jax.ShapeDtypeStruct((n_ctx, ATTN_DIM), BF16),
        grid=(batch, ATTN_KV_HEADS),
        in_specs=[pl.BlockSpec((ctx, 2 * HEAD_DIM), lambda b, g: (n_lat // ctx + b, g)),
                  pl.BlockSpec((ctx, HEAD_DIM), lambda b, g: (n_lat // ctx + b, g)),
                  pl.BlockSpec((None, VT_ROWS, ctx), lambda b, g: (n_lat // tk, g, b))],
        out_specs=pl.BlockSpec((ctx, 2 * HEAD_DIM), lambda b, g: (b, g)),
        compiler_params=pltpu.CompilerParams(
            dimension_semantics=("arbitrary",) * 2, vmem_limit_bytes=VMEM_LIMIT),
        name="attn_context",
    )(q, k, vt)


def _post_kernel(xs_ref, up_ref, upp_ref, upn_ref, z_ref, yf_ref, yb_ref, at_ref, mod_ref,
                 pw_ref, ps_ref, sw_ref, wo_ref, n2_ref, w1_ref, w3_ref, w2_ref,
                 o_ref, buf_ref, *, tiles_per_seq, seq_len):
    i = pl.program_id(0)
    tm = xs_ref.shape[0]
    halo = SUBLANES
    j = i % tiles_per_seq

    buf_ref[0:halo, :] = jnp.where(j == 0, 0.0, upp_ref[...])
    buf_ref[halo:halo + tm, :] = up_ref[...]
    buf_ref[halo + tm:2 * halo + tm, :] = jnp.where(j == tiles_per_seq - 1, 0.0, upn_ref[...])
    ub = buf_ref[...]
    n = tm + 2 * halo
    s2 = ub + pltpu.roll(ub, 1, 0)
    s4 = pltpu.roll(s2, 1, 0) + pltpu.roll(s2, n - 1, 0)
    s8 = pltpu.roll(s4, 2, 0) + pltpu.roll(s4, n - 2, 0)
    s16 = pltpu.roll(s8, 4, 0) + pltpu.roll(s8, n - 4, 0)
    shape = (tm, POOL_DIM)
    grp = lax.broadcasted_iota(jnp.int32, shape, 1) // POOL_GDIM
    t = lax.broadcasted_iota(jnp.int32, shape, 0) + j * tm
    half = jnp.left_shift(1, grp)
    cnt = (jnp.minimum(t + half, seq_len) - jnp.maximum(t - half, 0)).astype(F32)
    sl = slice(halo, halo + tm)
    wsum = jnp.where(grp == 0, s2[sl], jnp.where(grp == 1, s4[sl],
                                                 jnp.where(grp == 2, s8[sl], s16[sl])))
    pooled = wsum / cnt - ub[sl]
    pool = jnp.dot(pooled.astype(BF16), pw_ref[...], preferred_element_type=F32) * ps_ref[...]

    gy = (yf_ref[...] + yb_ref[...]) * _silu(z_ref[...])
    gw = SSD_INNER // SSD_GROUPS
    parts = []
    for g in range(SSD_GROUPS):
        t_g = gy[:, g * gw:(g + 1) * gw]
        parts.append(t_g * lax.rsqrt(jnp.mean(t_g * t_g, axis=-1, keepdims=True) + EPS))
    ssd = jnp.concatenate(parts, axis=1) * sw_ref[...]

    mix = jnp.concatenate([pool.astype(BF16), ssd.astype(BF16), at_ref[...]], axis=1)
    blocks = [slice(r0, r0 + min(POST_ROWS, tm)) for r0 in range(0, tm, POST_ROWS)]
    x1 = [xs_ref[rows, :] + mod_ref[2:3, :] * jnp.dot(mix[rows], wo_ref[...],
                                                       preferred_element_type=F32)
          for rows in blocks]

    def modulated_norm(v):
        h = v * lax.rsqrt(jnp.mean(v * v, axis=-1, keepdims=True) + EPS) * n2_ref[...]
        return (h * (1.0 + mod_ref[4:5, :]) + mod_ref[3:4, :]).astype(BF16)

    gated = []
    for v in x1:
        h = modulated_norm(v)
        a = jnp.dot(h, w1_ref[...], preferred_element_type=F32)
        b = jnp.dot(h, w3_ref[...], preferred_element_type=F32)
        gated.append((_silu(a) * b).astype(BF16))
    for rows, v, gt in zip(blocks, x1, gated):
        ff = jnp.dot(gt, w2_ref[...], preferred_element_type=F32)
        o_ref[rows, :] = v + mod_ref[5:6, :] * ff


def _post(xs, upool, z, yf, yb, attn, mod, pw_bd, pscale, ssd_w, w_out, n2, w1, w3, w2,
          *, tm, row0, tiles_per_seq, mod_row):
    n_rows, d = xs.shape
    n_tiles = n_rows // tm
    off = row0 // tm
    blk8 = tm // SUBLANES
    last8 = upool.shape[0] // SUBLANES - 1
    tok = lambda w: pl.BlockSpec((tm, w), lambda i: (off + i, 0))
    loc = lambda w: pl.BlockSpec((tm, w), lambda i: (i, 0))
    return pl.pallas_call(
        functools.partial(_post_kernel, tiles_per_seq=tiles_per_seq, seq_len=tiles_per_seq * tm),
        out_shape=jax.ShapeDtypeStruct((n_rows, d), F32),
        grid=(n_tiles,),
        in_specs=[
            loc(d),
            tok(POOL_DIM),
            pl.BlockSpec((SUBLANES, POOL_DIM), lambda i: (jnp.maximum((off + i) * blk8 - 1, 0), 0)),
            pl.BlockSpec((SUBLANES, POOL_DIM),
                         lambda i: (jnp.minimum((off + i) * blk8 + blk8, last8), 0)),
            tok(SSD_INNER), tok(SSD_INNER), tok(SSD_INNER),
            loc(ATTN_DIM),
            pl.BlockSpec((None, N_MOD, d), lambda i: (mod_row(i), 0, 0)),
            _const_spec(pw_bd.shape), _const_spec((1, POOL_DIM)), _const_spec((1, SSD_INNER)),
            _const_spec(w_out.shape), _const_spec((1, d)),
            _const_spec(w1.shape), _const_spec(w3.shape), _const_spec(w2.shape),
        ],
        out_specs=loc(d),
        scratch_shapes=[pltpu.VMEM((tm + 2 * SUBLANES, POOL_DIM), F32)],
        compiler_params=pltpu.CompilerParams(
            dimension_semantics=("arbitrary",), vmem_limit_bytes=VMEM_LIMIT),
        name="post_mix_ffn",
    )(xs, upool, upool, upool, z, yf, yb, attn, mod, pw_bd, pscale, ssd_w, w_out, n2, w1, w3, w2)


def _rope_tables(seq, extra_rows):
    rows, pad_rows = seq // GRID_W, extra_rows // GRID_W
    axis_dim = HEAD_DIM // 2
    inv_freq = ROPE_THETA ** (-jnp.arange(0, axis_dim, 2, dtype=F32) / axis_dim)
    ang_r = jnp.arange(rows, dtype=F32)[:, None] * inv_freq[None, :]
    ang_c = jnp.arange(GRID_W, dtype=F32)[:, None] * inv_freq[None, :]
    live = (jnp.arange(rows + pad_rows) < rows)[:, None, None]
    shape = (rows + pad_rows, GRID_W, axis_dim // 2)

    def per_row(t, fill):
        t = jnp.concatenate([t, jnp.full((pad_rows, t.shape[1]), fill, F32)], axis=0)
        return jnp.broadcast_to(t[:, None, :], shape)

    def per_col(t, fill):
        return jnp.where(live, jnp.broadcast_to(t[None, :, :], shape), fill)

    cr, sr = per_row(jnp.cos(ang_r), 1.0), per_row(jnp.sin(ang_r), 0.0)
    cc, sc = per_col(jnp.cos(ang_c), 1.0), per_col(jnp.sin(ang_c), 0.0)
    cos = jnp.concatenate([cr, cr, cc, cc], axis=-1).reshape(-1, HEAD_DIM)
    sin = jnp.concatenate([-sr, sr, -sc, sc], axis=-1).reshape(-1, HEAD_DIM)
    return cos, sin


def _fuse_w_in(w_in):
    d = w_in.shape[0]
    o_dt = POOL_DIM + SSD_INNER + SSD_XBC
    o_att = o_dt + SSD_DIRS * SSD_HEADS
    dt_pad = jnp.zeros((d, C_Q - C_DT - SSD_DIRS * SSD_HEADS), w_in.dtype)
    return jnp.concatenate([w_in[:, :o_dt], w_in[:, o_dt:o_att], dt_pad, w_in[:, o_att:]],
                           axis=1).astype(BF16)


def kernel(x, c, ctx, c_ctx, norm1_w, norm2_w, w_mod, b_mod, w_in, pool_w, pool_scale, conv_w, conv_b,
           dt_bias, a_log, d_skip, ssd_norm_w, q_norm_w, k_norm_w, w_out, w1, w3, w2):
    batch, seq, d = x.shape
    ctx_len = ctx.shape[1]
    depth = w_mod.shape[0]
    n_lat, n_ctx = batch * seq, batch * ctx_len
    assert batch + 1 <= SUBLANES and seq % TOKEN_TILE == 0 and seq % GRID_W == 0

    cond = jnp.zeros((SUBLANES, d), F32).at[:batch].set(c).at[batch].set(c_ctx)
    mod_all = _modulation(cond, w_mod, b_mod).reshape(depth, SUBLANES, N_MOD, d)
    cos_tab, sin_tab = _rope_tables(seq, TOKEN_TILE)

    xs_lat = x.reshape(n_lat, d)
    xs_ctx = ctx.reshape(n_ctx, d)
    tiles_per_seq = seq // TOKEN_TILE
    for layer in range(depth):
        need_ctx = layer < depth - 1
        mod = mod_all[layer]
        w_cat = _fuse_w_in(w_in[layer])
        upool, z, xbc, dt_raw, q, k, vt = _inproj(
            xs_lat, xs_ctx, mod, norm1_w[layer].reshape(1, d), w_cat,
            q_norm_w[layer].reshape(1, -1), k_norm_w[layer].reshape(1, -1), cos_tab, sin_tab, batch)
        xc = _ssd_conv(xbc, conv_w[layer], conv_b[layer], batch, n_lat, n_ctx)
        yf, yb = _ssd(xc, dt_raw, dt_bias[layer], a_log[layer], d_skip[layer], batch, n_lat, n_ctx)
        score_bound = (HEAD_DIM * Q_SCALE * SCORE_BOUND_SLACK * jnp.max(jnp.abs(q_norm_w[layer]))
                       * jnp.max(jnp.abs(k_norm_w[layer])))
        attn_x = lax.cond(
            score_bound < SCORE_BOUND_LIMIT,
            functools.partial(_attention_latent, batch=batch, n_lat=n_lat, n_ctx=n_ctx, bounded=True),
            functools.partial(_attention_latent, batch=batch, n_lat=n_lat, n_ctx=n_ctx, bounded=False),
            q, k, vt)
        pw_bd = jax.scipy.linalg.block_diag(*[pool_w[layer, g] for g in range(POOL_GROUPS)]).astype(BF16)
        post = functools.partial(
            _post, upool=upool, z=z, yf=yf, yb=yb, mod=mod, pw_bd=pw_bd,
            pscale=pool_scale[layer].reshape(1, -1), ssd_w=ssd_norm_w[layer].reshape(1, -1),
            w_out=w_out[layer].astype(BF16), n2=norm2_w[layer].reshape(1, d),
            w1=w1[layer].astype(BF16), w3=w3[layer].astype(BF16), w2=w2[layer].astype(BF16))
        new_lat = post(xs_lat, attn=attn_x, tm=TOKEN_TILE, row0=0, tiles_per_seq=tiles_per_seq,
                       mod_row=lambda i: i // tiles_per_seq)
        if need_ctx:
            attn_c = _attention_context(q, k, vt, batch, n_lat, n_ctx)
            xs_ctx = post(xs_ctx, attn=attn_c, tm=ctx_len, row0=n_lat, tiles_per_seq=1,
                          mod_row=lambda i: batch)
        xs_lat = new_lat
    return xs_lat.reshape(batch, seq, d)
```

```python
import functools

import jax
import jax.numpy as jnp
from jax import lax
from jax.experimental import pallas as pl
from jax.experimental.pallas import tpu as pltpu

F32 = jnp.float32
BF16 = jnp.bfloat16

GRID_W = 64
EPS = 1e-6
N_MOD = 6
POOL_DIM = 256
POOL_GDIM = 64
POOL_GROUPS = 4
SSD_HEADS = 4
SSD_HEAD_DIM = 64
SSD_INNER = 256
SSD_GROUPS = 2
SSD_STATE = 64
SSD_CONV = 5
SSD_CHUNK = 128
SSD_STEP_CHUNKS = 2
SSD_DIRS = 2
SSD_XBC = 512
ATTN_HEADS = 4
ATTN_KV_HEADS = 2
HEAD_DIM = 128
ATTN_DIM = 512
KV_DIM = 256
ROPE_THETA = 10000.0
Q_SCALE = HEAD_DIM ** -0.5 * 1.4426950408889634
SCORE_BOUND_LIMIT = 60.0
SCORE_BOUND_SLACK = 1.02

LANES = 128
SUBLANES = 8
TOKEN_TILE = 512
INPROJ_ROWS = 256
CONV_TILE = 1024
POST_ROWS = 256
ATTN_Q_TILE = 512
ATTN_K_TILE = 512
ATTN_UNROLL = 6
ATTN_BOUNDED_UNROLL = 30
BF16_SUBLANES = 16
VT_ROWS = HEAD_DIM + BF16_SUBLANES
VMEM_LIMIT = 56 * 1024 * 1024

C_POOL = 0
C_Z = C_POOL + POOL_DIM
C_XBC = C_Z + SSD_INNER
C_DT = C_XBC + SSD_XBC
C_Q = C_DT + LANES
C_K = C_Q + ATTN_DIM
C_V = C_K + KV_DIM
C_END = C_V + KV_DIM


def _silu(x):
    return x * (1.0 / (1.0 + jnp.exp(-x)))


def _softplus(x):
    return jnp.maximum(x, 0.0) + jnp.log1p(jnp.exp(-jnp.abs(x)))


def _const_spec(shape):
    nd = len(shape)
    return pl.BlockSpec(shape, lambda *_: (0,) * nd, pipeline_mode=pl.Buffered(1))


def _mod_kernel(cond_ref, w_ref, b_ref, o_ref):
    s = _silu(cond_ref[...]).astype(BF16)
    o_ref[...] = jnp.dot(s, w_ref[...].astype(BF16), preferred_element_type=F32) + b_ref[...]


def _modulation(cond, w_mod, b_mod):
    depth, d, n = w_mod.shape
    bn = d
    return pl.pallas_call(
        _mod_kernel,
        out_shape=jax.ShapeDtypeStruct((depth, SUBLANES, n), F32),
        grid=(depth, n // bn),
        in_specs=[
            pl.BlockSpec((SUBLANES, d), lambda l, j: (0, 0)),
            pl.BlockSpec((None, d, bn), lambda l, j: (l, 0, j)),
            pl.BlockSpec((None, 1, bn), lambda l, j: (l, 0, j)),
        ],
        out_specs=pl.BlockSpec((None, SUBLANES, bn), lambda l, j: (l, 0, j)),
        compiler_params=pltpu.CompilerParams(
            dimension_semantics=("arbitrary", "arbitrary"), vmem_limit_bytes=VMEM_LIMIT),
        name="modulation",
    )(cond, w_mod, b_mod.reshape(depth, 1, n))


def _rope_partner(x):
    lane = lax.broadcasted_iota(jnp.int32, x.shape, 1)
    fwd = pltpu.roll(x, LANES - 32, 1)
    bwd = pltpu.roll(x, 32, 1)
    return jnp.where((lane % 64) < 32, fwd, bwd)


def _inproj_kernel(xl_ref, xc_ref, mod_ref, n1_ref, w_ref, qw_ref, kw_ref, cos_ref, sin_ref,
                   pool_ref, z_ref, xbc_ref, dt_ref, q_ref, k_ref, vt_ref, *, n_lat_tiles):
    i = pl.program_id(0)
    is_lat = i < n_lat_tiles
    tm = xl_ref.shape[0]
    for r0 in range(0, tm, INPROJ_ROWS):
        rows = slice(r0, r0 + INPROJ_ROWS)
        x = jnp.where(is_lat, xl_ref[rows, :], xc_ref[rows, :])
        ms = jnp.mean(x * x, axis=-1, keepdims=True)
        h = x * lax.rsqrt(ms + EPS) * n1_ref[...]
        h = h * (1.0 + mod_ref[1:2, :]) + mod_ref[0:1, :]
        u = jnp.dot(h.astype(BF16), w_ref[...], preferred_element_type=F32)
        pool_ref[rows, :] = u[:, C_POOL:C_Z]
        z_ref[rows, :] = u[:, C_Z:C_XBC]
        xbc_ref[rows, :] = u[:, C_XBC:C_DT]
        dt_ref[rows, :] = u[:, C_DT:C_Q]
        vt = u[:, C_V:C_END].T.astype(BF16)
        for hd in range(ATTN_KV_HEADS):
            vt_ref[hd * VT_ROWS:hd * VT_ROWS + HEAD_DIM, rows] = vt[hd * HEAD_DIM:(hd + 1) * HEAD_DIM]
            vt_ref[hd * VT_ROWS + HEAD_DIM:(hd + 1) * VT_ROWS, rows] = jnp.ones(
                (VT_ROWS - HEAD_DIM, INPROJ_ROWS), BF16)
        cos = cos_ref[rows, :]
        sin = sin_ref[rows, :]

        def norm_rope(t, w):
            t = t * lax.rsqrt(jnp.mean(t * t, axis=-1, keepdims=True) + EPS) * w
            return t * cos + _rope_partner(t) * sin

        for hd in range(ATTN_HEADS):
            t = norm_rope(u[:, C_Q + hd * HEAD_DIM:C_Q + (hd + 1) * HEAD_DIM], qw_ref[...])
            q_ref[rows, hd * HEAD_DIM:(hd + 1) * HEAD_DIM] = (t * Q_SCALE).astype(BF16)
        for hd in range(ATTN_KV_HEADS):
            t = norm_rope(u[:, C_K + hd * HEAD_DIM:C_K + (hd + 1) * HEAD_DIM], kw_ref[...])
            k_ref[rows, hd * HEAD_DIM:(hd + 1) * HEAD_DIM] = t.astype(BF16)


def _inproj(xs_lat, xs_ctx, mod, n1, w_cat, qw, kw, cos_tab, sin_tab, batch):
    n_lat, d = xs_lat.shape
    n_ctx = xs_ctx.shape[0]
    tm = TOKEN_TILE
    assert n_ctx == tm and n_lat % (batch * tm) == 0
    n_lat_tiles = n_lat // tm
    tiles_per_seq = n_lat_tiles // batch
    n_tok = n_lat + n_ctx
    grid = (n_lat_tiles + 1,)

    def lat_idx(i):
        return jnp.minimum(i, n_lat_tiles - 1)

    def rope_idx(i):
        return jnp.where(i < n_lat_tiles, i % tiles_per_seq, tiles_per_seq)

    def mod_idx(i):
        return jnp.where(i < n_lat_tiles, i // tiles_per_seq, batch)

    row = lambda w: pl.BlockSpec((tm, w), lambda i: (i, 0))
    outs = [(POOL_DIM, F32), (SSD_INNER, F32), (SSD_XBC, F32), (LANES, F32),
            (ATTN_DIM, BF16), (KV_DIM, BF16)]
    return pl.pallas_call(
        functools.partial(_inproj_kernel, n_lat_tiles=n_lat_tiles),
        out_shape=[jax.ShapeDtypeStruct((n_tok, w), t) for w, t in outs]
        + [jax.ShapeDtypeStruct((n_tok // tm, ATTN_KV_HEADS * VT_ROWS, tm), BF16)],
        grid=grid,
        in_specs=[
            pl.BlockSpec((tm, d), lambda i: (lat_idx(i), 0)),
            pl.BlockSpec((tm, d), lambda i: (0, 0)),
            pl.BlockSpec((None, N_MOD, d), lambda i: (mod_idx(i), 0, 0)),
            _const_spec((1, d)),
            _const_spec(w_cat.shape),
            _const_spec((1, HEAD_DIM)),
            _const_spec((1, HEAD_DIM)),
            pl.BlockSpec((tm, HEAD_DIM), lambda i: (rope_idx(i), 0)),
            pl.BlockSpec((tm, HEAD_DIM), lambda i: (rope_idx(i), 0)),
        ],
        out_specs=[row(w) for w, _ in outs]
        + [pl.BlockSpec((None, ATTN_KV_HEADS * VT_ROWS, tm), lambda i: (i, 0, 0))],
        compiler_params=pltpu.CompilerParams(
            dimension_semantics=("arbitrary",), vmem_limit_bytes=VMEM_LIMIT),
        name="inproj",
    )(xs_lat, xs_ctx, mod, n1, w_cat, qw, kw, cos_tab, sin_tab)


def _conv_kernel(x_ref, prev_ref, next_ref, ctx_ref, cw_ref, cb_ref, o_ref, buf_ref,
                 *, tiles_per_seq, n_lat_tiles, ctx_len):
    i = pl.program_id(0)
    tm = x_ref.shape[0]
    pad = SSD_CONV // 2

    def conv_silu(rows):
        u = buf_ref[0:rows + 2 * SUBLANES, :]
        n = u.shape[0]
        acc = cb_ref[...] + cw_ref[pad:pad + 1, :] * u[SUBLANES:SUBLANES + rows]
        for k in range(SSD_CONV):
            if k != pad:
                shifted = pltpu.roll(u, (pad - k) % n, 0)
                acc = acc + cw_ref[k:k + 1, :] * shifted[SUBLANES:SUBLANES + rows]
        return _silu(acc)

    @pl.when(i < n_lat_tiles)
    def _():
        first = i % tiles_per_seq == 0
        last = i % tiles_per_seq == tiles_per_seq - 1
        buf_ref[0:SUBLANES, :] = jnp.where(first, 0.0, prev_ref[...])
        buf_ref[SUBLANES:SUBLANES + tm, :] = x_ref[...]
        buf_ref[SUBLANES + tm:2 * SUBLANES + tm, :] = jnp.where(last, 0.0, next_ref[...])
        o_ref[...] = conv_silu(tm)

    @pl.when(i == n_lat_tiles)
    def _():
        n_ctx = ctx_ref.shape[0]
        halo = jnp.zeros((SUBLANES, SSD_XBC), F32)
        for r0 in range(0, n_ctx, ctx_len):
            buf_ref[0:SUBLANES, :] = halo
            buf_ref[SUBLANES:SUBLANES + ctx_len, :] = ctx_ref[r0:r0 + ctx_len, :]
            buf_ref[SUBLANES + ctx_len:2 * SUBLANES + ctx_len, :] = halo
            o_ref[r0:r0 + ctx_len, :] = conv_silu(ctx_len)
        o_ref[n_ctx:tm, :] = jnp.zeros((tm - n_ctx, SSD_XBC), F32)


def _ssd_conv(xbc, conv_w, conv_b, batch, n_lat, n_ctx):
    n_tok = xbc.shape[0]
    seq, ctx = n_lat // batch, n_ctx // batch
    tm = CONV_TILE
    assert seq % tm == 0 and n_lat % n_ctx == 0 and n_ctx < tm
    n_lat_tiles = n_lat // tm
    blk8 = tm // SUBLANES
    lat = lambda i: jnp.minimum(i, n_lat_tiles - 1)
    cw = jnp.zeros((SUBLANES, SSD_XBC), F32).at[:SSD_CONV].set(conv_w)
    return pl.pallas_call(
        functools.partial(_conv_kernel, tiles_per_seq=seq // tm, n_lat_tiles=n_lat_tiles,
                          ctx_len=ctx),
        out_shape=jax.ShapeDtypeStruct((n_tok, SSD_XBC), F32),
        grid=(n_lat_tiles + 1,),
        in_specs=[
            pl.BlockSpec((tm, SSD_XBC), lambda i: (lat(i), 0)),
            pl.BlockSpec((SUBLANES, SSD_XBC), lambda i: (jnp.maximum(lat(i) * blk8 - 1, 0), 0)),
            pl.BlockSpec((SUBLANES, SSD_XBC), lambda i: (lat(i) * blk8 + blk8, 0)),
            pl.BlockSpec((n_ctx, SSD_XBC), lambda i: (n_lat // n_ctx, 0)),
            _const_spec((SUBLANES, SSD_XBC)), _const_spec((1, SSD_XBC)),
        ],
        out_specs=pl.BlockSpec((tm, SSD_XBC), lambda i: (i, 0)),
        scratch_shapes=[pltpu.VMEM((tm + 2 * SUBLANES, SSD_XBC), F32)],
        compiler_params=pltpu.CompilerParams(
            dimension_semantics=("arbitrary",), vmem_limit_bytes=VMEM_LIMIT),
        name="ssd_conv",
    )(xbc, xbc, xbc, xbc, cw, conv_b.reshape(1, -1))


def _ssd_kernel(xf_ref, dtf_ref, dtf_next_ref, xb_ref, dtb_ref, dtb_next_ref,
                bias_ref, alog_ref, dsk_ref, spread_ref, ecol_ref,
                yf_ref, yb_ref, h_ref, et_sc, spread_sc, ecols_sc):
    i = pl.program_id(1)
    q = SSD_CHUNK
    n_sub = SSD_STEP_CHUNKS
    blk = n_sub * q

    @pl.when(i == 0)
    def _():
        h_ref[...] = jnp.zeros_like(h_ref)

    row = lax.broadcasted_iota(jnp.int32, (q, q), 0)
    col = lax.broadcasted_iota(jnp.int32, (q, q), 1)
    n_col = SSD_DIRS * SSD_HEADS
    heads_per_group = SSD_HEADS // SSD_GROUPS
    pair = heads_per_group * SSD_HEAD_DIM
    lane_half = col // SSD_HEAD_DIM
    st_row = lax.broadcasted_iota(jnp.int32, (SSD_GROUPS * SSD_STATE, SSD_INNER), 0)
    st_col = lax.broadcasted_iota(jnp.int32, (SSD_GROUPS * SSD_STATE, SSD_INNER), 1)
    same_group = (st_row // SSD_STATE) == (st_col // pair)
    contract0 = (((0,), (0,)), ((), ()))

    def bf16_terms(v, n_terms):
        terms, rest = [], v
        for _ in range(n_terms):
            t = rest.astype(BF16).astype(F32)
            terms.append(t)
            rest = rest - t
        rows = -(-n_terms * n_col // BF16_SUBLANES) * BF16_SUBLANES
        if rows > n_terms * n_col:
            terms.append(jnp.zeros((rows - n_terms * n_col, v.shape[1]), F32))
        return jnp.concatenate(terms, axis=0).astype(BF16)

    fwd_rows = lax.broadcasted_iota(jnp.int32, (n_col, blk), 0) < SSD_HEADS

    def token_steps(dtf_blk_ref, dtb_blk_ref):
        dt_raw = jnp.where(fwd_rows, dtf_blk_ref[...].T[0:n_col], dtb_blk_ref[...].T[0:n_col])
        dt = _softplus(dt_raw + bias_ref[...])
        return dt, dt * -jnp.exp(alog_ref[...])

    def token_decay_sums(dt, a):
        r = lax.broadcasted_iota(jnp.int32, (blk, blk), 0)
        c = lax.broadcasted_iota(jnp.int32, (blk, blk), 1)
        tri_u = jnp.logical_and(r <= c, r // q == c // q).astype(F32)
        cs = jnp.dot(a, tri_u, preferred_element_type=F32, precision=lax.Precision.HIGHEST)
        return dt, a, cs

    def token_scalars(dt, a, cs):
        tot = jnp.concatenate([jnp.broadcast_to(cs[:, (u + 1) * q - 1:(u + 1) * q], (n_col, q))
                               for u in range(n_sub)], axis=1)
        e_t = jnp.where(fwd_rows, cs, cs - a)
        grow, shrink = jnp.exp(e_t), jnp.exp(tot - e_t)
        w_in = jnp.where(fwd_rows, grow, shrink)
        w_out = jnp.where(fwd_rows, shrink, grow)
        terms = jnp.concatenate(
            [bf16_terms(dt, 2), bf16_terms(w_in, 2), bf16_terms(dt * w_out, 2)], axis=1)
        spread = lax.dot_general(terms, spread_ref[...], contract0,
                                 preferred_element_type=F32)
        e_cols = lax.dot_general(bf16_terms(e_t, 3), ecol_ref[...], contract0,
                                 preferred_element_type=F32)
        return e_t, spread, e_cols

    def into_scratch(vals):
        et_sc[...], spread_sc[...], ecols_sc[...] = vals

    @pl.when(i == 0)
    def _():
        into_scratch(token_scalars(*token_decay_sums(*token_steps(dtf_ref, dtb_ref))))

    next_sums = token_decay_sums(*token_steps(dtf_next_ref, dtb_next_ref))

    zero = jnp.zeros((), BF16)

    def input_matmuls(d, u, xc):
        ch = slice(d * SSD_INNER, (d + 1) * SSD_INNER)
        tok = slice(u * q, (u + 1) * q)
        dt_x, w_in_x, dtw_x = (spread_sc[k * blk + u * q:k * blk + (u + 1) * q, ch] for k in range(3))
        x = xc[tok, 0:SSD_INNER]
        xdt = (x * dt_x).astype(BF16)
        xdtw = (x * dtw_x).astype(BF16)
        b_all = xc[tok, SSD_INNER:SSD_INNER + SSD_GROUPS * SSD_STATE].astype(BF16)
        c_all = xc[tok, SSD_INNER + SSD_GROUPS * SSD_STATE:].astype(BF16)
        hs = h_ref[d]
        y_off = jnp.dot(c_all, hs.astype(BF16), preferred_element_type=F32) * w_in_x
        cbs = [lax.dot_general(jnp.where(lane_half == g, c_all, zero), b_all,
                               (((1,), (1,)), ((), ())), preferred_element_type=F32)
               for g in range(SSD_GROUPS)]
        upd = lax.dot_general(b_all, xdtw, contract0, preferred_element_type=F32)
        exp_tot = w_in_x[q - 1:q, :] if d == 0 else w_in_x[0:1, :]
        h_ref[d] = exp_tot * hs + jnp.where(same_group, upd, 0.0)
        return x, xdt, y_off, cbs

    def decay_matmuls(d, u, x, xdt, y_off, cbs, y_ref):
        tok = slice(u * q, (u + 1) * q)
        for g in range(SSD_GROUPS):
            sl = slice(g * pair, (g + 1) * pair)
            y_g = y_off[:, sl] + dsk_ref[d:d + 1, sl] * x[:, sl]
            for hh in range(heads_per_group):
                c = d * SSD_HEADS + g * heads_per_group + hh
                e_col = ecols_sc[tok, c * q:(c + 1) * q]
                e_row = et_sc[c:c + 1, tok]
                if d == 0:
                    lmat = jnp.where(row >= col, jnp.exp(e_col - e_row), 0.0)
                else:
                    lmat = jnp.where(col >= row, jnp.exp(e_row - e_col), 0.0)
                x_h = jnp.where(lane_half == hh, xdt[:, sl], zero)
                y_g = y_g + jnp.dot((cbs[g] * lmat).astype(BF16), x_h,
                                    preferred_element_type=F32)
            y_ref[tok, sl] = y_g

    refs = ((xf_ref, yf_ref), (xb_ref, yb_ref))
    order = (tuple(range(n_sub)), tuple(reversed(range(n_sub))))
    lin = {}
    for k in range(n_sub):
        for d in range(SSD_DIRS):
            u = order[d][k]
            lin[d, u] = input_matmuls(d, u, refs[d][0])
    for k in range(n_sub):
        for d in range(SSD_DIRS):
            u = order[d][k]
            decay_matmuls(d, u, *lin[d, u], refs[d][1])
    into_scratch(token_scalars(*next_sums))


def _ssd(xc, dt_raw, dt_bias, a_log, d_skip, batch, n_lat, n_ctx):
    n_tok = xc.shape[0]
    q = SSD_CHUNK
    blk = SSD_STEP_CHUNKS * q
    lat_blk = n_lat // batch // blk
    ctx_blk = n_ctx // batch // blk
    assert lat_blk * blk * batch == n_lat and ctx_blk * blk * batch == n_ctx
    n_steps = lat_blk + ctx_blk

    def fwd(b, i):
        return jnp.where(i < ctx_blk, batch * lat_blk + b * ctx_blk + i, b * lat_blk + i - ctx_blk)

    def bwd(b, i):
        return jnp.where(i < ctx_blk, batch * lat_blk + b * ctx_blk + ctx_blk - 1 - i,
                         b * lat_blk + n_steps - 1 - i)

    def specs(ch):
        return [
            pl.BlockSpec((blk, SSD_XBC), lambda b, i: (ch(b, i), 0)),
            pl.BlockSpec((blk, LANES), lambda b, i: (ch(b, i), 0)),
            pl.BlockSpec((blk, LANES), lambda b, i: (ch(b, jnp.minimum(i + 1, n_steps - 1)), 0)),
        ]

    n_col = SSD_DIRS * SSD_HEADS
    rows = lambda v: jnp.broadcast_to(v.reshape(n_col, 1), (n_col, blk))
    dsk = jnp.repeat(d_skip, SSD_HEAD_DIM, axis=1)
    term_row = jnp.arange(2 * BF16_SUBLANES) % n_col
    live3 = jnp.arange(2 * BF16_SUBLANES) < 3 * n_col
    col_of_channel = jnp.arange(SSD_DIRS * SSD_INNER) // SSD_HEAD_DIM
    col_of_block = jnp.arange(n_col * q) // q
    spread = (term_row[:BF16_SUBLANES, None] == col_of_channel[None, :]).astype(BF16)
    ecol = ((term_row[:, None] == col_of_block[None, :]) & live3[:, None]).astype(BF16)
    return pl.pallas_call(
        _ssd_kernel,
        out_shape=[jax.ShapeDtypeStruct((n_tok, SSD_INNER), F32)] * 2,
        grid=(batch, n_steps),
        in_specs=specs(fwd) + specs(bwd) + [
            _const_spec((n_col, blk)), _const_spec((n_col, blk)), _const_spec((SSD_DIRS, SSD_INNER)),
            _const_spec(spread.shape), _const_spec(ecol.shape),
        ],
        out_specs=[pl.BlockSpec((blk, SSD_INNER), lambda b, i: (fwd(b, i), 0)),
                   pl.BlockSpec((blk, SSD_INNER), lambda b, i: (bwd(b, i), 0))],
        scratch_shapes=[pltpu.VMEM((SSD_DIRS, SSD_GROUPS * SSD_STATE, SSD_INNER), F32),
                        pltpu.VMEM((n_col, blk), F32),
                        pltpu.VMEM((3 * blk, SSD_DIRS * SSD_INNER), F32),
                        pltpu.VMEM((blk, n_col * q), F32)],
        compiler_params=pltpu.CompilerParams(
            dimension_semantics=("arbitrary", "arbitrary"), vmem_limit_bytes=VMEM_LIMIT),
        name="ssd_scan",
    )(xc, dt_raw, dt_raw, xc, dt_raw, dt_raw,
      rows(dt_bias), rows(a_log), dsk, spread, ecol)


def _steps_per_trip(n_steps, preferred):
    return max(u for u in range(2, preferred + 1, 2) if n_steps % u == 0) if n_steps else 2


def _scores_t(k, q2):
    return lax.dot_general(k, q2, (((1,), (1,)), ((), ())), preferred_element_type=F32)


def _softmax_stage(s, m_blk, m):
    if m is None:
        return m_blk, None, jnp.exp2((s - m_blk).astype(BF16))
    m_new = jnp.maximum(m, m_blk)
    return m_new, jnp.exp2(m - m_new), jnp.exp2((s - m_new).astype(BF16))


def _attn_kernel(q_ref, kc_ref, vtc_ref, kx_ref, vtx_ref, o_ref, s0_ref, s1_ref, p0_ref, p1_ref,
                 *, n_chunks):
    tq = q_ref.shape[0]
    tk = ATTN_K_TILE
    q2 = jnp.concatenate([q_ref[:, 0:HEAD_DIM], q_ref[:, HEAD_DIM:2 * HEAD_DIM]], axis=0)
    s_refs, p_refs = (s0_ref, s1_ref), (p0_ref, p1_ref)

    def scores_into(slot, c):
        s = _scores_t(kx_ref[pl.ds(pl.multiple_of(c * tk, tk), tk), :], q2)
        s_refs[slot][...] = s
        return jnp.max(s, axis=0, keepdims=True)

    def softmax_into(slot, m_blk, m):
        m, alpha, p = _softmax_stage(s_refs[slot][...], m_blk, m)
        p_refs[slot][...] = p
        return m, alpha

    def step(c, par, carry, with_scores, with_softmax):
        m, acc, alpha, m_blk = carry
        m_blk_next, alpha_next = m_blk, alpha
        if with_scores:
            m_blk_next = scores_into(par, c + 2)
        if with_softmax:
            m, alpha_next = softmax_into(1 - par, m_blk, m)
        acc = alpha * acc + jnp.dot(vtx_ref[c], p_refs[par][...], preferred_element_type=F32)
        return m, acc, alpha_next, m_blk_next

    if n_chunks:
        m_blk0 = scores_into(0, 0)
        m_blk1 = scores_into(1, 1)
    s_ctx = _scores_t(kc_ref[...], q2)
    m, _, p_ctx = _softmax_stage(s_ctx, jnp.max(s_ctx, axis=0, keepdims=True), None)
    acc = jnp.dot(vtc_ref[...], p_ctx, preferred_element_type=F32)
    if n_chunks:
        m, alpha = softmax_into(0, m_blk0, m)
        carry = (m, acc, alpha, m_blk1)
        unroll = _steps_per_trip(n_chunks - 2, ATTN_UNROLL)

        def body(i, carry):
            for u in range(unroll):
                carry = step(unroll * i + u, u % 2, carry, True, True)
            return carry

        carry = lax.fori_loop(0, (n_chunks - 2) // unroll, body, carry)
        carry = step(n_chunks - 2, 0, carry, False, True)
        _, acc, _, _ = step(n_chunks - 1, 1, carry, False, False)
    o = (acc[0:HEAD_DIM] / acc[HEAD_DIM:HEAD_DIM + 1]).T.astype(o_ref.dtype)
    o_ref[:, 0:HEAD_DIM] = o[0:tq]
    o_ref[:, HEAD_DIM:2 * HEAD_DIM] = o[tq:2 * tq]


def _attn_bounded_kernel(q_ref, kc_ref, vtc_ref, kx_ref, vtx_ref, o_ref, p0_ref, p1_ref, *, n_chunks):
    tq = q_ref.shape[0]
    tk = ATTN_K_TILE
    q2 = jnp.concatenate([q_ref[:, 0:HEAD_DIM], q_ref[:, HEAD_DIM:2 * HEAD_DIM]], axis=0)
    p_refs = (p0_ref, p1_ref)

    def colsum(p):
        return jnp.sum(p.reshape(p.shape[0] // SUBLANES, SUBLANES, p.shape[1]), axis=0)

    def weights_into(slot, c):
        p = jnp.exp2(_scores_t(kx_ref[pl.ds(pl.multiple_of(c * tk, tk), tk), :], q2))
        p_refs[slot][...] = p.astype(BF16)
        return colsum(p)

    def step(c, par, carry, with_scores):
        l8, acc = carry
        if with_scores:
            l8 = l8 + weights_into(1 - par, c + 1)
        acc = acc + jnp.dot(vtx_ref[c, 0:HEAD_DIM, :], p_refs[par][...],
                            preferred_element_type=F32)
        return l8, acc

    p_ctx = jnp.exp2(_scores_t(kc_ref[...], q2))
    l8 = colsum(p_ctx) + weights_into(0, 0)
    acc = jnp.dot(vtc_ref[0:HEAD_DIM, :], p_ctx.astype(BF16), preferred_element_type=F32)
    unroll = _steps_per_trip(n_chunks - 2, ATTN_BOUNDED_UNROLL)

    def body(i, carry):
        for u in range(unroll):
            carry = step(unroll * i + u, u % 2, carry, True)
        return carry

    carry = lax.fori_loop(0, (n_chunks - 2) // unroll, body, (l8, acc))
    carry = step(n_chunks - 2, 0, carry, True)
    l8, acc = step(n_chunks - 1, 1, carry, False)
    o = (acc / jnp.sum(l8, axis=0, keepdims=True)).T.astype(o_ref.dtype)
    o_ref[:, 0:HEAD_DIM] = o[0:tq]
    o_ref[:, HEAD_DIM:2 * HEAD_DIM] = o[tq:2 * tq]


def _attention_latent(q, k, vt, batch, n_lat, n_ctx, bounded):
    seq = n_lat // batch
    ctx = n_ctx // batch
    tq, tk = ATTN_Q_TILE, ATTN_K_TILE
    nq = seq // tq
    n_chunks = seq // tk
    assert seq % (2 * tk) == 0 and n_lat % ctx == 0 and n_ctx == tk and vt.shape[2] == tk
    r = 2 * tq
    body = _attn_bounded_kernel if bounded else _attn_kernel
    p_bufs = [pltpu.VMEM((tk, r), BF16), pltpu.VMEM((tk, r), BF16)]
    s_bufs = [] if bounded else [pltpu.VMEM((tk, r), F32), pltpu.VMEM((tk, r), F32)]
    return pl.pallas_call(
        functools.partial(body, n_chunks=n_chunks),
        out_shape=jax.ShapeDtypeStruct((n_lat, ATTN_DIM), BF16),
        grid=(batch, ATTN_KV_HEADS, nq),
        in_specs=[pl.BlockSpec((tq, 2 * HEAD_DIM), lambda b, g, j: (b * nq + j, g)),
                  pl.BlockSpec((ctx, HEAD_DIM), lambda b, g, j: (n_lat // ctx + b, g)),
                  pl.BlockSpec((None, VT_ROWS, ctx), lambda b, g, j: (n_lat // tk, g, b)),
                  pl.BlockSpec((seq, HEAD_DIM), lambda b, g, j: (b, g)),
                  pl.BlockSpec((n_chunks, VT_ROWS, tk), lambda b, g, j: (b, g, 0))],
        out_specs=pl.BlockSpec((tq, 2 * HEAD_DIM), lambda b, g, j: (b * nq + j, g)),
        scratch_shapes=s_bufs + p_bufs,
        compiler_params=pltpu.CompilerParams(
            dimension_semantics=("arbitrary",) * 3, vmem_limit_bytes=VMEM_LIMIT),
        name="attn_latent_bounded" if bounded else "attn_latent",
    )(q, k, vt, k, vt)


def _attn_ctx_kernel(q_ref, kc_ref, vtc_ref, o_ref):
    _attn_kernel(q_ref, kc_ref, vtc_ref, None, None, o_ref, None, None, None, None, n_chunks=0)


def _attention_context(q, k, vt, batch, n_lat, n_ctx):
    ctx = n_ctx // batch
    tk = ATTN_K_TILE
    return pl.pallas_call(
        _attn_ctx_kernel,
        out_shape=jax.ShapeDtypeStruct((n_ctx, ATTN_DIM), BF16),
        grid=(batch, ATTN_KV_HEADS),
        in_specs=[pl.BlockSpec((ctx, 2 * HEAD_DIM), lambda b, g: (n_lat // ctx + b, g)),
                  pl.BlockSpec((ctx, HEAD_DIM), lambda b, g: (n_lat // ctx + b, g)),
                  pl.BlockSpec((None, VT_ROWS, ctx), lambda b, g: (n_lat // tk, g, b))],
        out_specs=pl.BlockSpec((ctx, 2 * HEAD_DIM), lambda b, g: (b, g)),
        compiler_params=pltpu.CompilerParams(
            dimension_semantics=("arbitrary",) * 2, vmem_limit_bytes=VMEM_LIMIT),
        name="attn_context",
    )(q, k, vt)


def _post_kernel(xs_ref, up_ref, upp_ref, upn_ref, z_ref, yf_ref, yb_ref, at_ref, mod_ref,
                 pw_ref, ps_ref, sw_ref, wo_ref, n2_ref, w1_ref, w3_ref, w2_ref,
                 o_ref, buf_ref, *, tiles_per_seq, seq_len):
    i = pl.program_id(0)
    tm = xs_ref.shape[0]
    halo = SUBLANES
    j = i % tiles_per_seq

    buf_ref[0:halo, :] = jnp.where(j == 0, 0.0, upp_ref[...])
    buf_ref[halo:halo + tm, :] = up_ref[...]
    buf_ref[halo + tm:2 * halo + tm, :] = jnp.where(j == tiles_per_seq - 1, 0.0, upn_ref[...])
    ub = buf_ref[...]
    n = tm + 2 * halo
    s2 = ub + pltpu.roll(ub, 1, 0)
    s4 = pltpu.roll(s2, 1, 0) + pltpu.roll(s2, n - 1, 0)
    s8 = pltpu.roll(s4, 2, 0) + pltpu.roll(s4, n - 2, 0)
    s16 = pltpu.roll(s8, 4, 0) + pltpu.roll(s8, n - 4, 0)
    shape = (tm, POOL_DIM)
    grp = lax.broadcasted_iota(jnp.int32, shape, 1) // POOL_GDIM
    t = lax.broadcasted_iota(jnp.int32, shape, 0) + j * tm
    half = jnp.left_shift(1, grp)
    cnt = (jnp.minimum(t + half, seq_len) - jnp.maximum(t - half, 0)).astype(F32)
    sl = slice(halo, halo + tm)
    wsum = jnp.where(grp == 0, s2[sl], jnp.where(grp == 1, s4[sl],
                                                 jnp.where(grp == 2, s8[sl], s16[sl])))
    pooled = wsum / cnt - ub[sl]
    pool = jnp.dot(pooled.astype(BF16), pw_ref[...], preferred_element_type=F32) * ps_ref[...]

    gy = (yf_ref[...] + yb_ref[...]) * _silu(z_ref[...])
    gw = SSD_INNER // SSD_GROUPS
    parts = []
    for g in range(SSD_GROUPS):
        t_g = gy[:, g * gw:(g + 1) * gw]
        parts.append(t_g * lax.rsqrt(jnp.mean(t_g * t_g, axis=-1, keepdims=True) + EPS))
    ssd = jnp.concatenate(parts, axis=1) * sw_ref[...]

    mix = jnp.concatenate([pool.astype(BF16), ssd.astype(BF16), at_ref[...]], axis=1)
    blocks = [slice(r0, r0 + min(POST_ROWS, tm)) for r0 in range(0, tm, POST_ROWS)]
    x1 = [xs_ref[rows, :] + mod_ref[2:3, :] * jnp.dot(mix[rows], wo_ref[...],
                                                       preferred_element_type=F32)
          for rows in blocks]

    def modulated_norm(v):
        h = v * lax.rsqrt(jnp.mean(v * v, axis=-1, keepdims=True) + EPS) * n2_ref[...]
        return (h * (1.0 + mod_ref[4:5, :]) + mod_ref[3:4, :]).astype(BF16)

    gated = []
    for v in x1:
        h = modulated_norm(v)
        a = jnp.dot(h, w1_ref[...], preferred_element_type=F32)
        b = jnp.dot(h, w3_ref[...], preferred_element_type=F32)
        gated.append((_silu(a) * b).astype(BF16))
    for rows, v, gt in zip(blocks, x1, gated):
        ff = jnp.dot(gt, w2_ref[...], preferred_element_type=F32)
        o_ref[rows, :] = v + mod_ref[5:6, :] * ff


def _post(xs, upool, z, yf, yb, attn, mod, pw_bd, pscale, ssd_w, w_out, n2, w1, w3, w2,
          *, tm, row0, tiles_per_seq, mod_row):
    n_rows, d = xs.shape
    n_tiles = n_rows // tm
    off = row0 // tm
    blk8 = tm // SUBLANES
    last8 = upool.shape[0] // SUBLANES - 1
    tok = lambda w: pl.BlockSpec((tm, w), lambda i: (off + i, 0))
    loc = lambda w: pl.BlockSpec((tm, w), lambda i: (i, 0))
    return pl.pallas_call(
        functools.partial(_post_kernel, tiles_per_seq=tiles_per_seq, seq_len=tiles_per_seq * tm),
        out_shape=jax.ShapeDtypeStruct((n_rows, d), F32),
        grid=(n_tiles,),
        in_specs=[
            loc(d),
            tok(POOL_DIM),
            pl.BlockSpec((SUBLANES, POOL_DIM), lambda i: (jnp.maximum((off + i) * blk8 - 1, 0), 0)),
            pl.BlockSpec((SUBLANES, POOL_DIM),
                         lambda i: (jnp.minimum((off + i) * blk8 + blk8, last8), 0)),
            tok(SSD_INNER), tok(SSD_INNER), tok(SSD_INNER),
            loc(ATTN_DIM),
            pl.BlockSpec((None, N_MOD, d), lambda i: (mod_row(i), 0, 0)),
            _const_spec(pw_bd.shape), _const_spec((1, POOL_DIM)), _const_spec((1, SSD_INNER)),
            _const_spec(w_out.shape), _const_spec((1, d)),
            _const_spec(w1.shape), _const_spec(w3.shape), _const_spec(w2.shape),
        ],
        out_specs=loc(d),
        scratch_shapes=[pltpu.VMEM((tm + 2 * SUBLANES, POOL_DIM), F32)],
        compiler_params=pltpu.CompilerParams(
            dimension_semantics=("arbitrary",), vmem_limit_bytes=VMEM_LIMIT),
        name="post_mix_ffn",
    )(xs, upool, upool, upool, z, yf, yb, attn, mod, pw_bd, pscale, ssd_w, w_out, n2, w1, w3, w2)


def _rope_tables(seq, extra_rows):
    rows, pad_rows = seq // GRID_W, extra_rows // GRID_W
    axis_dim = HEAD_DIM // 2
    inv_freq = ROPE_THETA ** (-jnp.arange(0, axis_dim, 2, dtype=F32) / axis_dim)
    ang_r = jnp.arange(rows, dtype=F32)[:, None] * inv_freq[None, :]
    ang_c = jnp.arange(GRID_W, dtype=F32)[:, None] * inv_freq[None, :]
    live = (jnp.arange(rows + pad_rows) < rows)[:, None, None]
    shape = (rows + pad_rows, GRID_W, axis_dim // 2)

    def per_row(t, fill):
        t = jnp.concatenate([t, jnp.full((pad_rows, t.shape[1]), fill, F32)], axis=0)
        return jnp.broadcast_to(t[:, None, :], shape)

    def per_col(t, fill):
        return jnp.where(live, jnp.broadcast_to(t[None, :, :], shape), fill)

    cr, sr = per_row(jnp.cos(ang_r), 1.0), per_row(jnp.sin(ang_r), 0.0)
    cc, sc = per_col(jnp.cos(ang_c), 1.0), per_col(jnp.sin(ang_c), 0.0)
    cos = jnp.concatenate([cr, cr, cc, cc], axis=-1).reshape(-1, HEAD_DIM)
    sin = jnp.concatenate([-sr, sr, -sc, sc], axis=-1).reshape(-1, HEAD_DIM)
    return cos, sin


def _fuse_w_in(w_in):
    d = w_in.shape[0]
    o_dt = POOL_DIM + SSD_INNER + SSD_XBC
    o_att = o_dt + SSD_DIRS * SSD_HEADS
    dt_pad = jnp.zeros((d, C_Q - C_DT - SSD_DIRS * SSD_HEADS), w_in.dtype)
    return jnp.concatenate([w_in[:, :o_dt], w_in[:, o_dt:o_att], dt_pad, w_in[:, o_att:]],
                           axis=1).astype(BF16)


def kernel(x, c, ctx, c_ctx, norm1_w, norm2_w, w_mod, b_mod, w_in, pool_w, pool_scale, conv_w, conv_b,
           dt_bias, a_log, d_skip, ssd_norm_w, q_norm_w, k_norm_w, w_out, w1, w3, w2):
    batch, seq, d = x.shape
    ctx_len = ctx.shape[1]
    depth = w_mod.shape[0]
    n_lat, n_ctx = batch * seq, batch * ctx_len
    assert batch + 1 <= SUBLANES and seq % TOKEN_TILE == 0 and seq % GRID_W == 0

    cond = jnp.zeros((SUBLANES, d), F32).at[:batch].set(c).at[batch].set(c_ctx)
    mod_all = _modulation(cond, w_mod, b_mod).reshape(depth, SUBLANES, N_MOD, d)
    cos_tab, sin_tab = _rope_tables(seq, TOKEN_TILE)

    xs_lat = x.reshape(n_lat, d)
    xs_ctx = ctx.reshape(n_ctx, d)
    tiles_per_seq = seq // TOKEN_TILE
    for layer in range(depth):
        need_ctx = layer < depth - 1
        mod = mod_all[layer]
        w_cat = _fuse_w_in(w_in[layer])
        upool, z, xbc, dt_raw, q, k, vt = _inproj(
            xs_lat, xs_ctx, mod, norm1_w[layer].reshape(1, d), w_cat,
            q_norm_w[layer].reshape(1, -1), k_norm_w[layer].reshape(1, -1), cos_tab, sin_tab, batch)
        xc = _ssd_conv(xbc, conv_w[layer], conv_b[layer], batch, n_lat, n_ctx)
        yf, yb = _ssd(xc, dt_raw, dt_bias[layer], a_log[layer], d_skip[layer], batch, n_lat, n_ctx)
        score_bound = (HEAD_DIM * Q_SCALE * SCORE_BOUND_SLACK * jnp.max(jnp.abs(q_norm_w[layer]))
                       * jnp.max(jnp.abs(k_norm_w[layer])))
        attn_x = lax.cond(
            score_bound < SCORE_BOUND_LIMIT,
            functools.partial(_attention_latent, batch=batch, n_lat=n_lat, n_ctx=n_ctx, bounded=True),
            functools.partial(_attention_latent, batch=batch, n_lat=n_lat, n_ctx=n_ctx, bounded=False),
            q, k, vt)
        pw_bd = jax.scipy.linalg.block_diag(*[pool_w[layer, g] for g in range(POOL_GROUPS)]).astype(BF16)
        post = functools.partial(
            _post, upool=upool, z=z, yf=yf, yb=yb, mod=mod, pw_bd=pw_bd,
            pscale=pool_scale[layer].reshape(1, -1), ssd_w=ssd_norm_w[layer].reshape(1, -1),
            w_out=w_out[layer].astype(BF16), n2=norm2_w[layer].reshape(1, d),
            w1=w1[layer].astype(BF16), w3=w3[layer].astype(BF16), w2=w2[layer].astype(BF16))
        new_lat = post(xs_lat, attn=attn_x, tm=TOKEN_TILE, row0=0, tiles_per_seq=tiles_per_seq,
                       mod_row=lambda i: i // tiles_per_seq)
        if need_ctx:
            attn_c = _attention_context(q, k, vt, batch, n_lat, n_ctx)
            xs_ctx = post(xs_ctx, attn=attn_c, tm=ctx_len, row0=n_lat, tiles_per_seq=1,
                          mod_row=lambda i: batch)
        xs_lat = new_lat
    return xs_lat.reshape(batch, seq, d)
```

```python
import functools

import jax
import jax.numpy as jnp
from jax import lax
from jax.experimental import pallas as pl
from jax.experimental.pallas import tpu as pltpu

F32 = jnp.float32
BF16 = jnp.bfloat16

GRID_W = 64
EPS = 1e-6
N_MOD = 6
POOL_DIM = 256
POOL_GDIM = 64
POOL_GROUPS = 4
SSD_HEADS = 4
SSD_HEAD_DIM = 64
SSD_INNER = 256
SSD_GROUPS = 2
SSD_STATE = 64
SSD_CONV = 5
SSD_CHUNK = 128
SSD_STEP_CHUNKS = 2
SSD_DIRS = 2
SSD_XBC = 512
ATTN_HEADS = 4
ATTN_KV_HEADS = 2
HEAD_DIM = 128
ATTN_DIM = 512
KV_DIM = 256
ROPE_THETA = 10000.0
Q_SCALE = HEAD_DIM ** -0.5 * 1.4426950408889634
SCORE_BOUND_LIMIT = 60.0
SCORE_BOUND_SLACK = 1.02

LANES = 128
SUBLANES = 8
TOKEN_TILE = 512
INPROJ_ROWS = 256
CONV_TILE = 1024
POST_ROWS = 256
ATTN_Q_TILE = 512
ATTN_K_TILE = 512
ATTN_UNROLL = 6
ATTN_BOUNDED_UNROLL = 30
BF16_SUBLANES = 16
VT_ROWS = HEAD_DIM + BF16_SUBLANES
VMEM_LIMIT = 56 * 1024 * 1024

C_POOL = 0
C_Z = C_POOL + POOL_DIM
C_XBC = C_Z + SSD_INNER
C_DT = C_XBC + SSD_XBC
C_Q = C_DT + LANES
C_K = C_Q + ATTN_DIM
C_V = C_K + KV_DIM
C_END = C_V + KV_DIM


def _silu(x):
    return x * (1.0 / (1.0 + jnp.exp(-x)))


def _softplus(x):
    return jnp.maximum(x, 0.0) + jnp.log1p(jnp.exp(-jnp.abs(x)))


def _const_spec(shape):
    nd = len(shape)
    return pl.BlockSpec(shape, lambda *_: (0,) * nd, pipeline_mode=pl.Buffered(1))


def _mod_kernel(cond_ref, w_ref, b_ref, o_ref):
    s = _silu(cond_ref[...]).astype(BF16)
    o_ref[...] = jnp.dot(s, w_ref[...].astype(BF16), preferred_element_type=F32) + b_ref[...]


def _modulation(cond, w_mod, b_mod):
    depth, d, n = w_mod.shape
    bn = d
    return pl.pallas_call(
        _mod_kernel,
        out_shape=jax.ShapeDtypeStruct((depth, SUBLANES, n), F32),
        grid=(depth, n // bn),
        in_specs=[
            pl.BlockSpec((SUBLANES, d), lambda l, j: (0, 0)),
            pl.BlockSpec((None, d, bn), lambda l, j: (l, 0, j)),
            pl.BlockSpec((None, 1, bn), lambda l, j: (l, 0, j)),
        ],
        out_specs=pl.BlockSpec((None, SUBLANES, bn), lambda l, j: (l, 0, j)),
        compiler_params=pltpu.CompilerParams(
            dimension_semantics=("arbitrary", "arbitrary"), vmem_limit_bytes=VMEM_LIMIT),
        name="modulation",
    )(cond, w_mod, b_mod.reshape(depth, 1, n))


def _rope_partner(x):
    lane = lax.broadcasted_iota(jnp.int32, x.shape, 1)
    fwd = pltpu.roll(x, LANES - 32, 1)
    bwd = pltpu.roll(x, 32, 1)
    return jnp.where((lane % 64) < 32, fwd, bwd)


def _inproj_kernel(xl_ref, xc_ref, mod_ref, n1_ref, w_ref, qw_ref, kw_ref, cos_ref, sin_ref,
                   pool_ref, z_ref, xbc_ref, dt_ref, q_ref, k_ref, vt_ref, *, n_lat_tiles):
    i = pl.program_id(0)
    is_lat = i < n_lat_tiles
    tm = xl_ref.shape[0]
    for r0 in range(0, tm, INPROJ_ROWS):
        rows = slice(r0, r0 + INPROJ_ROWS)
        x = jnp.where(is_lat, xl_ref[rows, :], xc_ref[rows, :])
        ms = jnp.mean(x * x, axis=-1, keepdims=True)
        h = x * lax.rsqrt(ms + EPS) * n1_ref[...]
        h = h * (1.0 + mod_ref[1:2, :]) + mod_ref[0:1, :]
        u = jnp.dot(h.astype(BF16), w_ref[...], preferred_element_type=F32)
        pool_ref[rows, :] = u[:, C_POOL:C_Z]
        z_ref[rows, :] = u[:, C_Z:C_XBC]
        xbc_ref[rows, :] = u[:, C_XBC:C_DT]
        dt_ref[rows, :] = u[:, C_DT:C_Q]
        vt = u[:, C_V:C_END].T.astype(BF16)
        for hd in range(ATTN_KV_HEADS):
            vt_ref[hd * VT_ROWS:hd * VT_ROWS + HEAD_DIM, rows] = vt[hd * HEAD_DIM:(hd + 1) * HEAD_DIM]
            vt_ref[hd * VT_ROWS + HEAD_DIM:(hd + 1) * VT_ROWS, rows] = jnp.ones(
                (VT_ROWS - HEAD_DIM, INPROJ_ROWS), BF16)
        cos = cos_ref[rows, :]
        sin = sin_ref[rows, :]

        def norm_rope(t, w):
            t = t * lax.rsqrt(jnp.mean(t * t, axis=-1, keepdims=True) + EPS) * w
            return t * cos + _rope_partner(t) * sin

        for hd in range(ATTN_HEADS):
            t = norm_rope(u[:, C_Q + hd * HEAD_DIM:C_Q + (hd + 1) * HEAD_DIM], qw_ref[...])
            q_ref[rows, hd * HEAD_DIM:(hd + 1) * HEAD_DIM] = (t * Q_SCALE).astype(BF16)
        for hd in range(ATTN_KV_HEADS):
            t = norm_rope(u[:, C_K + hd * HEAD_DIM:C_K + (hd + 1) * HEAD_DIM], kw_ref[...])
            k_ref[rows, hd * HEAD_DIM:(hd + 1) * HEAD_DIM] = t.astype(BF16)


def _inproj(xs_lat, xs_ctx, mod, n1, w_cat, qw, kw, cos_tab, sin_tab, batch):
    n_lat, d = xs_lat.shape
    n_ctx = xs_ctx.shape[0]
    tm = TOKEN_TILE
    assert n_ctx == tm and n_lat % (batch * tm) == 0
    n_lat_tiles = n_lat // tm
    tiles_per_seq = n_lat_tiles // batch
    n_tok = n_lat + n_ctx
    grid = (n_lat_tiles + 1,)

    def lat_idx(i):
        return jnp.minimum(i, n_lat_tiles - 1)

    def rope_idx(i):
        return jnp.where(i < n_lat_tiles, i % tiles_per_seq, tiles_per_seq)

    def mod_idx(i):
        return jnp.where(i < n_lat_tiles, i // tiles_per_seq, batch)

    row = lambda w: pl.BlockSpec((tm, w), lambda i: (i, 0))
    outs = [(POOL_DIM, F32), (SSD_INNER, F32), (SSD_XBC, F32), (LANES, F32),
            (ATTN_DIM, BF16), (KV_DIM, BF16)]
    return pl.pallas_call(
        functools.partial(_inproj_kernel, n_lat_tiles=n_lat_tiles),
        out_shape=[jax.ShapeDtypeStruct((n_tok, w), t) for w, t in outs]
        + [jax.ShapeDtypeStruct((n_tok // tm, ATTN_KV_HEADS * VT_ROWS, tm), BF16)],
        grid=grid,
        in_specs=[
            pl.BlockSpec((tm, d), lambda i: (lat_idx(i), 0)),
            pl.BlockSpec((tm, d), lambda i: (0, 0)),
            pl.BlockSpec((None, N_MOD, d), lambda i: (mod_idx(i), 0, 0)),
            _const_spec((1, d)),
            _const_spec(w_cat.shape),
            _const_spec((1, HEAD_DIM)),
            _const_spec((1, HEAD_DIM)),
            pl.BlockSpec((tm, HEAD_DIM), lambda i: (rope_idx(i), 0)),
            pl.BlockSpec((tm, HEAD_DIM), lambda i: (rope_idx(i), 0)),
        ],
        out_specs=[row(w) for w, _ in outs]
        + [pl.BlockSpec((None, ATTN_KV_HEADS * VT_ROWS, tm), lambda i: (i, 0, 0))],
        compiler_params=pltpu.CompilerParams(
            dimension_semantics=("arbitrary",), vmem_limit_bytes=VMEM_LIMIT),
        name="inproj",
    )(xs_lat, xs_ctx, mod, n1, w_cat, qw, kw, cos_tab, sin_tab)


def _conv_kernel(x_ref, prev_ref, next_ref, ctx_ref, cw_ref, cb_ref, o_ref, buf_ref,
                 *, tiles_per_seq, n_lat_tiles, ctx_len):
    i = pl.program_id(0)
    tm = x_ref.shape[0]
    pad = SSD_CONV // 2

    def conv_silu(rows):
        u = buf_ref[0:rows + 2 * SUBLANES, :]
        n = u.shape[0]
        acc = cb_ref[...] + cw_ref[pad:pad + 1, :] * u[SUBLANES:SUBLANES + rows]
        for k in range(SSD_CONV):
            if k != pad:
                shifted = pltpu.roll(u, (pad - k) % n, 0)
                acc = acc + cw_ref[k:k + 1, :] * shifted[SUBLANES:SUBLANES + rows]
        return _silu(acc)

    @pl.when(i < n_lat_tiles)
    def _():
        first = i % tiles_per_seq == 0
        last = i % tiles_per_seq == tiles_per_seq - 1
        buf_ref[0:SUBLANES, :] = jnp.where(first, 0.0, prev_ref[...])
        buf_ref[SUBLANES:SUBLANES + tm, :] = x_ref[...]
        buf_ref[SUBLANES + tm:2 * SUBLANES + tm, :] = jnp.where(last, 0.0, next_ref[...])
        o_ref[...] = conv_silu(tm)

    @pl.when(i == n_lat_tiles)
    def _():
        n_ctx = ctx_ref.shape[0]
        halo = jnp.zeros((SUBLANES, SSD_XBC), F32)
        for r0 in range(0, n_ctx, ctx_len):
            buf_ref[0:SUBLANES, :] = halo
            buf_ref[SUBLANES:SUBLANES + ctx_len, :] = ctx_ref[r0:r0 + ctx_len, :]
            buf_ref[SUBLANES + ctx_len:2 * SUBLANES + ctx_len, :] = halo
            o_ref[r0:r0 + ctx_len, :] = conv_silu(ctx_len)
        o_ref[n_ctx:tm, :] = jnp.zeros((tm - n_ctx, SSD_XBC), F32)


def _ssd_conv(xbc, conv_w, conv_b, batch, n_lat, n_ctx):
    n_tok = xbc.shape[0]
    seq, ctx = n_lat // batch, n_ctx // batch
    tm = CONV_TILE
    assert seq % tm == 0 and n_lat % n_ctx == 0 and n_ctx < tm
    n_lat_tiles = n_lat // tm
    blk8 = tm // SUBLANES
    lat = lambda i: jnp.minimum(i, n_lat_tiles - 1)
    cw = jnp.zeros((SUBLANES, SSD_XBC), F32).at[:SSD_CONV].set(conv_w)
    return pl.pallas_call(
        functools.partial(_conv_kernel, tiles_per_seq=seq // tm, n_lat_tiles=n_lat_tiles,
                          ctx_len=ctx),
        out_shape=jax.ShapeDtypeStruct((n_tok, SSD_XBC), F32),
        grid=(n_lat_tiles + 1,),
        in_specs=[
            pl.BlockSpec((tm, SSD_XBC), lambda i: (lat(i), 0)),
            pl.BlockSpec((SUBLANES, SSD_XBC), lambda i: (jnp.maximum(lat(i) * blk8 - 1, 0), 0)),
            pl.BlockSpec((SUBLANES, SSD_XBC), lambda i: (lat(i) * blk8 + blk8, 0)),
            pl.BlockSpec((n_ctx, SSD_XBC), lambda i: (n_lat // n_ctx, 0)),
            _const_spec((SUBLANES, SSD_XBC)), _const_spec((1, SSD_XBC)),
        ],
        out_specs=pl.BlockSpec((tm, SSD_XBC), lambda i: (i, 0)),
        scratch_shapes=[pltpu.VMEM((tm + 2 * SUBLANES, SSD_XBC), F32)],
        compiler_params=pltpu.CompilerParams(
            dimension_semantics=("arbitrary",), vmem_limit_bytes=VMEM_LIMIT),
        name="ssd_conv",
    )(xbc, xbc, xbc, xbc, cw, conv_b.reshape(1, -1))


def _ssd_kernel(xf_ref, dtf_ref, dtf_next_ref, xb_ref, dtb_ref, dtb_next_ref,
                bias_ref, alog_ref, dsk_ref, spread_ref, ecol_ref,
                yf_ref, yb_ref, h_ref, et_sc, spread_sc, ecols_sc):
    i = pl.program_id(1)
    q = SSD_CHUNK
    n_sub = SSD_STEP_CHUNKS
    blk = n_sub * q

    @pl.when(i == 0)
    def _():
        h_ref[...] = jnp.zeros_like(h_ref)

    row = lax.broadcasted_iota(jnp.int32, (q, q), 0)
    col = lax.broadcasted_iota(jnp.int32, (q, q), 1)
    n_col = SSD_DIRS * SSD_HEADS
    heads_per_group = SSD_HEADS // SSD_GROUPS
    pair = heads_per_group * SSD_HEAD_DIM
    lane_half = col // SSD_HEAD_DIM
    st_row = lax.broadcasted_iota(jnp.int32, (SSD_GROUPS * SSD_STATE, SSD_INNER), 0)
    st_col = lax.broadcasted_iota(jnp.int32, (SSD_GROUPS * SSD_STATE, SSD_INNER), 1)
    same_group = (st_row // SSD_STATE) == (st_col // pair)
    contract0 = (((0,), (0,)), ((), ()))

    def bf16_terms(v, n_terms):
        terms, rest = [], v
        for _ in range(n_terms):
            t = rest.astype(BF16).astype(F32)
            terms.append(t)
            rest = rest - t
        rows = -(-n_terms * n_col // BF16_SUBLANES) * BF16_SUBLANES
        if rows > n_terms * n_col:
            terms.append(jnp.zeros((rows - n_terms * n_col, v.shape[1]), F32))
        return jnp.concatenate(terms, axis=0).astype(BF16)

    fwd_rows = lax.broadcasted_iota(jnp.int32, (n_col, blk), 0) < SSD_HEADS

    def token_steps(dtf_blk_ref, dtb_blk_ref):
        dt_raw = jnp.where(fwd_rows, dtf_blk_ref[...].T[0:n_col], dtb_blk_ref[...].T[0:n_col])
        dt = _softplus(dt_raw + bias_ref[...])
        return dt, dt * -jnp.exp(alog_ref[...])

    def token_decay_sums(dt, a):
        r = lax.broadcasted_iota(jnp.int32, (blk, blk), 0)
        c = lax.broadcasted_iota(jnp.int32, (blk, blk), 1)
        tri_u = jnp.logical_and(r <= c, r // q == c // q).astype(BF16)
        parts = jnp.dot(bf16_terms(a, 3), tri_u, preferred_element_type=F32)
        cs = parts[0:n_col] + parts[n_col:2 * n_col] + parts[2 * n_col:3 * n_col]
        return dt, a, cs

    def token_scalars(dt, a, cs):
        tot = jnp.concatenate([jnp.broadcast_to(cs[:, (u + 1) * q - 1:(u + 1) * q], (n_col, q))
                               for u in range(n_sub)], axis=1)
        e_t = jnp.where(fwd_rows, cs, cs - a)
        grow, shrink = jnp.exp(e_t), jnp.exp(tot - e_t)
        w_in = jnp.where(fwd_rows, grow, shrink)
        w_out = jnp.where(fwd_rows, shrink, grow)
        terms = jnp.concatenate(
            [bf16_terms(dt, 2), bf16_terms(w_in, 2), bf16_terms(dt * w_out, 2)], axis=1)
        spread = lax.dot_general(terms, spread_ref[...], contract0,
                                 preferred_element_type=F32)
        e_cols = lax.dot_general(bf16_terms(e_t, 3), ecol_ref[...], contract0,
                                 preferred_element_type=F32)
        return e_t, spread, e_cols

    def into_scratch(vals):
        et_sc[...], spread_sc[...], ecols_sc[...] = vals

    @pl.when(i == 0)
    def _():
        into_scratch(token_scalars(*token_decay_sums(*token_steps(dtf_ref, dtb_ref))))

    next_sums = token_decay_sums(*token_steps(dtf_next_ref, dtb_next_ref))

    zero = jnp.zeros((), BF16)

    def input_matmuls(d, u, xc):
        ch = slice(d * SSD_INNER, (d + 1) * SSD_INNER)
        tok = slice(u * q, (u + 1) * q)
        dt_x, w_in_x, dtw_x = (spread_sc[k * blk + u * q:k * blk + (u + 1) * q, ch] for k in range(3))
        x = xc[tok, 0:SSD_INNER]
        xdt = (x * dt_x).astype(BF16)
        xdtw = (x * dtw_x).astype(BF16)
        b_all = xc[tok, SSD_INNER:SSD_INNER + SSD_GROUPS * SSD_STATE].astype(BF16)
        c_all = xc[tok, SSD_INNER + SSD_GROUPS * SSD_STATE:].astype(BF16)
        hs = h_ref[d]
        y_off = jnp.dot(c_all, hs.astype(BF16), preferred_element_type=F32) * w_in_x
        cbs = [lax.dot_general(jnp.where(lane_half == g, c_all, zero), b_all,
                               (((1,), (1,)), ((), ())), preferred_element_type=F32)
               for g in range(SSD_GROUPS)]
        upd = lax.dot_general(b_all, xdtw, contract0, preferred_element_type=F32)
        exp_tot = w_in_x[q - 1:q, :] if d == 0 else w_in_x[0:1, :]
        h_ref[d] = exp_tot * hs + jnp.where(same_group, upd, 0.0)
        return x, xdt, y_off, cbs

    def decay_matmuls(d, u, x, xdt, y_off, cbs, y_ref):
        tok = slice(u * q, (u + 1) * q)
        for g in range(SSD_GROUPS):
            sl = slice(g * pair, (g + 1) * pair)
            y_g = y_off[:, sl] + dsk_ref[d:d + 1, sl] * x[:, sl]
            for hh in range(heads_per_group):
                c = d * SSD_HEADS + g * heads_per_group + hh
                e_col = ecols_sc[tok, c * q:(c + 1) * q]
                e_row = et_sc[c:c + 1, tok]
                if d == 0:
                    lmat = jnp.where(row >= col, jnp.exp(e_col - e_row), 0.0)
                else:
                    lmat = jnp.where(col >= row, jnp.exp(e_row - e_col), 0.0)
                x_h = jnp.where(lane_half == hh, xdt[:, sl], zero)
                y_g = y_g + jnp.dot((cbs[g] * lmat).astype(BF16), x_h,
                                    preferred_element_type=F32)
            y_ref[tok, sl] = y_g

    refs = ((xf_ref, yf_ref), (xb_ref, yb_ref))
    order = (tuple(range(n_sub)), tuple(reversed(range(n_sub))))
    lin = {}
    for k in range(n_sub):
        for d in range(SSD_DIRS):
            u = order[d][k]
            lin[d, u] = input_matmuls(d, u, refs[d][0])
    for k in range(n_sub):
        for d in range(SSD_DIRS):
            u = order[d][k]
            decay_matmuls(d, u, *lin[d, u], refs[d][1])
    into_scratch(token_scalars(*next_sums))


def _ssd(xc, dt_raw, dt_bias, a_log, d_skip, batch, n_lat, n_ctx):
    n_tok = xc.shape[0]
    q = SSD_CHUNK
    blk = SSD_STEP_CHUNKS * q
    lat_blk = n_lat // batch // blk
    ctx_blk = n_ctx // batch // blk
    assert lat_blk * blk * batch == n_lat and ctx_blk * blk * batch == n_ctx
    n_steps = lat_blk + ctx_blk

    def fwd(b, i):
        return jnp.where(i < ctx_blk, batch * lat_blk + b * ctx_blk + i, b * lat_blk + i - ctx_blk)

    def bwd(b, i):
        return jnp.where(i < ctx_blk, batch * lat_blk + b * ctx_blk + ctx_blk - 1 - i,
                         b * lat_blk + n_steps - 1 - i)

    def specs(ch):
        return [
            pl.BlockSpec((blk, SSD_XBC), lambda b, i: (ch(b, i), 0)),
            pl.BlockSpec((blk, LANES), lambda b, i: (ch(b, i), 0)),
            pl.BlockSpec((blk, LANES), lambda b, i: (ch(b, jnp.minimum(i + 1, n_steps - 1)), 0)),
        ]

    n_col = SSD_DIRS * SSD_HEADS
    rows = lambda v: jnp.broadcast_to(v.reshape(n_col, 1), (n_col, blk))
    dsk = jnp.repeat(d_skip, SSD_HEAD_DIM, axis=1)
    term_row = jnp.arange(2 * BF16_SUBLANES) % n_col
    live3 = jnp.arange(2 * BF16_SUBLANES) < 3 * n_col
    col_of_channel = jnp.arange(SSD_DIRS * SSD_INNER) // SSD_HEAD_DIM
    col_of_block = jnp.arange(n_col * q) // q
    spread = (term_row[:BF16_SUBLANES, None] == col_of_channel[None, :]).astype(BF16)
    ecol = ((term_row[:, None] == col_of_block[None, :]) & live3[:, None]).astype(BF16)
    return pl.pallas_call(
        _ssd_kernel,
        out_shape=[jax.ShapeDtypeStruct((n_tok, SSD_INNER), F32)] * 2,
        grid=(batch, n_steps),
        in_specs=specs(fwd) + specs(bwd) + [
            _const_spec((n_col, blk)), _const_spec((n_col, blk)), _const_spec((SSD_DIRS, SSD_INNER)),
            _const_spec(spread.shape), _const_spec(ecol.shape),
        ],
        out_specs=[pl.BlockSpec((blk, SSD_INNER), lambda b, i: (fwd(b, i), 0)),
                   pl.BlockSpec((blk, SSD_INNER), lambda b, i: (bwd(b, i), 0))],
        scratch_shapes=[pltpu.VMEM((SSD_DIRS, SSD_GROUPS * SSD_STATE, SSD_INNER), F32),
                        pltpu.VMEM((n_col, blk), F32),
                        pltpu.VMEM((3 * blk, SSD_DIRS * SSD_INNER), F32),
                        pltpu.VMEM((blk, n_col * q), F32)],
        compiler_params=pltpu.CompilerParams(
            dimension_semantics=("arbitrary", "arbitrary"), vmem_limit_bytes=VMEM_LIMIT),
        name="ssd_scan",
    )(xc, dt_raw, dt_raw, xc, dt_raw, dt_raw,
      rows(dt_bias), rows(a_log), dsk, spread, ecol)


def _steps_per_trip(n_steps, preferred):
    return max(u for u in range(2, preferred + 1, 2) if n_steps % u == 0) if n_steps else 2


def _scores_t(k, q2):
    return lax.dot_general(k, q2, (((1,), (1,)), ((), ())), preferred_element_type=F32)


def _softmax_stage(s, m_blk, m):
    if m is None:
        return m_blk, None, jnp.exp2((s - m_blk).astype(BF16))
    m_new = jnp.maximum(m, m_blk)
    return m_new, jnp.exp2(m - m_new), jnp.exp2((s - m_new).astype(BF16))


def _attn_kernel(q_ref, kc_ref, vtc_ref, kx_ref, vtx_ref, o_ref, s0_ref, s1_ref, p0_ref, p1_ref,
                 *, n_chunks):
    tq = q_ref.shape[0]
    tk = ATTN_K_TILE
    q2 = jnp.concatenate([q_ref[:, 0:HEAD_DIM], q_ref[:, HEAD_DIM:2 * HEAD_DIM]], axis=0)
    s_refs, p_refs = (s0_ref, s1_ref), (p0_ref, p1_ref)

    def scores_into(slot, c):
        s = _scores_t(kx_ref[pl.ds(pl.multiple_of(c * tk, tk), tk), :], q2)
        s_refs[slot][...] = s
        return jnp.max(s, axis=0, keepdims=True)

    def softmax_into(slot, m_blk, m):
        m, alpha, p = _softmax_stage(s_refs[slot][...], m_blk, m)
        p_refs[slot][...] = p
        return m, alpha

    def step(c, par, carry, with_scores, with_softmax):
        m, acc, alpha, m_blk = carry
        m_blk_next, alpha_next = m_blk, alpha
        if with_scores:
            m_blk_next = scores_into(par, c + 2)
        if with_softmax:
            m, alpha_next = softmax_into(1 - par, m_blk, m)
        acc = alpha * acc + jnp.dot(vtx_ref[c], p_refs[par][...], preferred_element_type=F32)
        return m, acc, alpha_next, m_blk_next

    if n_chunks:
        m_blk0 = scores_into(0, 0)
        m_blk1 = scores_into(1, 1)
    s_ctx = _scores_t(kc_ref[...], q2)
    m, _, p_ctx = _softmax_stage(s_ctx, jnp.max(s_ctx, axis=0, keepdims=True), None)
    acc = jnp.dot(vtc_ref[...], p_ctx, preferred_element_type=F32)
    if n_chunks:
        m, alpha = softmax_into(0, m_blk0, m)
        carry = (m, acc, alpha, m_blk1)
        unroll = _steps_per_trip(n_chunks - 2, ATTN_UNROLL)

        def body(i, carry):
            for u in range(unroll):
                carry = step(unroll * i + u, u % 2, carry, True, True)
            return carry

        carry = lax.fori_loop(0, (n_chunks - 2) // unroll, body, carry)
        carry = step(n_chunks - 2, 0, carry, False, True)
        _, acc, _, _ = step(n_chunks - 1, 1, carry, False, False)
    o = (acc[0:HEAD_DIM] / acc[HEAD_DIM:HEAD_DIM + 1]).T.astype(o_ref.dtype)
    o_ref[:, 0:HEAD_DIM] = o[0:tq]
    o_ref[:, HEAD_DIM:2 * HEAD_DIM] = o[tq:2 * tq]


def _attn_bounded_kernel(q_ref, kc_ref, vtc_ref, kx_ref, vtx_ref, o_ref, p0_ref, p1_ref, *, n_chunks):
    tq = q_ref.shape[0]
    tk = ATTN_K_TILE
    q2 = jnp.concatenate([q_ref[:, 0:HEAD_DIM], q_ref[:, HEAD_DIM:2 * HEAD_DIM]], axis=0)
    p_refs = (p0_ref, p1_ref)

    def colsum(p):
        return jnp.sum(p.reshape(p.shape[0] // SUBLANES, SUBLANES, p.shape[1]), axis=0)

    def weights_into(slot, c):
        p = jnp.exp2(_scores_t(kx_ref[pl.ds(pl.multiple_of(c * tk, tk), tk), :], q2))
        p_refs[slot][...] = p.astype(BF16)
        return colsum(p)

    def step(c, par, carry, with_scores):
        l8, acc = carry
        if with_scores:
            l8 = l8 + weights_into(1 - par, c + 1)
        acc = acc + jnp.dot(vtx_ref[c, 0:HEAD_DIM, :], p_refs[par][...],
                            preferred_element_type=F32)
        return l8, acc

    p_ctx = jnp.exp2(_scores_t(kc_ref[...], q2))
    l8 = colsum(p_ctx) + weights_into(0, 0)
    acc = jnp.dot(vtc_ref[0:HEAD_DIM, :], p_ctx.astype(BF16), preferred_element_type=F32)
    unroll = _steps_per_trip(n_chunks - 2, ATTN_BOUNDED_UNROLL)

    def body(i, carry):
        for u in range(unroll):
            carry = step(unroll * i + u, u % 2, carry, True)
        return carry

    carry = lax.fori_loop(0, (n_chunks - 2) // unroll, body, (l8, acc))
    carry = step(n_chunks - 2, 0, carry, True)
    l8, acc = step(n_chunks - 1, 1, carry, False)
    o = (acc / jnp.sum(l8, axis=0, keepdims=True)).T.astype(o_ref.dtype)
    o_ref[:, 0:HEAD_DIM] = o[0:tq]
    o_ref[:, HEAD_DIM:2 * HEAD_DIM] = o[tq:2 * tq]


def _attention_latent(q, k, vt, batch, n_lat, n_ctx, bounded):
    seq = n_lat // batch
    ctx = n_ctx // batch
    tq, tk = ATTN_Q_TILE, ATTN_K_TILE
    nq = seq // tq
    n_chunks = seq // tk
    assert seq % (2 * tk) == 0 and n_lat % ctx == 0 and n_ctx == tk and vt.shape[2] == tk
    r = 2 * tq
    body = _attn_bounded_kernel if bounded else _attn_kernel
    p_bufs = [pltpu.VMEM((tk, r), BF16), pltpu.VMEM((tk, r), BF16)]
    s_bufs = [] if bounded else [pltpu.VMEM((tk, r), F32), pltpu.VMEM((tk, r), F32)]
    return pl.pallas_call(
        functools.partial(body, n_chunks=n_chunks),
        out_shape=jax.ShapeDtypeStruct((n_lat, ATTN_DIM), BF16),
        grid=(batch, ATTN_KV_HEADS, nq),
        in_specs=[pl.BlockSpec((tq, 2 * HEAD_DIM), lambda b, g, j: (b * nq + j, g)),
                  pl.BlockSpec((ctx, HEAD_DIM), lambda b, g, j: (n_lat // ctx + b, g)),
                  pl.BlockSpec((None, VT_ROWS, ctx), lambda b, g, j: (n_lat // tk, g, b)),
                  pl.BlockSpec((seq, HEAD_DIM), lambda b, g, j: (b, g)),
                  pl.BlockSpec((n_chunks, VT_ROWS, tk), lambda b, g, j: (b, g, 0))],
        out_specs=pl.BlockSpec((tq, 2 * HEAD_DIM), lambda b, g, j: (b * nq + j, g)),
        scratch_shapes=s_bufs + p_bufs,
        compiler_params=pltpu.CompilerParams(
            dimension_semantics=("arbitrary",) * 3, vmem_limit_bytes=VMEM_LIMIT),
        name="attn_latent_bounded" if bounded else "attn_latent",
    )(q, k, vt, k, vt)


def _attn_ctx_kernel(q_ref, kc_ref, vtc_ref, o_ref):
    _attn_kernel(q_ref, kc_ref, vtc_ref, None, None, o_ref, None, None, None, None, n_chunks=0)


def _attention_context(q, k, vt, batch, n_lat, n_ctx):
    ctx = n_ctx // batch
    tk = ATTN_K_TILE
    return pl.pallas_call(
        _attn_ctx_kernel,
        out_shape=jax.ShapeDtypeStruct((n_ctx, ATTN_DIM), BF16),
        grid=(batch, ATTN_KV_HEADS),
        in_specs=[pl.BlockSpec((ctx, 2 * HEAD_DIM), lambda b, g: (n_lat // ctx + b, g)),
                  pl.BlockSpec((ctx, HEAD_DIM), lambda b, g: (n_lat // ctx + b, g)),
                  pl.BlockSpec((None, VT_ROWS, ctx), lambda b, g: (n_lat // tk, g, b))],
        out_specs=pl.BlockSpec((ctx, 2 * HEAD_DIM), lambda b, g: (b, g)),
        compiler_params=pltpu.CompilerParams(
            dimension_semantics=("arbitrary",) * 2, vmem_limit_bytes=VMEM_LIMIT),
        name="attn_context",
    )(q, k, vt)


def _post_kernel(xs_ref, up_ref, upp_ref, upn_ref, z_ref, yf_ref, yb_ref, at_ref, mod_ref,
                 pw_ref, ps_ref, sw_ref, wo_ref, n2_ref, w1_ref, w3_ref, w2_ref,
                 o_ref, buf_ref, *, tiles_per_seq, seq_len):
    i = pl.program_id(0)
    tm = xs_ref.shape[0]
    halo = SUBLANES
    j = i % tiles_per_seq

    buf_ref[0:halo, :] = jnp.where(j == 0, 0.0, upp_ref[...])
    buf_ref[halo:halo + tm, :] = up_ref[...]
    buf_ref[halo + tm:2 * halo + tm, :] = jnp.where(j == tiles_per_seq - 1, 0.0, upn_ref[...])
    ub = buf_ref[...]
    n = tm + 2 * halo
    s2 = ub + pltpu.roll(ub, 1, 0)
    s4 = pltpu.roll(s2, 1, 0) + pltpu.roll(s2, n - 1, 0)
    s8 = pltpu.roll(s4, 2, 0) + pltpu.roll(s4, n - 2, 0)
    s16 = pltpu.roll(s8, 4, 0) + pltpu.roll(s8, n - 4, 0)
    shape = (tm, POOL_DIM)
    grp = lax.broadcasted_iota(jnp.int32, shape, 1) // POOL_GDIM
    t = lax.broadcasted_iota(jnp.int32, shape, 0) + j * tm
    half = jnp.left_shift(1, grp)
    cnt = (jnp.minimum(t + half, seq_len) - jnp.maximum(t - half, 0)).astype(F32)
    sl = slice(halo, halo + tm)
    wsum = jnp.where(grp == 0, s2[sl], jnp.where(grp == 1, s4[sl],
                                                 jnp.where(grp == 2, s8[sl], s16[sl])))
    pooled = wsum / cnt - ub[sl]
    pool = jnp.dot(pooled.astype(BF16), pw_ref[...], preferred_element_type=F32) * ps_ref[...]

    gy = (yf_ref[...] + yb_ref[...]) * _silu(z_ref[...])
    gw = SSD_INNER // SSD_GROUPS
    parts = []
    for g in range(SSD_GROUPS):
        t_g = gy[:, g * gw:(g + 1) * gw]
        parts.append(t_g * lax.rsqrt(jnp.mean(t_g * t_g, axis=-1, keepdims=True) + EPS))
    ssd = jnp.concatenate(parts, axis=1) * sw_ref[...]

    mix = jnp.concatenate([pool.astype(BF16), ssd.astype(BF16), at_ref[...]], axis=1)
    blocks = [slice(r0, r0 + min(POST_ROWS, tm)) for r0 in range(0, tm, POST_ROWS)]
    x1 = [xs_ref[rows, :] + mod_ref[2:3, :] * jnp.dot(mix[rows], wo_ref[...],
                                                       preferred_element_type=F32)
          for rows in blocks]

    def modulated_norm(v):
        h = v * lax.rsqrt(jnp.mean(v * v, axis=-1, keepdims=True) + EPS) * n2_ref[...]
        return (h * (1.0 + mod_ref[4:5, :]) + mod_ref[3:4, :]).astype(BF16)

    gated = []
    for v in x1:
        h = modulated_norm(v)
        a = jnp.dot(h, w1_ref[...], preferred_element_type=F32)
        b = jnp.dot(h, w3_ref[...], preferred_element_type=F32)
        gated.append((_silu(a) * b).astype(BF16))
    for rows, v, gt in zip(blocks, x1, gated):
        ff = jnp.dot(gt, w2_ref[...], preferred_element_type=F32)
        o_ref[rows, :] = v + mod_ref[5:6, :] * ff


def _post(xs, upool, z, yf, yb, attn, mod, pw_bd, pscale, ssd_w, w_out, n2, w1, w3, w2,
          *, tm, row0, tiles_per_seq, mod_row):
    n_rows, d = xs.shape
    n_tiles = n_rows // tm
    off = row0 // tm
    blk8 = tm // SUBLANES
    last8 = upool.shape[0] // SUBLANES - 1
    tok = lambda w: pl.BlockSpec((tm, w), lambda i: (off + i, 0))
    loc = lambda w: pl.BlockSpec((tm, w), lambda i: (i, 0))
    return pl.pallas_call(
        functools.partial(_post_kernel, tiles_per_seq=tiles_per_seq, seq_len=tiles_per_seq * tm),
        out_shape=jax.ShapeDtypeStruct((n_rows, d), F32),
        grid=(n_tiles,),
        in_specs=[
            loc(d),
            tok(POOL_DIM),
            pl.BlockSpec((SUBLANES, POOL_DIM), lambda i: (jnp.maximum((off + i) * blk8 - 1, 0), 0)),
            pl.BlockSpec((SUBLANES, POOL_DIM),
                         lambda i: (jnp.minimum((off + i) * blk8 + blk8, last8), 0)),
            tok(SSD_INNER), tok(SSD_INNER), tok(SSD_INNER),
            loc(ATTN_DIM),
            pl.BlockSpec((None, N_MOD, d), lambda i: (mod_row(i), 0, 0)),
            _const_spec(pw_bd.shape), _const_spec((1, POOL_DIM)), _const_spec((1, SSD_INNER)),
            _const_spec(w_out.shape), _const_spec((1, d)),
            _const_spec(w1.shape), _const_spec(w3.shape), _const_spec(w2.shape),
        ],
        out_specs=loc(d),
        scratch_shapes=[pltpu.VMEM((tm + 2 * SUBLANES, POOL_DIM), F32)],
        compiler_params=pltpu.CompilerParams(
            dimension_semantics=("arbitrary",), vmem_limit_bytes=VMEM_LIMIT),
        name="post_mix_ffn",
    )(xs, upool, upool, upool, z, yf, yb, attn, mod, pw_bd, pscale, ssd_w, w_out, n2, w1, w3, w2)


def _rope_tables(seq, extra_rows):
    rows, pad_rows = seq // GRID_W, extra_rows // GRID_W
    axis_dim = HEAD_DIM // 2
    inv_freq = ROPE_THETA ** (-jnp.arange(0, axis_dim, 2, dtype=F32) / axis_dim)
    ang_r = jnp.arange(rows, dtype=F32)[:, None] * inv_freq[None, :]
    ang_c = jnp.arange(GRID_W, dtype=F32)[:, None] * inv_freq[None, :]
    live = (jnp.arange(rows + pad_rows) < rows)[:, None, None]
    shape = (rows + pad_rows, GRID_W, axis_dim // 2)

    def per_row(t, fill):
        t = jnp.concatenate([t, jnp.full((pad_rows, t.shape[1]), fill, F32)], axis=0)
        return jnp.broadcast_to(t[:, None, :], shape)

    def per_col(t, fill):
        return jnp.where(live, jnp.broadcast_to(t[None, :, :], shape), fill)

    cr, sr = per_row(jnp.cos(ang_r), 1.0), per_row(jnp.sin(ang_r), 0.0)
    cc, sc = per_col(jnp.cos(ang_c), 1.0), per_col(jnp.sin(ang_c), 0.0)
    cos = jnp.concatenate([cr, cr, cc, cc], axis=-1).reshape(-1, HEAD_DIM)
    sin = jnp.concatenate([-sr, sr, -sc, sc], axis=-1).reshape(-1, HEAD_DIM)
    return cos, sin


def _fuse_w_in(w_in):
    d = w_in.shape[0]
    o_dt = POOL_DIM + SSD_INNER + SSD_XBC
    o_att = o_dt + SSD_DIRS * SSD_HEADS
    dt_pad = jnp.zeros((d, C_Q - C_DT - SSD_DIRS * SSD_HEADS), w_in.dtype)
    return jnp.concatenate([w_in[:, :o_dt], w_in[:, o_dt:o_att], dt_pad, w_in[:, o_att:]],
                           axis=1).astype(BF16)


def kernel(x, c, ctx, c_ctx, norm1_w, norm2_w, w_mod, b_mod, w_in, pool_w, pool_scale, conv_w, conv_b,
           dt_bias, a_log, d_skip, ssd_norm_w, q_norm_w, k_norm_w, w_out, w1, w3, w2):
    batch, seq, d = x.shape
    ctx_len = ctx.shape[1]
    depth = w_mod.shape[0]
    n_lat, n_ctx = batch * seq, batch * ctx_len
    assert batch + 1 <= SUBLANES and seq % TOKEN_TILE == 0 and seq % GRID_W == 0

    cond = jnp.zeros((SUBLANES, d), F32).at[:batch].set(c).at[batch].set(c_ctx)
    mod_all = _modulation(cond, w_mod, b_mod).reshape(depth, SUBLANES, N_MOD, d)
    cos_tab, sin_tab = _rope_tables(seq, TOKEN_TILE)

    xs_lat = x.reshape(n_lat, d)
    xs_ctx = ctx.reshape(n_ctx, d)
    tiles_per_seq = seq // TOKEN_TILE
    for layer in range(depth):
        need_ctx = layer < depth - 1
        mod = mod_all[layer]
        w_cat = _fuse_w_in(w_in[layer])
        upool, z, xbc, dt_raw, q, k, vt = _inproj(
            xs_lat, xs_ctx, mod, norm1_w[layer].reshape(1, d), w_cat,
            q_norm_w[layer].reshape(1, -1), k_norm_w[layer].reshape(1, -1), cos_tab, sin_tab, batch)
        xc = _ssd_conv(xbc, conv_w[layer], conv_b[layer], batch, n_lat, n_ctx)
        yf, yb = _ssd(xc, dt_raw, dt_bias[layer], a_log[layer], d_skip[layer], batch, n_lat, n_ctx)
        score_bound = (HEAD_DIM * Q_SCALE * SCORE_BOUND_SLACK * jnp.max(jnp.abs(q_norm_w[layer]))
                       * jnp.max(jnp.abs(k_norm_w[layer])))
        attn_x = lax.cond(
            score_bound < SCORE_BOUND_LIMIT,
            functools.partial(_attention_latent, batch=batch, n_lat=n_lat, n_ctx=n_ctx, bounded=True),
            functools.partial(_attention_latent, batch=batch, n_lat=n_lat, n_ctx=n_ctx, bounded=False),
            q, k, vt)
        pw_bd = jax.scipy.linalg.block_diag(*[pool_w[layer, g] for g in range(POOL_GROUPS)]).astype(BF16)
        post = functools.partial(
            _post, upool=upool, z=z, yf=yf, yb=yb, mod=mod, pw_bd=pw_bd,
            pscale=pool_scale[layer].reshape(1, -1), ssd_w=ssd_norm_w[layer].reshape(1, -1),
            w_out=w_out[layer].astype(BF16), n2=norm2_w[layer].reshape(1, d),
            w1=w1[layer].astype(BF16), w3=w3[layer].astype(BF16), w2=w2[layer].astype(BF16))
        new_lat = post(xs_lat, attn=attn_x, tm=TOKEN_TILE, row0=0, tiles_per_seq=tiles_per_seq,
                       mod_row=lambda i: i // tiles_per_seq)
        if need_ctx:
            attn_c = _attention_context(q, k, vt, batch, n_lat, n_ctx)
            xs_ctx = post(xs_ctx, attn=attn_c, tm=ctx_len, row0=n_lat, tiles_per_seq=1,
                          mod_row=lambda i: batch)
        xs_lat = new_lat
    return xs_lat.reshape(batch, seq, d)
```

```python
import functools

import jax
import jax.numpy as jnp
from jax import lax
from jax.experimental import pallas as pl
from jax.experimental.pallas import tpu as pltpu

F32 = jnp.float32
BF16 = jnp.bfloat16

GRID_W = 64
EPS = 1e-6
N_MOD = 6
POOL_DIM = 256
POOL_GDIM = 64
POOL_GROUPS = 4
SSD_HEADS = 4
SSD_HEAD_DIM = 64
SSD_INNER = 256
SSD_GROUPS = 2
SSD_STATE = 64
SSD_CONV = 5
SSD_CHUNK = 128
SSD_STEP_CHUNKS = 2
SSD_DIRS = 2
SSD_XBC = 512
ATTN_HEADS = 4
ATTN_KV_HEADS = 2
HEAD_DIM = 128
ATTN_DIM = 512
KV_DIM = 256
ROPE_THETA = 10000.0
Q_SCALE = HEAD_DIM ** -0.5 * 1.4426950408889634
SCORE_BOUND_LIMIT = 60.0
SCORE_BOUND_SLACK = 1.02

LANES = 128
SUBLANES = 8
TOKEN_TILE = 512
INPROJ_ROWS = 256
CONV_TILE = 1024
POST_ROWS = 256
ATTN_Q_TILE = 1024
ATTN_K_TILE = 512
ATTN_UNROLL = 6
ATTN_BOUNDED_UNROLL = 30
BF16_SUBLANES = 16
VT_ROWS = HEAD_DIM + BF16_SUBLANES
VMEM_LIMIT = 56 * 1024 * 1024

C_POOL = 0
C_Z = C_POOL + POOL_DIM
C_XBC = C_Z + SSD_INNER
C_DT = C_XBC + SSD_XBC
C_Q = C_DT + LANES
C_K = C_Q + ATTN_DIM
C_V = C_K + KV_DIM
C_END = C_V + KV_DIM


def _silu(x):
    return x * (1.0 / (1.0 + jnp.exp(-x)))


def _softplus(x):
    return jnp.maximum(x, 0.0) + jnp.log1p(jnp.exp(-jnp.abs(x)))


def _const_spec(shape):
    nd = len(shape)
    return pl.BlockSpec(shape, lambda *_: (0,) * nd, pipeline_mode=pl.Buffered(1))


def _mod_kernel(cond_ref, w_ref, b_ref, o_ref):
    s = _silu(cond_ref[...]).astype(BF16)
    o_ref[...] = jnp.dot(s, w_ref[...].astype(BF16), preferred_element_type=F32) + b_ref[...]


def _modulation(cond, w_mod, b_mod):
    depth, d, n = w_mod.shape
    bn = d
    return pl.pallas_call(
        _mod_kernel,
        out_shape=jax.ShapeDtypeStruct((depth, SUBLANES, n), F32),
        grid=(depth, n // bn),
        in_specs=[
            pl.BlockSpec((SUBLANES, d), lambda l, j: (0, 0)),
            pl.BlockSpec((None, d, bn), lambda l, j: (l, 0, j)),
            pl.BlockSpec((None, 1, bn), lambda l, j: (l, 0, j)),
        ],
        out_specs=pl.BlockSpec((None, SUBLANES, bn), lambda l, j: (l, 0, j)),
        compiler_params=pltpu.CompilerParams(
            dimension_semantics=("arbitrary", "arbitrary"), vmem_limit_bytes=VMEM_LIMIT),
        name="modulation",
    )(cond, w_mod, b_mod.reshape(depth, 1, n))


def _rope_partner(x):
    lane = lax.broadcasted_iota(jnp.int32, x.shape, 1)
    fwd = pltpu.roll(x, LANES - 32, 1)
    bwd = pltpu.roll(x, 32, 1)
    return jnp.where((lane % 64) < 32, fwd, bwd)


def _inproj_kernel(xl_ref, xc_ref, mod_ref, n1_ref, w_ref, qw_ref, kw_ref, cos_ref, sin_ref,
                   pool_ref, z_ref, xbc_ref, dt_ref, q_ref, k_ref, vt_ref, *, n_lat_tiles):
    i = pl.program_id(0)
    is_lat = i < n_lat_tiles
    tm = xl_ref.shape[0]
    for r0 in range(0, tm, INPROJ_ROWS):
        rows = slice(r0, r0 + INPROJ_ROWS)
        x = jnp.where(is_lat, xl_ref[rows, :], xc_ref[rows, :])
        ms = jnp.mean(x * x, axis=-1, keepdims=True)
        h = x * lax.rsqrt(ms + EPS) * n1_ref[...]
        h = h * (1.0 + mod_ref[1:2, :]) + mod_ref[0:1, :]
        u = jnp.dot(h.astype(BF16), w_ref[...], preferred_element_type=F32)
        pool_ref[rows, :] = u[:, C_POOL:C_Z]
        z_ref[rows, :] = u[:, C_Z:C_XBC]
        xbc_ref[rows, :] = u[:, C_XBC:C_DT]
        dt_ref[rows, :] = u[:, C_DT:C_Q]
        vt = u[:, C_V:C_END].T.astype(BF16)
        for hd in range(ATTN_KV_HEADS):
            vt_ref[hd * VT_ROWS:hd * VT_ROWS + HEAD_DIM, rows] = vt[hd * HEAD_DIM:(hd + 1) * HEAD_DIM]
            vt_ref[hd * VT_ROWS + HEAD_DIM:(hd + 1) * VT_ROWS, rows] = jnp.ones(
                (VT_ROWS - HEAD_DIM, INPROJ_ROWS), BF16)
        cos = cos_ref[rows, :]
        sin = sin_ref[rows, :]

        def norm_rope(t, w):
            t = t * lax.rsqrt(jnp.mean(t * t, axis=-1, keepdims=True) + EPS) * w
            return t * cos + _rope_partner(t) * sin

        for hd in range(ATTN_HEADS):
            t = norm_rope(u[:, C_Q + hd * HEAD_DIM:C_Q + (hd + 1) * HEAD_DIM], qw_ref[...])
            q_ref[rows, hd * HEAD_DIM:(hd + 1) * HEAD_DIM] = (t * Q_SCALE).astype(BF16)
        for hd in range(ATTN_KV_HEADS):
            t = norm_rope(u[:, C_K + hd * HEAD_DIM:C_K + (hd + 1) * HEAD_DIM], kw_ref[...])
            k_ref[rows, hd * HEAD_DIM:(hd + 1) * HEAD_DIM] = t.astype(BF16)


def _inproj(xs_lat, xs_ctx, mod, n1, w_cat, qw, kw, cos_tab, sin_tab, batch):
    n_lat, d = xs_lat.shape
    n_ctx = xs_ctx.shape[0]
    tm = TOKEN_TILE
    assert n_ctx == tm and n_lat % (batch * tm) == 0
    n_lat_tiles = n_lat // tm
    tiles_per_seq = n_lat_tiles // batch
    n_tok = n_lat + n_ctx
    grid = (n_lat_tiles + 1,)

    def lat_idx(i):
        return jnp.minimum(i, n_lat_tiles - 1)

    def rope_idx(i):
        return jnp.where(i < n_lat_tiles, i % tiles_per_seq, tiles_per_seq)

    def mod_idx(i):
        return jnp.where(i < n_lat_tiles, i // tiles_per_seq, batch)

    row = lambda w: pl.BlockSpec((tm, w), lambda i: (i, 0))
    outs = [(POOL_DIM, F32), (SSD_INNER, F32), (SSD_XBC, F32), (LANES, F32),
            (ATTN_DIM, BF16), (KV_DIM, BF16)]
    return pl.pallas_call(
        functools.partial(_inproj_kernel, n_lat_tiles=n_lat_tiles),
        out_shape=[jax.ShapeDtypeStruct((n_tok, w), t) for w, t in outs]
        + [jax.ShapeDtypeStruct((n_tok // tm, ATTN_KV_HEADS * VT_ROWS, tm), BF16)],
        grid=grid,
        in_specs=[
            pl.BlockSpec((tm, d), lambda i: (lat_idx(i), 0)),
            pl.BlockSpec((tm, d), lambda i: (0, 0)),
            pl.BlockSpec((None, N_MOD, d), lambda i: (mod_idx(i), 0, 0)),
            _const_spec((1, d)),
            _const_spec(w_cat.shape),
            _const_spec((1, HEAD_DIM)),
            _const_spec((1, HEAD_DIM)),
            pl.BlockSpec((tm, HEAD_DIM), lambda i: (rope_idx(i), 0)),
            pl.BlockSpec((tm, HEAD_DIM), lambda i: (rope_idx(i), 0)),
        ],
        out_specs=[row(w) for w, _ in outs]
        + [pl.BlockSpec((None, ATTN_KV_HEADS * VT_ROWS, tm), lambda i: (i, 0, 0))],
        compiler_params=pltpu.CompilerParams(
            dimension_semantics=("arbitrary",), vmem_limit_bytes=VMEM_LIMIT),
        name="inproj",
    )(xs_lat, xs_ctx, mod, n1, w_cat, qw, kw, cos_tab, sin_tab)


def _conv_kernel(x_ref, prev_ref, next_ref, ctx_ref, cw_ref, cb_ref, o_ref, buf_ref,
                 *, tiles_per_seq, n_lat_tiles, ctx_len):
    i = pl.program_id(0)
    tm = x_ref.shape[0]
    pad = SSD_CONV // 2

    def conv_silu(rows):
        u = buf_ref[0:rows + 2 * SUBLANES, :]
        n = u.shape[0]
        acc = cb_ref[...] + cw_ref[pad:pad + 1, :] * u[SUBLANES:SUBLANES + rows]
        for k in range(SSD_CONV):
            if k != pad:
                shifted = pltpu.roll(u, (pad - k) % n, 0)
                acc = acc + cw_ref[k:k + 1, :] * shifted[SUBLANES:SUBLANES + rows]
        return _silu(acc)

    @pl.when(i < n_lat_tiles)
    def _():
        first = i % tiles_per_seq == 0
        last = i % tiles_per_seq == tiles_per_seq - 1
        buf_ref[0:SUBLANES, :] = jnp.where(first, 0.0, prev_ref[...])
        buf_ref[SUBLANES:SUBLANES + tm, :] = x_ref[...]
        buf_ref[SUBLANES + tm:2 * SUBLANES + tm, :] = jnp.where(last, 0.0, next_ref[...])
        o_ref[...] = conv_silu(tm)

    @pl.when(i == n_lat_tiles)
    def _():
        n_ctx = ctx_ref.shape[0]
        halo = jnp.zeros((SUBLANES, SSD_XBC), F32)
        for r0 in range(0, n_ctx, ctx_len):
            buf_ref[0:SUBLANES, :] = halo
            buf_ref[SUBLANES:SUBLANES + ctx_len, :] = ctx_ref[r0:r0 + ctx_len, :]
            buf_ref[SUBLANES + ctx_len:2 * SUBLANES + ctx_len, :] = halo
            o_ref[r0:r0 + ctx_len, :] = conv_silu(ctx_len)
        o_ref[n_ctx:tm, :] = jnp.zeros((tm - n_ctx, SSD_XBC), F32)


def _ssd_conv(xbc, conv_w, conv_b, batch, n_lat, n_ctx):
    n_tok = xbc.shape[0]
    seq, ctx = n_lat // batch, n_ctx // batch
    tm = CONV_TILE
    assert seq % tm == 0 and n_lat % n_ctx == 0 and n_ctx < tm
    n_lat_tiles = n_lat // tm
    blk8 = tm // SUBLANES
    lat = lambda i: jnp.minimum(i, n_lat_tiles - 1)
    cw = jnp.zeros((SUBLANES, SSD_XBC), F32).at[:SSD_CONV].set(conv_w)
    return pl.pallas_call(
        functools.partial(_conv_kernel, tiles_per_seq=seq // tm, n_lat_tiles=n_lat_tiles,
                          ctx_len=ctx),
        out_shape=jax.ShapeDtypeStruct((n_tok, SSD_XBC), F32),
        grid=(n_lat_tiles + 1,),
        in_specs=[
            pl.BlockSpec((tm, SSD_XBC), lambda i: (lat(i), 0)),
            pl.BlockSpec((SUBLANES, SSD_XBC), lambda i: (jnp.maximum(lat(i) * blk8 - 1, 0), 0)),
            pl.BlockSpec((SUBLANES, SSD_XBC), lambda i: (lat(i) * blk8 + blk8, 0)),
            pl.BlockSpec((n_ctx, SSD_XBC), lambda i: (n_lat // n_ctx, 0)),
            _const_spec((SUBLANES, SSD_XBC)), _const_spec((1, SSD_XBC)),
        ],
        out_specs=pl.BlockSpec((tm, SSD_XBC), lambda i: (i, 0)),
        scratch_shapes=[pltpu.VMEM((tm + 2 * SUBLANES, SSD_XBC), F32)],
        compiler_params=pltpu.CompilerParams(
            dimension_semantics=("arbitrary",), vmem_limit_bytes=VMEM_LIMIT),
        name="ssd_conv",
    )(xbc, xbc, xbc, xbc, cw, conv_b.reshape(1, -1))


def _ssd_kernel(xf_ref, dtf_ref, dtf_next_ref, xb_ref, dtb_ref, dtb_next_ref,
                bias_ref, alog_ref, dsk_ref, spread_ref, ecol_ref,
                yf_ref, yb_ref, h_ref, et_sc, spread_sc, ecols_sc):
    i = pl.program_id(1)
    q = SSD_CHUNK
    n_sub = SSD_STEP_CHUNKS
    blk = n_sub * q

    @pl.when(i == 0)
    def _():
        h_ref[...] = jnp.zeros_like(h_ref)

    row = lax.broadcasted_iota(jnp.int32, (q, q), 0)
    col = lax.broadcasted_iota(jnp.int32, (q, q), 1)
    n_col = SSD_DIRS * SSD_HEADS
    heads_per_group = SSD_HEADS // SSD_GROUPS
    pair = heads_per_group * SSD_HEAD_DIM
    lane_half = col // SSD_HEAD_DIM
    st_row = lax.broadcasted_iota(jnp.int32, (SSD_GROUPS * SSD_STATE, SSD_INNER), 0)
    st_col = lax.broadcasted_iota(jnp.int32, (SSD_GROUPS * SSD_STATE, SSD_INNER), 1)
    same_group = (st_row // SSD_STATE) == (st_col // pair)
    contract0 = (((0,), (0,)), ((), ()))

    def bf16_terms(v, n_terms):
        terms, rest = [], v
        for _ in range(n_terms):
            t = rest.astype(BF16).astype(F32)
            terms.append(t)
            rest = rest - t
        rows = -(-n_terms * n_col // BF16_SUBLANES) * BF16_SUBLANES
        if rows > n_terms * n_col:
            terms.append(jnp.zeros((rows - n_terms * n_col, v.shape[1]), F32))
        return jnp.concatenate(terms, axis=0).astype(BF16)

    fwd_rows = lax.broadcasted_iota(jnp.int32, (n_col, blk), 0) < SSD_HEADS

    def token_steps(dtf_blk_ref, dtb_blk_ref):
        dt_raw = jnp.where(fwd_rows, dtf_blk_ref[...].T[0:n_col], dtb_blk_ref[...].T[0:n_col])
        dt = _softplus(dt_raw + bias_ref[...])
        return dt, dt * -jnp.exp(alog_ref[...])

    def token_decay_sums(dt, a):
        r = lax.broadcasted_iota(jnp.int32, (blk, blk), 0)
        c = lax.broadcasted_iota(jnp.int32, (blk, blk), 1)
        tri_u = jnp.logical_and(r <= c, r // q == c // q).astype(BF16)
        parts = jnp.dot(bf16_terms(a, 3), tri_u, preferred_element_type=F32)
        cs = parts[0:n_col] + parts[n_col:2 * n_col] + parts[2 * n_col:3 * n_col]
        return dt, a, cs

    def token_scalars(dt, a, cs):
        tot = jnp.concatenate([jnp.broadcast_to(cs[:, (u + 1) * q - 1:(u + 1) * q], (n_col, q))
                               for u in range(n_sub)], axis=1)
        e_t = jnp.where(fwd_rows, cs, cs - a)
        grow, shrink = jnp.exp(e_t), jnp.exp(tot - e_t)
        w_in = jnp.where(fwd_rows, grow, shrink)
        w_out = jnp.where(fwd_rows, shrink, grow)
        terms = jnp.concatenate(
            [bf16_terms(dt, 2), bf16_terms(w_in, 2), bf16_terms(dt * w_out, 2)], axis=1)
        spread = lax.dot_general(terms, spread_ref[...], contract0,
                                 preferred_element_type=F32)
        e_cols = lax.dot_general(bf16_terms(e_t, 3), ecol_ref[...], contract0,
                                 preferred_element_type=F32)
        return e_t, spread, e_cols

    def into_scratch(vals):
        et_sc[...], spread_sc[...], ecols_sc[...] = vals

    @pl.when(i == 0)
    def _():
        into_scratch(token_scalars(*token_decay_sums(*token_steps(dtf_ref, dtb_ref))))

    next_sums = token_decay_sums(*token_steps(dtf_next_ref, dtb_next_ref))

    zero = jnp.zeros((), BF16)

    def input_matmuls(d, u, xc):
        ch = slice(d * SSD_INNER, (d + 1) * SSD_INNER)
        tok = slice(u * q, (u + 1) * q)
        dt_x, w_in_x, dtw_x = (spread_sc[k * blk + u * q:k * blk + (u + 1) * q, ch] for k in range(3))
        x = xc[tok, 0:SSD_INNER]
        xdt = (x * dt_x).astype(BF16)
        xdtw = (x * dtw_x).astype(BF16)
        b_all = xc[tok, SSD_INNER:SSD_INNER + SSD_GROUPS * SSD_STATE].astype(BF16)
        c_all = xc[tok, SSD_INNER + SSD_GROUPS * SSD_STATE:].astype(BF16)
        hs = h_ref[d]
        y_off = jnp.dot(c_all, hs.astype(BF16), preferred_element_type=F32) * w_in_x
        cbs = [lax.dot_general(jnp.where(lane_half == g, c_all, zero), b_all,
                               (((1,), (1,)), ((), ())), preferred_element_type=F32)
               for g in range(SSD_GROUPS)]
        upd = lax.dot_general(b_all, xdtw, contract0, preferred_element_type=F32)
        exp_tot = w_in_x[q - 1:q, :] if d == 0 else w_in_x[0:1, :]
        h_ref[d] = exp_tot * hs + jnp.where(same_group, upd, 0.0)
        return x, xdt, y_off, cbs

    def decay_matmuls(d, u, x, xdt, y_off, cbs, y_ref):
        tok = slice(u * q, (u + 1) * q)
        for g in range(SSD_GROUPS):
            sl = slice(g * pair, (g + 1) * pair)
            y_g = y_off[:, sl] + dsk_ref[d:d + 1, sl] * x[:, sl]
            for hh in range(heads_per_group):
                c = d * SSD_HEADS + g * heads_per_group + hh
                e_col = ecols_sc[tok, c * q:(c + 1) * q]
                e_row = et_sc[c:c + 1, tok]
                if d == 0:
                    lmat = jnp.where(row >= col, jnp.exp(e_col - e_row), 0.0)
                else:
                    lmat = jnp.where(col >= row, jnp.exp(e_row - e_col), 0.0)
                x_h = jnp.where(lane_half == hh, xdt[:, sl], zero)
                y_g = y_g + jnp.dot((cbs[g] * lmat).astype(BF16), x_h,
                                    preferred_element_type=F32)
            y_ref[tok, sl] = y_g

    refs = ((xf_ref, yf_ref), (xb_ref, yb_ref))
    order = (tuple(range(n_sub)), tuple(reversed(range(n_sub))))
    lin = {}
    for k in range(n_sub):
        for d in range(SSD_DIRS):
            u = order[d][k]
            lin[d, u] = input_matmuls(d, u, refs[d][0])
    for k in range(n_sub):
        for d in range(SSD_DIRS):
            u = order[d][k]
            decay_matmuls(d, u, *lin[d, u], refs[d][1])
    into_scratch(token_scalars(*next_sums))


def _ssd(xc, dt_raw, dt_bias, a_log, d_skip, batch, n_lat, n_ctx):
    n_tok = xc.shape[0]
    q = SSD_CHUNK
    blk = SSD_STEP_CHUNKS * q
    lat_blk = n_lat // batch // blk
    ctx_blk = n_ctx // batch // blk
    assert lat_blk * blk * batch == n_lat and ctx_blk * blk * batch == n_ctx
    n_steps = lat_blk + ctx_blk

    def fwd(b, i):
        return jnp.where(i < ctx_blk, batch * lat_blk + b * ctx_blk + i, b * lat_blk + i - ctx_blk)

    def bwd(b, i):
        return jnp.where(i < ctx_blk, batch * lat_blk + b * ctx_blk + ctx_blk - 1 - i,
                         b * lat_blk + n_steps - 1 - i)

    def specs(ch):
        return [
            pl.BlockSpec((blk, SSD_XBC), lambda b, i: (ch(b, i), 0)),
            pl.BlockSpec((blk, LANES), lambda b, i: (ch(b, i), 0)),
            pl.BlockSpec((blk, LANES), lambda b, i: (ch(b, jnp.minimum(i + 1, n_steps - 1)), 0)),
        ]

    n_col = SSD_DIRS * SSD_HEADS
    rows = lambda v: jnp.broadcast_to(v.reshape(n_col, 1), (n_col, blk))
    dsk = jnp.repeat(d_skip, SSD_HEAD_DIM, axis=1)
    term_row = jnp.arange(2 * BF16_SUBLANES) % n_col
    live3 = jnp.arange(2 * BF16_SUBLANES) < 3 * n_col
    col_of_channel = jnp.arange(SSD_DIRS * SSD_INNER) // SSD_HEAD_DIM
    col_of_block = jnp.arange(n_col * q) // q
    spread = (term_row[:BF16_SUBLANES, None] == col_of_channel[None, :]).astype(BF16)
    ecol = ((term_row[:, None] == col_of_block[None, :]) & live3[:, None]).astype(BF16)
    return pl.pallas_call(
        _ssd_kernel,
        out_shape=[jax.ShapeDtypeStruct((n_tok, SSD_INNER), F32)] * 2,
        grid=(batch, n_steps),
        in_specs=specs(fwd) + specs(bwd) + [
            _const_spec((n_col, blk)), _const_spec((n_col, blk)), _const_spec((SSD_DIRS, SSD_INNER)),
            _const_spec(spread.shape), _const_spec(ecol.shape),
        ],
        out_specs=[pl.BlockSpec((blk, SSD_INNER), lambda b, i: (fwd(b, i), 0)),
                   pl.BlockSpec((blk, SSD_INNER), lambda b, i: (bwd(b, i), 0))],
        scratch_shapes=[pltpu.VMEM((SSD_DIRS, SSD_GROUPS * SSD_STATE, SSD_INNER), F32),
                        pltpu.VMEM((n_col, blk), F32),
                        pltpu.VMEM((3 * blk, SSD_DIRS * SSD_INNER), F32),
                        pltpu.VMEM((blk, n_col * q), F32)],
        compiler_params=pltpu.CompilerParams(
            dimension_semantics=("arbitrary", "arbitrary"), vmem_limit_bytes=VMEM_LIMIT),
        name="ssd_scan",
    )(xc, dt_raw, dt_raw, xc, dt_raw, dt_raw,
      rows(dt_bias), rows(a_log), dsk, spread, ecol)


def _steps_per_trip(n_steps, preferred):
    return max(u for u in range(2, preferred + 1, 2) if n_steps % u == 0) if n_steps else 2


def _scores_t(k, q2):
    return lax.dot_general(k, q2, (((1,), (1,)), ((), ())), preferred_element_type=F32)


def _softmax_stage(s, m_blk, m):
    if m is None:
        return m_blk, None, jnp.exp2((s - m_blk).astype(BF16))
    m_new = jnp.maximum(m, m_blk)
    return m_new, jnp.exp2(m - m_new), jnp.exp2((s - m_new).astype(BF16))


def _attn_kernel(q_ref, kc_ref, vtc_ref, kx_ref, vtx_ref, o_ref, s0_ref, s1_ref, p0_ref, p1_ref,
                 *, n_chunks):
    tq = q_ref.shape[0]
    tk = ATTN_K_TILE
    q2 = jnp.concatenate([q_ref[:, 0:HEAD_DIM], q_ref[:, HEAD_DIM:2 * HEAD_DIM]], axis=0)
    s_refs, p_refs = (s0_ref, s1_ref), (p0_ref, p1_ref)

    def scores_into(slot, c):
        s = _scores_t(kx_ref[pl.ds(pl.multiple_of(c * tk, tk), tk), :], q2)
        s_refs[slot][...] = s
        return jnp.max(s, axis=0, keepdims=True)

    def softmax_into(slot, m_blk, m):
        m, alpha, p = _softmax_stage(s_refs[slot][...], m_blk, m)
        p_refs[slot][...] = p
        return m, alpha

    def step(c, par, carry, with_scores, with_softmax):
        m, acc, alpha, m_blk = carry
        m_blk_next, alpha_next = m_blk, alpha
        if with_scores:
            m_blk_next = scores_into(par, c + 2)
        if with_softmax:
            m, alpha_next = softmax_into(1 - par, m_blk, m)
        acc = alpha * acc + jnp.dot(vtx_ref[c], p_refs[par][...], preferred_element_type=F32)
        return m, acc, alpha_next, m_blk_next

    if n_chunks:
        m_blk0 = scores_into(0, 0)
        m_blk1 = scores_into(1, 1)
    s_ctx = _scores_t(kc_ref[...], q2)
    m, _, p_ctx = _softmax_stage(s_ctx, jnp.max(s_ctx, axis=0, keepdims=True), None)
    acc = jnp.dot(vtc_ref[...], p_ctx, preferred_element_type=F32)
    if n_chunks:
        m, alpha = softmax_into(0, m_blk0, m)
        carry = (m, acc, alpha, m_blk1)
        unroll = _steps_per_trip(n_chunks - 2, ATTN_UNROLL)

        def body(i, carry):
            for u in range(unroll):
                carry = step(unroll * i + u, u % 2, carry, True, True)
            return carry

        carry = lax.fori_loop(0, (n_chunks - 2) // unroll, body, carry)
        carry = step(n_chunks - 2, 0, carry, False, True)
        _, acc, _, _ = step(n_chunks - 1, 1, carry, False, False)
    o = (acc[0:HEAD_DIM] / acc[HEAD_DIM:HEAD_DIM + 1]).T.astype(o_ref.dtype)
    o_ref[:, 0:HEAD_DIM] = o[0:tq]
    o_ref[:, HEAD_DIM:2 * HEAD_DIM] = o[tq:2 * tq]


def _attn_bounded_kernel(q_ref, kc_ref, vtc_ref, kx_ref, vtx_ref, o_ref, p0_ref, p1_ref, *, n_chunks):
    tq = q_ref.shape[0]
    tk = ATTN_K_TILE
    q2 = jnp.concatenate([q_ref[:, 0:HEAD_DIM], q_ref[:, HEAD_DIM:2 * HEAD_DIM]], axis=0)
    p_refs = (p0_ref, p1_ref)

    def colsum(p):
        return jnp.sum(p.reshape(p.shape[0] // SUBLANES, SUBLANES, p.shape[1]), axis=0)

    def weights_into(slot, c):
        p = jnp.exp2(_scores_t(kx_ref[pl.ds(pl.multiple_of(c * tk, tk), tk), :], q2))
        p_refs[slot][...] = p.astype(BF16)
        return colsum(p)

    def step(c, par, carry, with_scores):
        l8, acc = carry
        if with_scores:
            l8 = l8 + weights_into(1 - par, c + 1)
        acc = acc + jnp.dot(vtx_ref[c, 0:HEAD_DIM, :], p_refs[par][...],
                            preferred_element_type=F32)
        return l8, acc

    p_ctx = jnp.exp2(_scores_t(kc_ref[...], q2))
    l8 = colsum(p_ctx) + weights_into(0, 0)
    acc = jnp.dot(vtc_ref[0:HEAD_DIM, :], p_ctx.astype(BF16), preferred_element_type=F32)
    unroll = _steps_per_trip(n_chunks - 2, ATTN_BOUNDED_UNROLL)

    def body(i, carry):
        for u in range(unroll):
            carry = step(unroll * i + u, u % 2, carry, True)
        return carry

    carry = lax.fori_loop(0, (n_chunks - 2) // unroll, body, (l8, acc))
    carry = step(n_chunks - 2, 0, carry, True)
    l8, acc = step(n_chunks - 1, 1, carry, False)
    o = (acc / jnp.sum(l8, axis=0, keepdims=True)).T.astype(o_ref.dtype)
    o_ref[:, 0:HEAD_DIM] = o[0:tq]
    o_ref[:, HEAD_DIM:2 * HEAD_DIM] = o[tq:2 * tq]


def _attention_latent(q, k, vt, batch, n_lat, n_ctx, bounded):
    seq = n_lat // batch
    ctx = n_ctx // batch
    tq, tk = ATTN_Q_TILE, ATTN_K_TILE
    nq = seq // tq
    n_chunks = seq // tk
    assert seq % (2 * tk) == 0 and n_lat % ctx == 0 and n_ctx == tk and vt.shape[2] == tk
    r = 2 * tq
    body = _attn_bounded_kernel if bounded else _attn_kernel
    p_bufs = [pltpu.VMEM((tk, r), BF16), pltpu.VMEM((tk, r), BF16)]
    s_bufs = [] if bounded else [pltpu.VMEM((tk, r), F32), pltpu.VMEM((tk, r), F32)]
    return pl.pallas_call(
        functools.partial(body, n_chunks=n_chunks),
        out_shape=jax.ShapeDtypeStruct((n_lat, ATTN_DIM), BF16),
        grid=(batch, ATTN_KV_HEADS, nq),
        in_specs=[pl.BlockSpec((tq, 2 * HEAD_DIM), lambda b, g, j: (b * nq + j, g)),
                  pl.BlockSpec((ctx, HEAD_DIM), lambda b, g, j: (n_lat // ctx + b, g)),
                  pl.BlockSpec((None, VT_ROWS, ctx), lambda b, g, j: (n_lat // tk, g, b)),
                  pl.BlockSpec((seq, HEAD_DIM), lambda b, g, j: (b, g)),
                  pl.BlockSpec((n_chunks, VT_ROWS, tk), lambda b, g, j: (b, g, 0))],
        out_specs=pl.BlockSpec((tq, 2 * HEAD_DIM), lambda b, g, j: (b * nq + j, g)),
        scratch_shapes=s_bufs + p_bufs,
        compiler_params=pltpu.CompilerParams(
            dimension_semantics=("arbitrary",) * 3, vmem_limit_bytes=VMEM_LIMIT),
        name="attn_latent_bounded" if bounded else "attn_latent",
    )(q, k, vt, k, vt)


def _attn_ctx_kernel(q_ref, kc_ref, vtc_ref, o_ref):
    _attn_kernel(q_ref, kc_ref, vtc_ref, None, None, o_ref, None, None, None, None, n_chunks=0)


def _attention_context(q, k, vt, batch, n_lat, n_ctx):
    ctx = n_ctx // batch
    tk = ATTN_K_TILE
    return pl.pallas_call(
        _attn_ctx_kernel,
        out_shape=jax.ShapeDtypeStruct((n_ctx, ATTN_DIM), BF16),
        grid=(batch, ATTN_KV_HEADS),
        in_specs=[pl.BlockSpec((ctx, 2 * HEAD_DIM), lambda b, g: (n_lat // ctx + b, g)),
                  pl.BlockSpec((ctx, HEAD_DIM), lambda b, g: (n_lat // ctx + b, g)),
                  pl.BlockSpec((None, VT_ROWS, ctx), lambda b, g: (n_lat // tk, g, b))],
        out_specs=pl.BlockSpec((ctx, 2 * HEAD_DIM), lambda b, g: (b, g)),
        compiler_params=pltpu.CompilerParams(
            dimension_semantics=("arbitrary",) * 2, vmem_limit_bytes=VMEM_LIMIT),
        name="attn_context",
    )(q, k, vt)


def _post_kernel(xs_ref, up_ref, upp_ref, upn_ref, z_ref, yf_ref, yb_ref, at_ref, mod_ref,
                 pw_ref, ps_ref, sw_ref, wo_ref, n2_ref, w1_ref, w3_ref, w2_ref,
                 o_ref, buf_ref, *, tiles_per_seq, seq_len):
    i = pl.program_id(0)
    tm = xs_ref.shape[0]
    halo = SUBLANES
    j = i % tiles_per_seq

    buf_ref[0:halo, :] = jnp.where(j == 0, 0.0, upp_ref[...])
    buf_ref[halo:halo + tm, :] = up_ref[...]
    buf_ref[halo + tm:2 * halo + tm, :] = jnp.where(j == tiles_per_seq - 1, 0.0, upn_ref[...])
    ub = buf_ref[...]
    n = tm + 2 * halo
    s2 = ub + pltpu.roll(ub, 1, 0)
    s4 = pltpu.roll(s2, 1, 0) + pltpu.roll(s2, n - 1, 0)
    s8 = pltpu.roll(s4, 2, 0) + pltpu.roll(s4, n - 2, 0)
    s16 = pltpu.roll(s8, 4, 0) + pltpu.roll(s8, n - 4, 0)
    shape = (tm, POOL_DIM)
    grp = lax.broadcasted_iota(jnp.int32, shape, 1) // POOL_GDIM
    t = lax.broadcasted_iota(jnp.int32, shape, 0) + j * tm
    half = jnp.left_shift(1, grp)
    cnt = (jnp.minimum(t + half, seq_len) - jnp.maximum(t - half, 0)).astype(F32)
    sl = slice(halo, halo + tm)
    wsum = jnp.where(grp == 0, s2[sl], jnp.where(grp == 1, s4[sl],
                                                 jnp.where(grp == 2, s8[sl], s16[sl])))
    pooled = wsum / cnt - ub[sl]
    pool = jnp.dot(pooled.astype(BF16), pw_ref[...], preferred_element_type=F32) * ps_ref[...]

    gy = (yf_ref[...] + yb_ref[...]) * _silu(z_ref[...])
    gw = SSD_INNER // SSD_GROUPS
    parts = []
    for g in range(SSD_GROUPS):
        t_g = gy[:, g * gw:(g + 1) * gw]
        parts.append(t_g * lax.rsqrt(jnp.mean(t_g * t_g, axis=-1, keepdims=True) + EPS))
    ssd = jnp.concatenate(parts, axis=1) * sw_ref[...]

    mix = jnp.concatenate([pool.astype(BF16), ssd.astype(BF16), at_ref[...]], axis=1)
    blocks = [slice(r0, r0 + min(POST_ROWS, tm)) for r0 in range(0, tm, POST_ROWS)]
    x1 = [xs_ref[rows, :] + mod_ref[2:3, :] * jnp.dot(mix[rows], wo_ref[...],
                                                       preferred_element_type=F32)
          for rows in blocks]

    def modulated_norm(v):
        h = v * lax.rsqrt(jnp.mean(v * v, axis=-1, keepdims=True) + EPS) * n2_ref[...]
        return (h * (1.0 + mod_ref[4:5, :]) + mod_ref[3:4, :]).astype(BF16)

    gated = []
    for v in x1:
        h = modulated_norm(v)
        a = jnp.dot(h, w1_ref[...], preferred_element_type=F32)
        b = jnp.dot(h, w3_ref[...], preferred_element_type=F32)
        gated.append((_silu(a) * b).astype(BF16))
    for rows, v, gt in zip(blocks, x1, gated):
        ff = jnp.dot(gt, w2_ref[...], preferred_element_type=F32)
        o_ref[rows, :] = v + mod_ref[5:6, :] * ff


def _post(xs, upool, z, yf, yb, attn, mod, pw_bd, pscale, ssd_w, w_out, n2, w1, w3, w2,
          *, tm, row0, tiles_per_seq, mod_row):
    n_rows, d = xs.shape
    n_tiles = n_rows // tm
    off = row0 // tm
    blk8 = tm // SUBLANES
    last8 = upool.shape[0] // SUBLANES - 1
    tok = lambda w: pl.BlockSpec((tm, w), lambda i: (off + i, 0))
    loc = lambda w: pl.BlockSpec((tm, w), lambda i: (i, 0))
    return pl.pallas_call(
        functools.partial(_post_kernel, tiles_per_seq=tiles_per_seq, seq_len=tiles_per_seq * tm),
        out_shape=jax.ShapeDtypeStruct((n_rows, d), F32),
        grid=(n_tiles,),
        in_specs=[
            loc(d),
            tok(POOL_DIM),
            pl.BlockSpec((SUBLANES, POOL_DIM), lambda i: (jnp.maximum((off + i) * blk8 - 1, 0), 0)),
            pl.BlockSpec((SUBLANES, POOL_DIM),
                         lambda i: (jnp.minimum((off + i) * blk8 + blk8, last8), 0)),
            tok(SSD_INNER), tok(SSD_INNER), tok(SSD_INNER),
            loc(ATTN_DIM),
            pl.BlockSpec((None, N_MOD, d), lambda i: (mod_row(i), 0, 0)),
            _const_spec(pw_bd.shape), _const_spec((1, POOL_DIM)), _const_spec((1, SSD_INNER)),
            _const_spec(w_out.shape), _const_spec((1, d)),
            _const_spec(w1.shape), _const_spec(w3.shape), _const_spec(w2.shape),
        ],
        out_specs=loc(d),
        scratch_shapes=[pltpu.VMEM((tm + 2 * SUBLANES, POOL_DIM), F32)],
        compiler_params=pltpu.CompilerParams(
            dimension_semantics=("arbitrary",), vmem_limit_bytes=VMEM_LIMIT),
        name="post_mix_ffn",
    )(xs, upool, upool, upool, z, yf, yb, attn, mod, pw_bd, pscale, ssd_w, w_out, n2, w1, w3, w2)


def _rope_tables(seq, extra_rows):
    rows, pad_rows = seq // GRID_W, extra_rows // GRID_W
    axis_dim = HEAD_DIM // 2
    inv_freq = ROPE_THETA ** (-jnp.arange(0, axis_dim, 2, dtype=F32) / axis_dim)
    ang_r = jnp.arange(rows, dtype=F32)[:, None] * inv_freq[None, :]
    ang_c = jnp.arange(GRID_W, dtype=F32)[:, None] * inv_freq[None, :]
    live = (jnp.arange(rows + pad_rows) < rows)[:, None, None]
    shape = (rows + pad_rows, GRID_W, axis_dim // 2)

    def per_row(t, fill):
        t = jnp.concatenate([t, jnp.full((pad_rows, t.shape[1]), fill, F32)], axis=0)
        return jnp.broadcast_to(t[:, None, :], shape)

    def per_col(t, fill):
        return jnp.where(live, jnp.broadcast_to(t[None, :, :], shape), fill)

    cr, sr = per_row(jnp.cos(ang_r), 1.0), per_row(jnp.sin(ang_r), 0.0)
    cc, sc = per_col(jnp.cos(ang_c), 1.0), per_col(jnp.sin(ang_c), 0.0)
    cos = jnp.concatenate([cr, cr, cc, cc], axis=-1).reshape(-1, HEAD_DIM)
    sin = jnp.concatenate([-sr, sr, -sc, sc], axis=-1).reshape(-1, HEAD_DIM)
    return cos, sin


def _fuse_w_in(w_in):
    d = w_in.shape[0]
    o_dt = POOL_DIM + SSD_INNER + SSD_XBC
    o_att = o_dt + SSD_DIRS * SSD_HEADS
    dt_pad = jnp.zeros((d, C_Q - C_DT - SSD_DIRS * SSD_HEADS), w_in.dtype)
    return jnp.concatenate([w_in[:, :o_dt], w_in[:, o_dt:o_att], dt_pad, w_in[:, o_att:]],
                           axis=1).astype(BF16)


def kernel(x, c, ctx, c_ctx, norm1_w, norm2_w, w_mod, b_mod, w_in, pool_w, pool_scale, conv_w, conv_b,
           dt_bias, a_log, d_skip, ssd_norm_w, q_norm_w, k_norm_w, w_out, w1, w3, w2):
    batch, seq, d = x.shape
    ctx_len = ctx.shape[1]
    depth = w_mod.shape[0]
    n_lat, n_ctx = batch * seq, batch * ctx_len
    assert batch + 1 <= SUBLANES and seq % TOKEN_TILE == 0 and seq % GRID_W == 0

    cond = jnp.zeros((SUBLANES, d), F32).at[:batch].set(c).at[batch].set(c_ctx)
    mod_all = _modulation(cond, w_mod, b_mod).reshape(depth, SUBLANES, N_MOD, d)
    cos_tab, sin_tab = _rope_tables(seq, TOKEN_TILE)

    xs_lat = x.reshape(n_lat, d)
    xs_ctx = ctx.reshape(n_ctx, d)
    tiles_per_seq = seq // TOKEN_TILE
    for layer in range(depth):
        need_ctx = layer < depth - 1
        mod = mod_all[layer]
        w_cat = _fuse_w_in(w_in[layer])
        upool, z, xbc, dt_raw, q, k, vt = _inproj(
            xs_lat, xs_ctx, mod, norm1_w[layer].reshape(1, d), w_cat,
            q_norm_w[layer].reshape(1, -1), k_norm_w[layer].reshape(1, -1), cos_tab, sin_tab, batch)
        xc = _ssd_conv(xbc, conv_w[layer], conv_b[layer], batch, n_lat, n_ctx)
        yf, yb = _ssd(xc, dt_raw, dt_bias[layer], a_log[layer], d_skip[layer], batch, n_lat, n_ctx)
        score_bound = (HEAD_DIM * Q_SCALE * SCORE_BOUND_SLACK * jnp.max(jnp.abs(q_norm_w[layer]))
                       * jnp.max(jnp.abs(k_norm_w[layer])))
        attn_x = lax.cond(
            score_bound < SCORE_BOUND_LIMIT,
            functools.partial(_attention_latent, batch=batch, n_lat=n_lat, n_ctx=n_ctx, bounded=True),
            functools.partial(_attention_latent, batch=batch, n_lat=n_lat, n_ctx=n_ctx, bounded=False),
            q, k, vt)
        pw_bd = jax.scipy.linalg.block_diag(*[pool_w[layer, g] for g in range(POOL_GROUPS)]).astype(BF16)
        post = functools.partial(
            _post, upool=upool, z=z, yf=yf, yb=yb, mod=mod, pw_bd=pw_bd,
            pscale=pool_scale[layer].reshape(1, -1), ssd_w=ssd_norm_w[layer].reshape(1, -1),
            w_out=w_out[layer].astype(BF16), n2=norm2_w[layer].reshape(1, d),
            w1=w1[layer].astype(BF16), w3=w3[layer].astype(BF16), w2=w2[layer].astype(BF16))
        new_lat = post(xs_lat, attn=attn_x, tm=TOKEN_TILE, row0=0, tiles_per_seq=tiles_per_seq,
                       mod_row=lambda i: i // tiles_per_seq)
        if need_ctx:
            attn_c = _attention_context(q, k, vt, batch, n_lat, n_ctx)
            xs_ctx = post(xs_ctx, attn=attn_c, tm=ctx_len, row0=n_lat, tiles_per_seq=1,
                          mod_row=lambda i: batch)
        xs_lat = new_lat
    return xs_lat.reshape(batch, seq, d)
```

```python
import functools

import jax
import jax.numpy as jnp
from jax import lax
from jax.experimental import pallas as pl
from jax.experimental.pallas import tpu as pltpu

F32 = jnp.float32
BF16 = jnp.bfloat16

GRID_W = 64
EPS = 1e-6
N_MOD = 6
POOL_DIM = 256
POOL_GDIM = 64
POOL_GROUPS = 4
SSD_HEADS = 4
SSD_HEAD_DIM = 64
SSD_INNER = 256
SSD_GROUPS = 2
SSD_STATE = 64
SSD_CONV = 5
SSD_CHUNK = 128
SSD_STEP_CHUNKS = 2
SSD_DIRS = 2
SSD_XBC = 512
ATTN_HEADS = 4
ATTN_KV_HEADS = 2
HEAD_DIM = 128
ATTN_DIM = 512
KV_DIM = 256
ROPE_THETA = 10000.0
Q_SCALE = HEAD_DIM ** -0.5 * 1.4426950408889634
SCORE_BOUND_LIMIT = 60.0
SCORE_BOUND_SLACK = 1.02

LANES = 128
SUBLANES = 8
TOKEN_TILE = 512
INPROJ_ROWS = 256
CONV_TILE = 1024
POST_ROWS = 256
ATTN_Q_TILE = 2048
ATTN_K_TILE = 512
ATTN_UNROLL = 6
ATTN_BOUNDED_UNROLL = 30
BF16_SUBLANES = 16
VT_ROWS = HEAD_DIM + BF16_SUBLANES
VMEM_LIMIT = 56 * 1024 * 1024

C_POOL = 0
C_Z = C_POOL + POOL_DIM
C_XBC = C_Z + SSD_INNER
C_DT = C_XBC + SSD_XBC
C_Q = C_DT + LANES
C_K = C_Q + ATTN_DIM
C_V = C_K + KV_DIM
C_END = C_V + KV_DIM


def _silu(x):
    return x * (1.0 / (1.0 + jnp.exp(-x)))


def _softplus(x):
    return jnp.maximum(x, 0.0) + jnp.log1p(jnp.exp(-jnp.abs(x)))


def _const_spec(shape):
    nd = len(shape)
    return pl.BlockSpec(shape, lambda *_: (0,) * nd, pipeline_mode=pl.Buffered(1))


def _mod_kernel(cond_ref, w_ref, b_ref, o_ref):
    s = _silu(cond_ref[...]).astype(BF16)
    o_ref[...] = jnp.dot(s, w_ref[...].astype(BF16), preferred_element_type=F32) + b_ref[...]


def _modulation(cond, w_mod, b_mod):
    depth, d, n = w_mod.shape
    bn = d
    return pl.pallas_call(
        _mod_kernel,
        out_shape=jax.ShapeDtypeStruct((depth, SUBLANES, n), F32),
        grid=(depth, n // bn),
        in_specs=[
            pl.BlockSpec((SUBLANES, d), lambda l, j: (0, 0)),
            pl.BlockSpec((None, d, bn), lambda l, j: (l, 0, j)),
            pl.BlockSpec((None, 1, bn), lambda l, j: (l, 0, j)),
        ],
        out_specs=pl.BlockSpec((None, SUBLANES, bn), lambda l, j: (l, 0, j)),
        compiler_params=pltpu.CompilerParams(
            dimension_semantics=("arbitrary", "arbitrary"), vmem_limit_bytes=VMEM_LIMIT),
        name="modulation",
    )(cond, w_mod, b_mod.reshape(depth, 1, n))


def _rope_partner(x):
    lane = lax.broadcasted_iota(jnp.int32, x.shape, 1)
    fwd = pltpu.roll(x, LANES - 32, 1)
    bwd = pltpu.roll(x, 32, 1)
    return jnp.where((lane % 64) < 32, fwd, bwd)


def _inproj_kernel(xl_ref, xc_ref, mod_ref, n1_ref, w_ref, qw_ref, kw_ref, cos_ref, sin_ref,
                   pool_ref, z_ref, xbc_ref, dt_ref, q_ref, k_ref, vt_ref, *, n_lat_tiles):
    i = pl.program_id(0)
    is_lat = i < n_lat_tiles
    tm = xl_ref.shape[0]
    for r0 in range(0, tm, INPROJ_ROWS):
        rows = slice(r0, r0 + INPROJ_ROWS)
        x = jnp.where(is_lat, xl_ref[rows, :], xc_ref[rows, :])
        ms = jnp.mean(x * x, axis=-1, keepdims=True)
        h = x * lax.rsqrt(ms + EPS) * n1_ref[...]
        h = h * (1.0 + mod_ref[1:2, :]) + mod_ref[0:1, :]
        u = jnp.dot(h.astype(BF16), w_ref[...], preferred_element_type=F32)
        pool_ref[rows, :] = u[:, C_POOL:C_Z]
        z_ref[rows, :] = u[:, C_Z:C_XBC]
        xbc_ref[rows, :] = u[:, C_XBC:C_DT]
        dt_ref[rows, :] = u[:, C_DT:C_Q]
        vt = u[:, C_V:C_END].T.astype(BF16)
        for hd in range(ATTN_KV_HEADS):
            vt_ref[hd * VT_ROWS:hd * VT_ROWS + HEAD_DIM, rows] = vt[hd * HEAD_DIM:(hd + 1) * HEAD_DIM]
            vt_ref[hd * VT_ROWS + HEAD_DIM:(hd + 1) * VT_ROWS, rows] = jnp.ones(
                (VT_ROWS - HEAD_DIM, INPROJ_ROWS), BF16)
        cos = cos_ref[rows, :]
        sin = sin_ref[rows, :]

        def norm_rope(t, w):
            t = t * lax.rsqrt(jnp.mean(t * t, axis=-1, keepdims=True) + EPS) * w
            return t * cos + _rope_partner(t) * sin

        for hd in range(ATTN_HEADS):
            t = norm_rope(u[:, C_Q + hd * HEAD_DIM:C_Q + (hd + 1) * HEAD_DIM], qw_ref[...])
            q_ref[rows, hd * HEAD_DIM:(hd + 1) * HEAD_DIM] = (t * Q_SCALE).astype(BF16)
        for hd in range(ATTN_KV_HEADS):
            t = norm_rope(u[:, C_K + hd * HEAD_DIM:C_K + (hd + 1) * HEAD_DIM], kw_ref[...])
            k_ref[rows, hd * HEAD_DIM:(hd + 1) * HEAD_DIM] = t.astype(BF16)


def _inproj(xs_lat, xs_ctx, mod, n1, w_cat, qw, kw, cos_tab, sin_tab, batch):
    n_lat, d = xs_lat.shape
    n_ctx = xs_ctx.shape[0]
    tm = TOKEN_TILE
    assert n_ctx == tm and n_lat % (batch * tm) == 0
    n_lat_tiles = n_lat // tm
    tiles_per_seq = n_lat_tiles // batch
    n_tok = n_lat + n_ctx
    grid = (n_lat_tiles + 1,)

    def lat_idx(i):
        return jnp.minimum(i, n_lat_tiles - 1)

    def rope_idx(i):
        return jnp.where(i < n_lat_tiles, i % tiles_per_seq, tiles_per_seq)

    def mod_idx(i):
        return jnp.where(i < n_lat_tiles, i // tiles_per_seq, batch)

    row = lambda w: pl.BlockSpec((tm, w), lambda i: (i, 0))
    outs = [(POOL_DIM, F32), (SSD_INNER, F32), (SSD_XBC, F32), (LANES, F32),
            (ATTN_DIM, BF16), (KV_DIM, BF16)]
    return pl.pallas_call(
        functools.partial(_inproj_kernel, n_lat_tiles=n_lat_tiles),
        out_shape=[jax.ShapeDtypeStruct((n_tok, w), t) for w, t in outs]
        + [jax.ShapeDtypeStruct((n_tok // tm, ATTN_KV_HEADS * VT_ROWS, tm), BF16)],
        grid=grid,
        in_specs=[
            pl.BlockSpec((tm, d), lambda i: (lat_idx(i), 0)),
            pl.BlockSpec((tm, d), lambda i: (0, 0)),
            pl.BlockSpec((None, N_MOD, d), lambda i: (mod_idx(i), 0, 0)),
            _const_spec((1, d)),
            _const_spec(w_cat.shape),
            _const_spec((1, HEAD_DIM)),
            _const_spec((1, HEAD_DIM)),
            pl.BlockSpec((tm, HEAD_DIM), lambda i: (rope_idx(i), 0)),
            pl.BlockSpec((tm, HEAD_DIM), lambda i: (rope_idx(i), 0)),
        ],
        out_specs=[row(w) for w, _ in outs]
        + [pl.BlockSpec((None, ATTN_KV_HEADS * VT_ROWS, tm), lambda i: (i, 0, 0))],
        compiler_params=pltpu.CompilerParams(
            dimension_semantics=("arbitrary",), vmem_limit_bytes=VMEM_LIMIT),
        name="inproj",
    )(xs_lat, xs_ctx, mod, n1, w_cat, qw, kw, cos_tab, sin_tab)


def _conv_kernel(x_ref, prev_ref, next_ref, ctx_ref, cw_ref, cb_ref, o_ref, buf_ref,
                 *, tiles_per_seq, n_lat_tiles, ctx_len):
    i = pl.program_id(0)
    tm = x_ref.shape[0]
    pad = SSD_CONV // 2

    def conv_silu(rows):
        u = buf_ref[0:rows + 2 * SUBLANES, :]
        n = u.shape[0]
        acc = cb_ref[...] + cw_ref[pad:pad + 1, :] * u[SUBLANES:SUBLANES + rows]
        for k in range(SSD_CONV):
            if k != pad:
                shifted = pltpu.roll(u, (pad - k) % n, 0)
                acc = acc + cw_ref[k:k + 1, :] * shifted[SUBLANES:SUBLANES + rows]
        return _silu(acc)

    @pl.when(i < n_lat_tiles)
    def _():
        first = i % tiles_per_seq == 0
        last = i % tiles_per_seq == tiles_per_seq - 1
        buf_ref[0:SUBLANES, :] = jnp.where(first, 0.0, prev_ref[...])
        buf_ref[SUBLANES:SUBLANES + tm, :] = x_ref[...]
        buf_ref[SUBLANES + tm:2 * SUBLANES + tm, :] = jnp.where(last, 0.0, next_ref[...])
        o_ref[...] = conv_silu(tm)

    @pl.when(i == n_lat_tiles)
    def _():
        n_ctx = ctx_ref.shape[0]
        halo = jnp.zeros((SUBLANES, SSD_XBC), F32)
        for r0 in range(0, n_ctx, ctx_len):
            buf_ref[0:SUBLANES, :] = halo
            buf_ref[SUBLANES:SUBLANES + ctx_len, :] = ctx_ref[r0:r0 + ctx_len, :]
            buf_ref[SUBLANES + ctx_len:2 * SUBLANES + ctx_len, :] = halo
            o_ref[r0:r0 + ctx_len, :] = conv_silu(ctx_len)
        o_ref[n_ctx:tm, :] = jnp.zeros((tm - n_ctx, SSD_XBC), F32)


def _ssd_conv(xbc, conv_w, conv_b, batch, n_lat, n_ctx):
    n_tok = xbc.shape[0]
    seq, ctx = n_lat // batch, n_ctx // batch
    tm = CONV_TILE
    assert seq % tm == 0 and n_lat % n_ctx == 0 and n_ctx < tm
    n_lat_tiles = n_lat // tm
    blk8 = tm // SUBLANES
    lat = lambda i: jnp.minimum(i, n_lat_tiles - 1)
    cw = jnp.zeros((SUBLANES, SSD_XBC), F32).at[:SSD_CONV].set(conv_w)
    return pl.pallas_call(
        functools.partial(_conv_kernel, tiles_per_seq=seq // tm, n_lat_tiles=n_lat_tiles,
                          ctx_len=ctx),
        out_shape=jax.ShapeDtypeStruct((n_tok, SSD_XBC), F32),
        grid=(n_lat_tiles + 1,),
        in_specs=[
            pl.BlockSpec((tm, SSD_XBC), lambda i: (lat(i), 0)),
            pl.BlockSpec((SUBLANES, SSD_XBC), lambda i: (jnp.maximum(lat(i) * blk8 - 1, 0), 0)),
            pl.BlockSpec((SUBLANES, SSD_XBC), lambda i: (lat(i) * blk8 + blk8, 0)),
            pl.BlockSpec((n_ctx, SSD_XBC), lambda i: (n_lat // n_ctx, 0)),
            _const_spec((SUBLANES, SSD_XBC)), _const_spec((1, SSD_XBC)),
        ],
        out_specs=pl.BlockSpec((tm, SSD_XBC), lambda i: (i, 0)),
        scratch_shapes=[pltpu.VMEM((tm + 2 * SUBLANES, SSD_XBC), F32)],
        compiler_params=pltpu.CompilerParams(
            dimension_semantics=("arbitrary",), vmem_limit_bytes=VMEM_LIMIT),
        name="ssd_conv",
    )(xbc, xbc, xbc, xbc, cw, conv_b.reshape(1, -1))


def _ssd_kernel(xf_ref, dtf_ref, dtf_next_ref, xb_ref, dtb_ref, dtb_next_ref,
                bias_ref, alog_ref, dsk_ref, spread_ref, ecol_ref,
                yf_ref, yb_ref, h_ref, et_sc, spread_sc, ecols_sc):
    i = pl.program_id(1)
    q = SSD_CHUNK
    n_sub = SSD_STEP_CHUNKS
    blk = n_sub * q

    @pl.when(i == 0)
    def _():
        h_ref[...] = jnp.zeros_like(h_ref)

    row = lax.broadcasted_iota(jnp.int32, (q, q), 0)
    col = lax.broadcasted_iota(jnp.int32, (q, q), 1)
    n_col = SSD_DIRS * SSD_HEADS
    heads_per_group = SSD_HEADS // SSD_GROUPS
    pair = heads_per_group * SSD_HEAD_DIM
    lane_half = col // SSD_HEAD_DIM
    st_row = lax.broadcasted_iota(jnp.int32, (SSD_GROUPS * SSD_STATE, SSD_INNER), 0)
    st_col = lax.broadcasted_iota(jnp.int32, (SSD_GROUPS * SSD_STATE, SSD_INNER), 1)
    same_group = (st_row // SSD_STATE) == (st_col // pair)
    contract0 = (((0,), (0,)), ((), ()))

    def bf16_terms(v, n_terms):
        terms, rest = [], v
        for _ in range(n_terms):
            t = rest.astype(BF16).astype(F32)
            terms.append(t)
            rest = rest - t
        rows = -(-n_terms * n_col // BF16_SUBLANES) * BF16_SUBLANES
        if rows > n_terms * n_col:
            terms.append(jnp.zeros((rows - n_terms * n_col, v.shape[1]), F32))
        return jnp.concatenate(terms, axis=0).astype(BF16)

    fwd_rows = lax.broadcasted_iota(jnp.int32, (n_col, blk), 0) < SSD_HEADS

    def token_steps(dtf_blk_ref, dtb_blk_ref):
        dt_raw = jnp.where(fwd_rows, dtf_blk_ref[...].T[0:n_col], dtb_blk_ref[...].T[0:n_col])
        dt = _softplus(dt_raw + bias_ref[...])
        return dt, dt * -jnp.exp(alog_ref[...])

    def token_decay_sums(dt, a):
        r = lax.broadcasted_iota(jnp.int32, (blk, blk), 0)
        c = lax.broadcasted_iota(jnp.int32, (blk, blk), 1)
        tri_u = jnp.logical_and(r <= c, r // q == c // q).astype(BF16)
        parts = jnp.dot(bf16_terms(a, 3), tri_u, preferred_element_type=F32)
        cs = parts[0:n_col] + parts[n_col:2 * n_col] + parts[2 * n_col:3 * n_col]
        return dt, a, cs

    def token_scalars(dt, a, cs):
        tot = jnp.concatenate([jnp.broadcast_to(cs[:, (u + 1) * q - 1:(u + 1) * q], (n_col, q))
                               for u in range(n_sub)], axis=1)
        e_t = jnp.where(fwd_rows, cs, cs - a)
        grow, shrink = jnp.exp(e_t), jnp.exp(tot - e_t)
        w_in = jnp.where(fwd_rows, grow, shrink)
        w_out = jnp.where(fwd_rows, shrink, grow)
        terms = jnp.concatenate(
            [bf16_terms(dt, 2), bf16_terms(w_in, 2), bf16_terms(dt * w_out, 2)], axis=1)
        spread = lax.dot_general(terms, spread_ref[...], contract0,
                                 preferred_element_type=F32)
        e_cols = lax.dot_general(bf16_terms(e_t, 3), ecol_ref[...], contract0,
                                 preferred_element_type=F32)
        return e_t, spread, e_cols

    def into_scratch(vals):
        et_sc[...], spread_sc[...], ecols_sc[...] = vals

    @pl.when(i == 0)
    def _():
        into_scratch(token_scalars(*token_decay_sums(*token_steps(dtf_ref, dtb_ref))))

    next_sums = token_decay_sums(*token_steps(dtf_next_ref, dtb_next_ref))

    zero = jnp.zeros((), BF16)

    def input_matmuls(d, u, xc):
        ch = slice(d * SSD_INNER, (d + 1) * SSD_INNER)
        tok = slice(u * q, (u + 1) * q)
        dt_x, w_in_x, dtw_x = (spread_sc[k * blk + u * q:k * blk + (u + 1) * q, ch] for k in range(3))
        x = xc[tok, 0:SSD_INNER]
        xdt = (x * dt_x).astype(BF16)
        xdtw = (x * dtw_x).astype(BF16)
        b_all = xc[tok, SSD_INNER:SSD_INNER + SSD_GROUPS * SSD_STATE].astype(BF16)
        c_all = xc[tok, SSD_INNER + SSD_GROUPS * SSD_STATE:].astype(BF16)
        hs = h_ref[d]
        y_off = jnp.dot(c_all, hs.astype(BF16), preferred_element_type=F32) * w_in_x
        cbs = [lax.dot_general(jnp.where(lane_half == g, c_all, zero), b_all,
                               (((1,), (1,)), ((), ())), preferred_element_type=F32)
               for g in range(SSD_GROUPS)]
        upd = lax.dot_general(b_all, xdtw, contract0, preferred_element_type=F32)
        exp_tot = w_in_x[q - 1:q, :] if d == 0 else w_in_x[0:1, :]
        h_ref[d] = exp_tot * hs + jnp.where(same_group, upd, 0.0)
        return x, xdt, y_off, cbs

    def decay_matmuls(d, u, x, xdt, y_off, cbs, y_ref):
        tok = slice(u * q, (u + 1) * q)
        for g in range(SSD_GROUPS):
            sl = slice(g * pair, (g + 1) * pair)
            y_g = y_off[:, sl] + dsk_ref[d:d + 1, sl] * x[:, sl]
            for hh in range(heads_per_group):
                c = d * SSD_HEADS + g * heads_per_group + hh
                e_col = ecols_sc[tok, c * q:(c + 1) * q]
                e_row = et_sc[c:c + 1, tok]
                if d == 0:
                    lmat = jnp.where(row >= col, jnp.exp(e_col - e_row), 0.0)
                else:
                    lmat = jnp.where(col >= row, jnp.exp(e_row - e_col), 0.0)
                x_h = jnp.where(lane_half == hh, xdt[:, sl], zero)
                y_g = y_g + jnp.dot((cbs[g] * lmat).astype(BF16), x_h,
                                    preferred_element_type=F32)
            y_ref[tok, sl] = y_g

    refs = ((xf_ref, yf_ref), (xb_ref, yb_ref))
    order = (tuple(range(n_sub)), tuple(reversed(range(n_sub))))
    lin = {}
    for k in range(n_sub):
        for d in range(SSD_DIRS):
            u = order[d][k]
            lin[d, u] = input_matmuls(d, u, refs[d][0])
    for k in range(n_sub):
        for d in range(SSD_DIRS):
            u = order[d][k]
            decay_matmuls(d, u, *lin[d, u], refs[d][1])
    into_scratch(token_scalars(*next_sums))


def _ssd(xc, dt_raw, dt_bias, a_log, d_skip, batch, n_lat, n_ctx):
    n_tok = xc.shape[0]
    q = SSD_CHUNK
    blk = SSD_STEP_CHUNKS * q
    lat_blk = n_lat // batch // blk
    ctx_blk = n_ctx // batch // blk
    assert lat_blk * blk * batch == n_lat and ctx_blk * blk * batch == n_ctx
    n_steps = lat_blk + ctx_blk

    def fwd(b, i):
        return jnp.where(i < ctx_blk, batch * lat_blk + b * ctx_blk + i, b * lat_blk + i - ctx_blk)

    def bwd(b, i):
        return jnp.where(i < ctx_blk, batch * lat_blk + b * ctx_blk + ctx_blk - 1 - i,
                         b * lat_blk + n_steps - 1 - i)

    def specs(ch):
        return [
            pl.BlockSpec((blk, SSD_XBC), lambda b, i: (ch(b, i), 0)),
            pl.BlockSpec((blk, LANES), lambda b, i: (ch(b, i), 0)),
            pl.BlockSpec((blk, LANES), lambda b, i: (ch(b, jnp.minimum(i + 1, n_steps - 1)), 0)),
        ]

    n_col = SSD_DIRS * SSD_HEADS
    rows = lambda v: jnp.broadcast_to(v.reshape(n_col, 1), (n_col, blk))
    dsk = jnp.repeat(d_skip, SSD_HEAD_DIM, axis=1)
    term_row = jnp.arange(2 * BF16_SUBLANES) % n_col
    live3 = jnp.arange(2 * BF16_SUBLANES) < 3 * n_col
    col_of_channel = jnp.arange(SSD_DIRS * SSD_INNER) // SSD_HEAD_DIM
    col_of_block = jnp.arange(n_col * q) // q
    spread = (term_row[:BF16_SUBLANES, None] == col_of_channel[None, :]).astype(BF16)
    ecol = ((term_row[:, None] == col_of_block[None, :]) & live3[:, None]).astype(BF16)
    return pl.pallas_call(
        _ssd_kernel,
        out_shape=[jax.ShapeDtypeStruct((n_tok, SSD_INNER), F32)] * 2,
        grid=(batch, n_steps),
        in_specs=specs(fwd) + specs(bwd) + [
            _const_spec((n_col, blk)), _const_spec((n_col, blk)), _const_spec((SSD_DIRS, SSD_INNER)),
            _const_spec(spread.shape), _const_spec(ecol.shape),
        ],
        out_specs=[pl.BlockSpec((blk, SSD_INNER), lambda b, i: (fwd(b, i), 0)),
                   pl.BlockSpec((blk, SSD_INNER), lambda b, i: (bwd(b, i), 0))],
        scratch_shapes=[pltpu.VMEM((SSD_DIRS, SSD_GROUPS * SSD_STATE, SSD_INNER), F32),
                        pltpu.VMEM((n_col, blk), F32),
                        pltpu.VMEM((3 * blk, SSD_DIRS * SSD_INNER), F32),
                        pltpu.VMEM((blk, n_col * q), F32)],
        compiler_params=pltpu.CompilerParams(
            dimension_semantics=("arbitrary", "arbitrary"), vmem_limit_bytes=VMEM_LIMIT),
        name="ssd_scan",
    )(xc, dt_raw, dt_raw, xc, dt_raw, dt_raw,
      rows(dt_bias), rows(a_log), dsk, spread, ecol)


def _steps_per_trip(n_steps, preferred):
    return max(u for u in range(2, preferred + 1, 2) if n_steps % u == 0) if n_steps else 2


def _scores_t(k, q2):
    return lax.dot_general(k, q2, (((1,), (1,)), ((), ())), preferred_element_type=F32)


def _softmax_stage(s, m_blk, m):
    if m is None:
        return m_blk, None, jnp.exp2((s - m_blk).astype(BF16))
    m_new = jnp.maximum(m, m_blk)
    return m_new, jnp.exp2(m - m_new), jnp.exp2((s - m_new).astype(BF16))


def _attn_kernel(q_ref, kc_ref, vtc_ref, kx_ref, vtx_ref, o_ref, s0_ref, s1_ref, p0_ref, p1_ref,
                 *, n_chunks):
    tq = q_ref.shape[0]
    tk = ATTN_K_TILE
    q2 = jnp.concatenate([q_ref[:, 0:HEAD_DIM], q_ref[:, HEAD_DIM:2 * HEAD_DIM]], axis=0)
    s_refs, p_refs = (s0_ref, s1_ref), (p0_ref, p1_ref)

    def scores_into(slot, c):
        s = _scores_t(kx_ref[pl.ds(pl.multiple_of(c * tk, tk), tk), :], q2)
        s_refs[slot][...] = s
        return jnp.max(s, axis=0, keepdims=True)

    def softmax_into(slot, m_blk, m):
        m, alpha, p = _softmax_stage(s_refs[slot][...], m_blk, m)
        p_refs[slot][...] = p
        return m, alpha

    def step(c, par, carry, with_scores, with_softmax):
        m, acc, alpha, m_blk = carry
        m_blk_next, alpha_next = m_blk, alpha
        if with_scores:
            m_blk_next = scores_into(par, c + 2)
        if with_softmax:
            m, alpha_next = softmax_into(1 - par, m_blk, m)
        acc = alpha * acc + jnp.dot(vtx_ref[c], p_refs[par][...], preferred_element_type=F32)
        return m, acc, alpha_next, m_blk_next

    if n_chunks:
        m_blk0 = scores_into(0, 0)
        m_blk1 = scores_into(1, 1)
    s_ctx = _scores_t(kc_ref[...], q2)
    m, _, p_ctx = _softmax_stage(s_ctx, jnp.max(s_ctx, axis=0, keepdims=True), None)
    acc = jnp.dot(vtc_ref[...], p_ctx, preferred_element_type=F32)
    if n_chunks:
        m, alpha = softmax_into(0, m_blk0, m)
        carry = (m, acc, alpha, m_blk1)
        unroll = _steps_per_trip(n_chunks - 2, ATTN_UNROLL)

        def body(i, carry):
            for u in range(unroll):
                carry = step(unroll * i + u, u % 2, carry, True, True)
            return carry

        carry = lax.fori_loop(0, (n_chunks - 2) // unroll, body, carry)
        carry = step(n_chunks - 2, 0, carry, False, True)
        _, acc, _, _ = step(n_chunks - 1, 1, carry, False, False)
    o = (acc[0:HEAD_DIM] / acc[HEAD_DIM:HEAD_DIM + 1]).T.astype(o_ref.dtype)
    o_ref[:, 0:HEAD_DIM] = o[0:tq]
    o_ref[:, HEAD_DIM:2 * HEAD_DIM] = o[tq:2 * tq]


def _attn_bounded_kernel(q_ref, kc_ref, vtc_ref, kx_ref, vtx_ref, o_ref, p0_ref, p1_ref, *, n_chunks):
    tq = q_ref.shape[0]
    tk = ATTN_K_TILE
    q2 = jnp.concatenate([q_ref[:, 0:HEAD_DIM], q_ref[:, HEAD_DIM:2 * HEAD_DIM]], axis=0)
    p_refs = (p0_ref, p1_ref)

    def colsum(p):
        return jnp.sum(p.reshape(p.shape[0] // SUBLANES, SUBLANES, p.shape[1]), axis=0)

    def weights_into(slot, c):
        p = jnp.exp2(_scores_t(kx_ref[pl.ds(pl.multiple_of(c * tk, tk), tk), :], q2))
        p_refs[slot][...] = p.astype(BF16)
        return colsum(p)

    def step(c, par, carry, with_scores):
        l8, acc = carry
        if with_scores:
            l8 = l8 + weights_into(1 - par, c + 1)
        acc = acc + jnp.dot(vtx_ref[c, 0:HEAD_DIM, :], p_refs[par][...],
                            preferred_element_type=F32)
        return l8, acc

    p_ctx = jnp.exp2(_scores_t(kc_ref[...], q2))
    l8 = colsum(p_ctx) + weights_into(0, 0)
    acc = jnp.dot(vtc_ref[0:HEAD_DIM, :], p_ctx.astype(BF16), preferred_element_type=F32)
    unroll = _steps_per_trip(n_chunks - 2, ATTN_BOUNDED_UNROLL)

    def body(i, carry):
        for u in range(unroll):
            carry = step(unroll * i + u, u % 2, carry, True)
        return carry

    carry = lax.fori_loop(0, (n_chunks - 2) // unroll, body, (l8, acc))
    carry = step(n_chunks - 2, 0, carry, True)
    l8, acc = step(n_chunks - 1, 1, carry, False)
    o = (acc / jnp.sum(l8, axis=0, keepdims=True)).T.astype(o_ref.dtype)
    o_ref[:, 0:HEAD_DIM] = o[0:tq]
    o_ref[:, HEAD_DIM:2 * HEAD_DIM] = o[tq:2 * tq]


def _attention_latent(q, k, vt, batch, n_lat, n_ctx, bounded):
    seq = n_lat // batch
    ctx = n_ctx // batch
    tq, tk = ATTN_Q_TILE, ATTN_K_TILE
    nq = seq // tq
    n_chunks = seq // tk
    assert seq % (2 * tk) == 0 and n_lat % ctx == 0 and n_ctx == tk and vt.shape[2] == tk
    r = 2 * tq
    body = _attn_bounded_kernel if bounded else _attn_kernel
    p_bufs = [pltpu.VMEM((tk, r), BF16), pltpu.VMEM((tk, r), BF16)]
    s_bufs = [] if bounded else [pltpu.VMEM((tk, r), F32), pltpu.VMEM((tk, r), F32)]
    return pl.pallas_call(
        functools.partial(body, n_chunks=n_chunks),
        out_shape=jax.ShapeDtypeStruct((n_lat, ATTN_DIM), BF16),
        grid=(batch, ATTN_KV_HEADS, nq),
        in_specs=[pl.BlockSpec((tq, 2 * HEAD_DIM), lambda b, g, j: (b * nq + j, g)),
                  pl.BlockSpec((ctx, HEAD_DIM), lambda b, g, j: (n_lat // ctx + b, g)),
                  pl.BlockSpec((None, VT_ROWS, ctx), lambda b, g, j: (n_lat // tk, g, b)),
                  pl.BlockSpec((seq, HEAD_DIM), lambda b, g, j: (b, g)),
                  pl.BlockSpec((n_chunks, VT_ROWS, tk), lambda b, g, j: (b, g, 0))],
        out_specs=pl.BlockSpec((tq, 2 * HEAD_DIM), lambda b, g, j: (b * nq + j, g)),
        scratch_shapes=s_bufs + p_bufs,
        compiler_params=pltpu.CompilerParams(
            dimension_semantics=("arbitrary",) * 3, vmem_limit_bytes=VMEM_LIMIT),
        name="attn_latent_bounded" if bounded else "attn_latent",
    )(q, k, vt, k, vt)


def _attn_ctx_kernel(q_ref, kc_ref, vtc_ref, o_ref):
    _attn_kernel(q_ref, kc_ref, vtc_ref, None, None, o_ref, None, None, None, None, n_chunks=0)


def _attention_context(q, k, vt, batch, n_lat, n_ctx):
    ctx = n_ctx // batch
    tk = ATTN_K_TILE
    return pl.pallas_call(
        _attn_ctx_kernel,
        out_shape=jax.ShapeDtypeStruct((n_ctx, ATTN_DIM), BF16),
        grid=(batch, ATTN_KV_HEADS),
        in_specs=[pl.BlockSpec((ctx, 2 * HEAD_DIM), lambda b, g: (n_lat // ctx + b, g)),
                  pl.BlockSpec((ctx, HEAD_DIM), lambda b, g: (n_lat // ctx + b, g)),
                  pl.BlockSpec((None, VT_ROWS, ctx), lambda b, g: (n_lat // tk, g, b))],
        out_specs=pl.BlockSpec((ctx, 2 * HEAD_DIM), lambda b, g: (b, g)),
        compiler_params=pltpu.CompilerParams(
            dimension_semantics=("arbitrary",) * 2, vmem_limit_bytes=VMEM_LIMIT),
        name="attn_context",
    )(q, k, vt)


def _post_kernel(xs_ref, up_ref, upp_ref, upn_ref, z_ref, yf_ref, yb_ref, at_ref, mod_ref,
                 pw_ref, ps_ref, sw_ref, wo_ref, n2_ref, w1_ref, w3_ref, w2_ref,
                 o_ref, buf_ref, *, tiles_per_seq, seq_len):
    i = pl.program_id(0)
    tm = xs_ref.shape[0]
    halo = SUBLANES
    j = i % tiles_per_seq

    buf_ref[0:halo, :] = jnp.where(j == 0, 0.0, upp_ref[...])
    buf_ref[halo:halo + tm, :] = up_ref[...]
    buf_ref[halo + tm:2 * halo + tm, :] = jnp.where(j == tiles_per_seq - 1, 0.0, upn_ref[...])
    ub = buf_ref[...]
    n = tm + 2 * halo
    s2 = ub + pltpu.roll(ub, 1, 0)
    s4 = pltpu.roll(s2, 1, 0) + pltpu.roll(s2, n - 1, 0)
    s8 = pltpu.roll(s4, 2, 0) + pltpu.roll(s4, n - 2, 0)
    s16 = pltpu.roll(s8, 4, 0) + pltpu.roll(s8, n - 4, 0)
    shape = (tm, POOL_DIM)
    grp = lax.broadcasted_iota(jnp.int32, shape, 1) // POOL_GDIM
    t = lax.broadcasted_iota(jnp.int32, shape, 0) + j * tm
    half = jnp.left_shift(1, grp)
    cnt = (jnp.minimum(t + half, seq_len) - jnp.maximum(t - half, 0)).astype(F32)
    sl = slice(halo, halo + tm)
    wsum = jnp.where(grp == 0, s2[sl], jnp.where(grp == 1, s4[sl],
                                                 jnp.where(grp == 2, s8[sl], s16[sl])))
    pooled = wsum / cnt - ub[sl]
    pool = jnp.dot(pooled.astype(BF16), pw_ref[...], preferred_element_type=F32) * ps_ref[...]

    gy = (yf_ref[...] + yb_ref[...]) * _silu(z_ref[...])
    gw = SSD_INNER // SSD_GROUPS
    parts = []
    for g in range(SSD_GROUPS):
        t_g = gy[:, g * gw:(g + 1) * gw]
        parts.append(t_g * lax.rsqrt(jnp.mean(t_g * t_g, axis=-1, keepdims=True) + EPS))
    ssd = jnp.concatenate(parts, axis=1) * sw_ref[...]

    mix = jnp.concatenate([pool.astype(BF16), ssd.astype(BF16), at_ref[...]], axis=1)
    blocks = [slice(r0, r0 + min(POST_ROWS, tm)) for r0 in range(0, tm, POST_ROWS)]
    x1 = [xs_ref[rows, :] + mod_ref[2:3, :] * jnp.dot(mix[rows], wo_ref[...],
                                                       preferred_element_type=F32)
          for rows in blocks]

    def modulated_norm(v):
        h = v * lax.rsqrt(jnp.mean(v * v, axis=-1, keepdims=True) + EPS) * n2_ref[...]
        return (h * (1.0 + mod_ref[4:5, :]) + mod_ref[3:4, :]).astype(BF16)

    gated = []
    for v in x1:
        h = modulated_norm(v)
        a = jnp.dot(h, w1_ref[...], preferred_element_type=F32)
        b = jnp.dot(h, w3_ref[...], preferred_element_type=F32)
        gated.append((_silu(a) * b).astype(BF16))
    for rows, v, gt in zip(blocks, x1, gated):
        ff = jnp.dot(gt, w2_ref[...], preferred_element_type=F32)
        o_ref[rows, :] = v + mod_ref[5:6, :] * ff


def _post(xs, upool, z, yf, yb, attn, mod, pw_bd, pscale, ssd_w, w_out, n2, w1, w3, w2,
          *, tm, row0, tiles_per_seq, mod_row):
    n_rows, d = xs.shape
    n_tiles = n_rows // tm
    off = row0 // tm
    blk8 = tm // SUBLANES
    last8 = upool.shape[0] // SUBLANES - 1
    tok = lambda w: pl.BlockSpec((tm, w), lambda i: (off + i, 0))
    loc = lambda w: pl.BlockSpec((tm, w), lambda i: (i, 0))
    return pl.pallas_call(
        functools.partial(_post_kernel, tiles_per_seq=tiles_per_seq, seq_len=tiles_per_seq * tm),
        out_shape=jax.ShapeDtypeStruct((n_rows, d), F32),
        grid=(n_tiles,),
        in_specs=[
            loc(d),
            tok(POOL_DIM),
            pl.BlockSpec((SUBLANES, POOL_DIM), lambda i: (jnp.maximum((off + i) * blk8 - 1, 0), 0)),
            pl.BlockSpec((SUBLANES, POOL_DIM),
                         lambda i: (jnp.minimum((off + i) * blk8 + blk8, last8), 0)),
            tok(SSD_INNER), tok(SSD_INNER), tok(SSD_INNER),
            loc(ATTN_DIM),
            pl.BlockSpec((None, N_MOD, d), lambda i: (mod_row(i), 0, 0)),
            _const_spec(pw_bd.shape), _const_spec((1, POOL_DIM)), _const_spec((1, SSD_INNER)),
            _const_spec(w_out.shape), _const_spec((1, d)),
            _const_spec(w1.shape), _const_spec(w3.shape), _const_spec(w2.shape),
        ],
        out_specs=loc(d),
        scratch_shapes=[pltpu.VMEM((tm + 2 * SUBLANES, POOL_DIM), F32)],
        compiler_params=pltpu.CompilerParams(
            dimension_semantics=("arbitrary",), vmem_limit_bytes=VMEM_LIMIT),
        name="post_mix_ffn",
    )(xs, upool, upool, upool, z, yf, yb, attn, mod, pw_bd, pscale, ssd_w, w_out, n2, w1, w3, w2)


def _rope_tables(seq, extra_rows):
    rows, pad_rows = seq // GRID_W, extra_rows // GRID_W
    axis_dim = HEAD_DIM // 2
    inv_freq = ROPE_THETA ** (-jnp.arange(0, axis_dim, 2, dtype=F32) / axis_dim)
    ang_r = jnp.arange(rows, dtype=F32)[:, None] * inv_freq[None, :]
    ang_c = jnp.arange(GRID_W, dtype=F32)[:, None] * inv_freq[None, :]
    live = (jnp.arange(rows + pad_rows) < rows)[:, None, None]
    shape = (rows + pad_rows, GRID_W, axis_dim // 2)

    def per_row(t, fill):
        t = jnp.concatenate([t, jnp.full((pad_rows, t.shape[1]), fill, F32)], axis=0)
        return jnp.broadcast_to(t[:, None, :], shape)

    def per_col(t, fill):
        return jnp.where(live, jnp.broadcast_to(t[None, :, :], shape), fill)

    cr, sr = per_row(jnp.cos(ang_r), 1.0), per_row(jnp.sin(ang_r), 0.0)
    cc, sc = per_col(jnp.cos(ang_c), 1.0), per_col(jnp.sin(ang_c), 0.0)
    cos = jnp.concatenate([cr, cr, cc, cc], axis=-1).reshape(-1, HEAD_DIM)
    sin = jnp.concatenate([-sr, sr, -sc, sc], axis=-1).reshape(-1, HEAD_DIM)
    return cos, sin


def _fuse_w_in(w_in):
    d = w_in.shape[0]
    o_dt = POOL_DIM + SSD_INNER + SSD_XBC
    o_att = o_dt + SSD_DIRS * SSD_HEADS
    dt_pad = jnp.zeros((d, C_Q - C_DT - SSD_DIRS * SSD_HEADS), w_in.dtype)
    return jnp.concatenate([w_in[:, :o_dt], w_in[:, o_dt:o_att], dt_pad, w_in[:, o_att:]],
                           axis=1).astype(BF16)


def kernel(x, c, ctx, c_ctx, norm1_w, norm2_w, w_mod, b_mod, w_in, pool_w, pool_scale, conv_w, conv_b,
           dt_bias, a_log, d_skip, ssd_norm_w, q_norm_w, k_norm_w, w_out, w1, w3, w2):
    batch, seq, d = x.shape
    ctx_len = ctx.shape[1]
    depth = w_mod.shape[0]
    n_lat, n_ctx = batch * seq, batch * ctx_len
    assert batch + 1 <= SUBLANES and seq % TOKEN_TILE == 0 and seq % GRID_W == 0

    cond = jnp.zeros((SUBLANES, d), F32).at[:batch].set(c).at[batch].set(c_ctx)
    mod_all = _modulation(cond, w_mod, b_mod).reshape(depth, SUBLANES, N_MOD, d)
    cos_tab, sin_tab = _rope_tables(seq, TOKEN_TILE)

    xs_lat = x.reshape(n_lat, d)
    xs_ctx = ctx.reshape(n_ctx, d)
    tiles_per_seq = seq // TOKEN_TILE
    for layer in range(depth):
        need_ctx = layer < depth - 1
        mod = mod_all[layer]
        w_cat = _fuse_w_in(w_in[layer])
        upool, z, xbc, dt_raw, q, k, vt = _inproj(
            xs_lat, xs_ctx, mod, norm1_w[layer].reshape(1, d), w_cat,
            q_norm_w[layer].reshape(1, -1), k_norm_w[layer].reshape(1, -1), cos_tab, sin_tab, batch)
        xc = _ssd_conv(xbc, conv_w[layer], conv_b[layer], batch, n_lat, n_ctx)
        yf, yb = _ssd(xc, dt_raw, dt_bias[layer], a_log[layer], d_skip[layer], batch, n_lat, n_ctx)
        score_bound = (HEAD_DIM * Q_SCALE * SCORE_BOUND_SLACK * jnp.max(jnp.abs(q_norm_w[layer]))
                       * jnp.max(jnp.abs(k_norm_w[layer])))
        attn_x = lax.cond(
            score_bound < SCORE_BOUND_LIMIT,
            functools.partial(_attention_latent, batch=batch, n_lat=n_lat, n_ctx=n_ctx, bounded=True),
            functools.partial(_attention_latent, batch=batch, n_lat=n_lat, n_ctx=n_ctx, bounded=False),
            q, k, vt)
        pw_bd = jax.scipy.linalg.block_diag(*[pool_w[layer, g] for g in range(POOL_GROUPS)]).astype(BF16)
        post = functools.partial(
            _post, upool=upool, z=z, yf=yf, yb=yb, mod=mod, pw_bd=pw_bd,
            pscale=pool_scale[layer].reshape(1, -1), ssd_w=ssd_norm_w[layer].reshape(1, -1),
            w_out=w_out[layer].astype(BF16), n2=norm2_w[layer].reshape(1, d),
            w1=w1[layer].astype(BF16), w3=w3[layer].astype(BF16), w2=w2[layer].astype(BF16))
        new_lat = post(xs_lat, attn=attn_x, tm=TOKEN_TILE, row0=0, tiles_per_seq=tiles_per_seq,
                       mod_row=lambda i: i // tiles_per_seq)
        if need_ctx:
            attn_c = _attention_context(q, k, vt, batch, n_lat, n_ctx)
            xs_ctx = post(xs_ctx, attn=attn_c, tm=ctx_len, row0=n_lat, tiles_per_seq=1,
                          mod_row=lambda i: batch)
        xs_lat = new_lat
    return xs_lat.reshape(batch, seq, d)
```
